```python
import math
import jax, jax.numpy as jnp
from jax import lax
import numpy as np

D_MODEL = 1024
BATCH = 32
SEQ = 256
DEPTH = 2
DEC_BATCH = 8
DEC_SEQ = 1024
PAST_LEN = 512

GRID_W = 64
HEAD_DIM = 64
Q_BLOCK = 128
EPS = 1e-6
ROPE_BASE = 10000.0

LRU_WIDTH = 256
LRU_BLOCKS = 4
LRU_BW = LRU_WIDTH // LRU_BLOCKS
LRU_C = 8.0
CONV_W = 4
CONV_LEFT = 2

GQA_Q_HEADS = 4
GQA_KV_HEADS = 2
GQA_WIDTH = GQA_Q_HEADS * HEAD_DIM
GQA_KV_WIDTH = GQA_KV_HEADS * HEAD_DIM

DIFF_HEADS = 4
DIFF_WIDTH = DIFF_HEADS * 2 * HEAD_DIM

D_MIX = LRU_WIDTH + GQA_WIDTH + DIFF_WIDTH
IN_SIZES = (LRU_WIDTH, LRU_WIDTH, GQA_WIDTH, GQA_KV_WIDTH, GQA_KV_WIDTH, GQA_WIDTH,
            DIFF_WIDTH, DIFF_WIDTH, DIFF_WIDTH, DIFF_WIDTH)
N_IN = sum(IN_SIZES)
IN_OFFSETS = tuple(sum(IN_SIZES[:i + 1]) for i in range(len(IN_SIZES) - 1))

kernel_name = "hybrid_dit_lru_gqa_diffattn_step"


def lambda_init(layer):
    return 0.8 - 0.6 * math.exp(-0.3 * layer)


def rms_norm(x, g):
    xf = x.astype(jnp.float32)
    y = xf * lax.rsqrt(jnp.mean(xf * xf, axis=-1, keepdims=True) + EPS)
    return (y * g.astype(jnp.float32)).astype(x.dtype)


def centred_dwconv(x, w, b):
    T = x.shape[1]
    xp = jnp.pad(x, ((0, 0), (CONV_LEFT, CONV_W - 1 - CONV_LEFT), (0, 0)))
    out = xp[:, 0:T] * w[0] + b
    for j in range(1, CONV_W):
        out = out + xp[:, j:j + T] * w[j]
    return out


def _rotate(x, ang):
    half = x.shape[-1] // 2
    x1, x2 = x[..., :half], x[..., half:]
    cos = jnp.cos(ang)[None, :, None, :]
    sin = jnp.sin(ang)[None, :, None, :]
    return jnp.concatenate([x1 * cos - x2 * sin, x2 * cos + x1 * sin], axis=-1)


def axial_rope(x, n_rows):
    d = x.shape[-1]
    quarter = d // 4
    inv = jnp.power(ROPE_BASE, -jnp.arange(quarter, dtype=jnp.float32) / quarter)
    row = jnp.repeat(jnp.arange(n_rows, dtype=jnp.float32), GRID_W)
    col = jnp.tile(jnp.arange(GRID_W, dtype=jnp.float32), n_rows)
    xf = x.astype(jnp.float32)
    half = d // 2
    out = jnp.concatenate([_rotate(xf[..., :half], row[:, None] * inv[None]),
                           _rotate(xf[..., half:], col[:, None] * inv[None])], axis=-1)
    return out.astype(x.dtype)


def gqa_attend(q, k, v):
    B, Tq, Hq, d = q.shape
    Hkv = k.shape[2]
    G = Hq // Hkv
    nb = Tq // Q_BLOCK
    scale = HEAD_DIM ** -0.5
    qb = q.reshape(B, nb, Q_BLOCK, Hkv, G, d).swapaxes(0, 1)

    def one(qblk):
        s = jnp.einsum('bqhgd,bkhd->bhgqk', qblk, k).astype(jnp.float32) * scale
        p = jax.nn.softmax(s, axis=-1).astype(v.dtype)
        return jnp.einsum('bhgqk,bkhd->bqhgd', p, v)

    o = lax.map(one, qb)
    return o.swapaxes(0, 1).reshape(B, Tq, Hq * v.shape[-1])


def diff_attend(q, k, v, lam):
    B, Tq, H, _, d = q.shape
    nb = Tq // Q_BLOCK
    scale = HEAD_DIM ** -0.5
    qb = q.reshape(B, nb, Q_BLOCK, H, 2, d).swapaxes(0, 1)

    def one(qblk):
        s = jnp.einsum('bqhcd,bkhcd->bhcqk', qblk, k).astype(jnp.float32) * scale
        p = jax.nn.softmax(s, axis=-1)
        w = p[:, :, 0] - lam * p[:, :, 1]
        return jnp.einsum('bhqk,bkhe->bqhe', w.astype(v.dtype), v)

    o = lax.map(one, qb)
    return o.swapaxes(0, 1).reshape(B, Tq, H, v.shape[-1])


def rglru_scan(u, wa, ba, wx, bx, lam, h0, reverse):
    B, T, W = u.shape
    ub = u.reshape(B, T, LRU_BLOCKS, LRU_BW)
    r = jax.nn.sigmoid(jnp.einsum('btnc,ncd->btnd', ub, wa).reshape(B, T, W) + ba)
    i = jax.nn.sigmoid(jnp.einsum('btnc,ncd->btnd', ub, wx).reshape(B, T, W) + bx)
    log_a = -LRU_C * r.astype(jnp.float32) * jax.nn.softplus(-lam.astype(jnp.float32))
    a = jnp.exp(log_a)
    inp = jnp.sqrt(-jnp.expm1(2.0 * log_a)) * (i * u).astype(jnp.float32)

    def step(h, xs):
        a_t, x_t = xs
        h = a_t * h + x_t
        return h, h

    hT, ys = lax.scan(step, h0.astype(jnp.float32),
                      (a.swapaxes(0, 1), inp.swapaxes(0, 1)), reverse=reverse)
    return ys.swapaxes(0, 1).astype(u.dtype), hT.astype(u.dtype)


def mixer(h, p, layer, ctx):
    B, T, _ = h.shape
    z = h @ p['w_in']
    (lru_x, lru_g, gq, gk, gv, gg, dq, dk, dv, dg) = jnp.split(z, IN_OFFSETS, axis=-1)

    u = centred_dwconv(lru_x, p['conv_w'], p['conv_b'])
    if ctx is None:
        h0 = jnp.zeros((B, 2, LRU_WIDTH), dtype=u.dtype)
    else:
        h0 = ctx[4]
    yf, hf = rglru_scan(u, p['wa'][0], p['ba'][0], p['wx'][0], p['bx'][0], p['lam'][0],
                        h0[:, 0], reverse=False)
    yb, hb = rglru_scan(u, p['wa'][1], p['ba'][1], p['wx'][1], p['bx'][1], p['lam'][1],
                        h0[:, 1], reverse=True)
    lru_out = (yf + yb) * jax.nn.silu(lru_g)
    lru_state = jnp.stack([hf, hb], axis=1)

    q = rms_norm(gq.reshape(B, T, GQA_Q_HEADS, HEAD_DIM), p['gqa_gq'])
    k = rms_norm(gk.reshape(B, T, GQA_KV_HEADS, HEAD_DIM), p['gqa_gk'])
    v = gv.reshape(B, T, GQA_KV_HEADS, HEAD_DIM)

    dqr = dq.reshape(B, T, DIFF_HEADS, 2, HEAD_DIM)
    dkr = dk.reshape(B, T, DIFF_HEADS, 2, HEAD_DIM)
    dvr = dv.reshape(B, T, DIFF_HEADS, 2 * HEAD_DIM)

    if ctx is None:
        k_all, v_all, dk_all, dv_all = k, v, dkr, dvr
    else:
        n_rows = T // GRID_W
        q = axial_rope(q, n_rows)
        k_all = jnp.concatenate([ctx[0], axial_rope(k, n_rows)], axis=1)
        v_all = jnp.concatenate([ctx[1], v], axis=1)
        dqr = axial_rope(dqr.reshape(B, T, 2 * DIFF_HEADS, HEAD_DIM), n_rows).reshape(dqr.shape)
        dk_rot = axial_rope(dkr.reshape(B, T, 2 * DIFF_HEADS, HEAD_DIM), n_rows).reshape(dkr.shape)
        dk_all = jnp.concatenate([ctx[2], dk_rot], axis=1)
        dv_all = jnp.concatenate([ctx[3], dvr], axis=1)

    gqa_out = gqa_attend(q, k_all, v_all) * jax.nn.silu(gg)

    dl = p['diff_lam'].astype(jnp.float32)
    lam_i = lambda_init(layer)
    lam = jnp.exp(jnp.sum(dl[0] * dl[1])) - jnp.exp(jnp.sum(dl[2] * dl[3])) + lam_i
    do = diff_attend(dqr, dk_all, dv_all, lam)
    do = rms_norm(do, p['diff_gsub']) * (1.0 - lam_i)
    diff_out = do.reshape(B, T, DIFF_WIDTH) * jax.nn.silu(dg)

    out = jnp.concatenate([lru_out, gqa_out, diff_out], axis=-1) @ p['w_out']
    return out, (k, v, dkr, dvr, lru_state)


def adaln_prenorm(x, cond, w_mod, b_mod, g_pre):
    mod = jax.nn.silu(cond) @ w_mod + b_mod
    shift, scale, gate = jnp.split(mod[:, None, :], 3, axis=-1)
    return rms_norm(x, g_pre) * (1.0 + scale) + shift, gate


def setup_inputs(seed: int = 0) -> dict:
    key = jax.random.key(seed)
    ks = jax.random.split(key, 32)
    f32 = jnp.float32
    nrm = lambda k, s: jax.random.normal(k, s, dtype=f32)
    a_init = jax.random.uniform(ks[20], (DEPTH, 2, LRU_WIDTH), dtype=f32, minval=0.9, maxval=0.999)
    s_init = a_init ** (1.0 / LRU_C)
    return {
        "x_prompt": nrm(ks[0], (BATCH, SEQ, D_MODEL)),
        "x_sample": nrm(ks[1], (DEC_BATCH, DEC_SEQ, D_MODEL)),
        "cache_gqa_k": nrm(ks[2], (DEC_BATCH, DEPTH, PAST_LEN, GQA_KV_HEADS, HEAD_DIM)),
        "cache_gqa_v": nrm(ks[3], (DEC_BATCH, DEPTH, PAST_LEN, GQA_KV_HEADS, HEAD_DIM)),
        "cache_diff_k": nrm(ks[4], (DEC_BATCH, DEPTH, PAST_LEN, DIFF_HEADS, 2, HEAD_DIM)),
        "cache_diff_v": nrm(ks[5], (DEC_BATCH, DEPTH, PAST_LEN, DIFF_HEADS, 2 * HEAD_DIM)),
        "state_lru": 0.5 * nrm(ks[6], (DEC_BATCH, DEPTH, 2, LRU_WIDTH)),
        "c": nrm(ks[7], (DEC_BATCH, D_MODEL)),
        "c_ctx": nrm(ks[8], (D_MODEL,)),
        "w_mod": 0.5 * D_MODEL ** -0.5 * nrm(ks[9], (DEPTH, D_MODEL, 3 * D_MODEL)),
        "b_mod": 0.02 * nrm(ks[10], (DEPTH, 3 * D_MODEL)),
        "g_pre": 1.0 + 0.02 * nrm(ks[11], (DEPTH, D_MODEL)),
        "g_post": 1.0 + 0.02 * nrm(ks[12], (DEPTH, D_MODEL)),
        "w_in": D_MODEL ** -0.5 * nrm(ks[13], (DEPTH, D_MODEL, N_IN)),
        "w_out": D_MIX ** -0.5 * nrm(ks[14], (DEPTH, D_MIX, D_MODEL)),
        "lru_conv_w": CONV_W ** -0.5 * nrm(ks[15], (DEPTH, CONV_W, LRU_WIDTH)),
        "lru_conv_b": 0.02 * nrm(ks[16], (DEPTH, LRU_WIDTH)),
        "lru_wa": LRU_BW ** -0.5 * nrm(ks[17], (DEPTH, 2, LRU_BLOCKS, LRU_BW, LRU_BW)),
        "lru_ba": 0.02 * nrm(ks[18], (DEPTH, 2, LRU_WIDTH)),
        "lru_wx": LRU_BW ** -0.5 * nrm(ks[19], (DEPTH, 2, LRU_BLOCKS, LRU_BW, LRU_BW)),
        "lru_bx": 0.02 * nrm(ks[21], (DEPTH, 2, LRU_WIDTH)),
        "lru_lambda": jnp.log(s_init) - jnp.log1p(-s_init),
        "gqa_gq": 1.0 + 0.02 * nrm(ks[22], (DEPTH, HEAD_DIM)),
        "gqa_gk": 1.0 + 0.02 * nrm(ks[23], (DEPTH, HEAD_DIM)),
        "diff_lam": 0.1 * nrm(ks[24], (DEPTH, 4, HEAD_DIM)),
        "diff_gsub": 1.0 + 0.02 * nrm(ks[25], (DEPTH, 2 * HEAD_DIM)),
    }


def reference(x_prompt, x_sample, cache_gqa_k, cache_gqa_v, cache_diff_k, cache_diff_v, state_lru,
              c, c_ctx, w_mod, b_mod, g_pre, g_post, w_in, w_out, lru_conv_w, lru_conv_b,
              lru_wa, lru_ba, lru_wx, lru_bx, lru_lambda, gqa_gq, gqa_gk, diff_lam, diff_gsub):
    def layer_params(l):
        return {'w_in': w_in[l], 'w_out': w_out[l], 'conv_w': lru_conv_w[l], 'conv_b': lru_conv_b[l],
                'wa': lru_wa[l], 'ba': lru_ba[l], 'wx': lru_wx[l], 'bx': lru_bx[l],
                'lam': lru_lambda[l], 'gqa_gq': gqa_gq[l], 'gqa_gk': gqa_gk[l],
                'diff_lam': diff_lam[l], 'diff_gsub': diff_gsub[l]}

    x = x_prompt
    gks, gvs, dks, dvs, sts = [], [], [], [], []
    for l in range(DEPTH):
        p = layer_params(l)
        h, gate = adaln_prenorm(x, c_ctx[None, :], w_mod[l], b_mod[l], g_pre[l])
        y, (gk, gv, dk, dv, st) = mixer(h, p, l, None)
        x = x + gate * rms_norm(y, g_post[l])
        gks.append(gk); gvs.append(gv); dks.append(dk); dvs.append(dv); sts.append(st)
    y_prompt = x

    x = x_sample
    for l in range(DEPTH):
        p = layer_params(l)
        h, gate = adaln_prenorm(x, c, w_mod[l], b_mod[l], g_pre[l])
        ctx = (cache_gqa_k[:, l], cache_gqa_v[:, l], cache_diff_k[:, l], cache_diff_v[:, l],
               state_lru[:, l])
        y, _ = mixer(h, p, l, ctx)
        x = x + gate * rms_norm(y, g_post[l])
    y_sample = x

    new_gqa_k = jnp.stack(gks, axis=1)
    new_gqa_v = jnp.stack(gvs, axis=1)
    new_diff_k = jnp.stack(dks, axis=1)
    new_diff_v = jnp.stack(dvs, axis=1)
    new_lru = jnp.stack(sts, axis=1)
    return (y_prompt, y_sample, new_gqa_k, new_gqa_v, new_diff_k, new_diff_v, new_lru)
```

```python
import functools
import math

import jax
import jax.numpy as jnp
from jax import lax
from jax.experimental import pallas as pl
from jax.experimental.pallas import tpu as pltpu

F32 = jnp.float32
BF16 = jnp.bfloat16

GRID_W = 64
HEAD_DIM = 64
EPS = 1e-6
ROPE_BASE = 10000.0
LRU_WIDTH = 256
LRU_BLOCKS = 4
LRU_C = 8.0
CONV_W = 4
CONV_LEFT = 2
GQA_Q_HEADS = 4
GQA_KV_HEADS = 2
GQA_WIDTH = GQA_Q_HEADS * HEAD_DIM
GQA_KV_WIDTH = GQA_KV_HEADS * HEAD_DIM
DIFF_HEADS = 4
DIFF_WIDTH = DIFF_HEADS * 2 * HEAD_DIM
QK_SCALE = HEAD_DIM ** -0.5

C_LRU_X = 0
C_LRU_G = C_LRU_X + LRU_WIDTH
C_GQ = C_LRU_G + LRU_WIDTH
C_GK = C_GQ + GQA_WIDTH
C_GV = C_GK + GQA_KV_WIDTH
C_GG = C_GV + GQA_KV_WIDTH
C_DQ = C_GG + GQA_WIDTH
C_DK = C_DQ + DIFF_WIDTH
C_DV = C_DK + DIFF_WIDTH
C_DG = C_DV + DIFF_WIDTH
N_IN = C_DG + DIFF_WIDTH
M_LRU = 0
M_GQA = LRU_WIDTH
M_DIFF = LRU_WIDTH + GQA_WIDTH
D_MIX = LRU_WIDTH + GQA_WIDTH + DIFF_WIDTH

LANES = 128
ROW_CHUNK = 256
MOD_ROWS = 16
VMEM_LIMIT = 56 * 1024 * 1024


def _lambda_init(layer):
    return 0.8 - 0.6 * math.exp(-0.3 * layer)


def _dot(a, b):
    return jnp.dot(a, b, preferred_element_type=F32)


def _silu(x):
    return x * jax.nn.sigmoid(x)


def _rms(x, g):
    ms = jnp.mean(x * x, axis=-1, keepdims=True)
    return x * lax.rsqrt(ms + EPS) * g


def _rms_heads(x, g2):
    lo = lax.broadcasted_iota(jnp.int32, x.shape, 1) < HEAD_DIM
    t = x * x
    s_lo = jnp.sum(jnp.where(lo, t, 0.0), axis=-1, keepdims=True)
    s_hi = jnp.sum(jnp.where(lo, 0.0, t), axis=-1, keepdims=True)
    ms = jnp.where(lo, s_lo, s_hi) * (1.0 / HEAD_DIM)
    return x * lax.rsqrt(ms + EPS) * g2


def _rope(x, cos, sin_signed):
    first = (lax.broadcasted_iota(jnp.int32, x.shape, 1) % 32) < 16
    partner = jnp.where(first, pltpu.roll(x, LANES - 16, 1), pltpu.roll(x, 16, 1))
    return x * cos + partner * sin_signed


def _shift_rows(x, k, fill, n_rows):
    rows = lax.broadcasted_iota(jnp.int32, x.shape, 0)
    if k > 0:
        return jnp.where(rows >= k, pltpu.roll(x, k, 0), fill)
    return jnp.where(rows < n_rows + k, pltpu.roll(x, n_rows + k, 0), fill)


def _linear_scan(a, x, n_rows, reverse):
    k = 1
    while k < n_rows:
        kk = -k if reverse else k
        x = a * _shift_rows(x, kk, 0.0, n_rows) + x
        if 2 * k < n_rows:
            a = a * _shift_rows(a, kk, 1.0, n_rows)
        k *= 2
    return x


def _attend(qm, kt, v):
    s = _dot(qm, kt)
    m = jnp.max(s, axis=-1, keepdims=True)
    e = jnp.exp(s - m)
    l = jnp.sum(e, axis=-1, keepdims=True)
    return _dot(e.astype(BF16), v) / l


def _mod_kernel(cond_ref, w_ref, b_ref, o_ref):
    o_ref[0] = jnp.dot(_silu(cond_ref[...]), w_ref[0], preferred_element_type=F32,
                       precision=lax.Precision.HIGHEST) + b_ref[0]


def _layer_kernel(*refs, is_ctx, n_seq, seq_len, past_len, depth, layer0):
    T = seq_len
    Tk = past_len + T
    n_chunks = T // ROW_CHUNK
    it = iter(refs)
    x_ref, mod_ref, gpre_ref, gpost_ref, win_ref, wout_ref = (next(it) for _ in range(6))
    convw_ref, convb_ref, wg_ref, bg_ref, lam_ref = (next(it) for _ in range(5))
    gq_ref, gk_ref, dlam_ref, gsub_ref = (next(it) for _ in range(4))
    if not is_ctx:
        ck_ref, cv_ref, cdk_ref, cdv_ref, st_ref, cos_ref, sin_ref = (next(it) for _ in range(7))
    y_ref = next(it)
    if is_ctx:
        ogk_ref, ogv_ref, odk_ref, odv_ref, ost_ref = (next(it) for _ in range(5))
    (h_scr, mix_scr, zx_scr, q_scr, kt_scr, v_scr, dq_scr, dkt_scr, dv_scr) = (next(it) for _ in range(9))

    @pl.when(pl.program_id(1) == 0)
    def _():
        y_ref[...] = x_ref[...]

    layer = pl.program_id(1) + layer0

    lam_i = jnp.float32(_lambda_init(depth - 1))
    for lyr in range(depth - 2, -1, -1):
        lam_i = jnp.where(layer == lyr, jnp.float32(_lambda_init(lyr)), lam_i)

    shift = mod_ref[0, 0, :, 0:1024]
    scale = mod_ref[0, 0, :, 1024:2048]
    gate = mod_ref[0, 0, :, 2048:3072]
    lo_lanes = lax.broadcasted_iota(jnp.int32, (ROW_CHUNK, LANES), 1) < HEAD_DIM

    dl = dlam_ref[0]
    lam = (jnp.exp(jnp.sum(dl[0:1] * dl[1:2], axis=-1, keepdims=True))
           - jnp.exp(jnp.sum(dl[2:3] * dl[3:4], axis=-1, keepdims=True)) + lam_i)

    def win(c0, n):
        return win_ref[0, :, c0:c0 + n]

    def rows_of(c):
        return pl.ds(c * ROW_CHUNK, ROW_CHUNK)

    def one_sequence(s):
        for c in range(n_chunks):
            r = rows_of(c)
            hn = _rms(y_ref[s, r, :], gpre_ref[0]) * (1.0 + scale) + shift
            h_scr[r, :] = hn.astype(BF16)

        for c in range(n_chunks):
            r = rows_of(c)
            zxg = _dot(h_scr[r, :], win(C_LRU_X, 2 * LRU_WIDTH))
            zx_scr[r, 0:LRU_WIDTH] = zxg[:, 0:LRU_WIDTH]
            zx_scr[r, LRU_WIDTH:] = _silu(zxg[:, LRU_WIDTH:])
        row_id = lax.broadcasted_iota(jnp.int32, (T, LANES), 0)
        for ct in range(LRU_WIDTH // LANES):
            cl = slice(ct * LANES, (ct + 1) * LANES)
            zx = zx_scr[:, cl]
            u = convb_ref[0, :, cl] + zx * convw_ref[0, CONV_LEFT:CONV_LEFT + 1, cl]
            for j in range(CONV_W):
                if j != CONV_LEFT:
                    u = u + _shift_rows(zx, CONV_LEFT - j, 0.0, T) * convw_ref[0, j:j + 1, cl]
            u_bf = u.astype(BF16)
            y_sum = None
            for d in range(2):
                reverse = d == 1
                gates = _dot(u_bf, wg_ref[0, ct, :, d * 2 * LANES:(d + 1) * 2 * LANES])
                gates = gates + bg_ref[0, ct, :, d * 2 * LANES:(d + 1) * 2 * LANES]
                r_g = jax.nn.sigmoid(gates[:, 0:LANES])
                i_g = jax.nn.sigmoid(gates[:, LANES:])
                lam_d = lam_ref[0, d:d + 1, cl]
                sp = jnp.maximum(-lam_d, 0.0) + jnp.log1p(jnp.exp(-jnp.abs(lam_d)))
                a = jnp.exp(-LRU_C * r_g * sp)
                inp = jnp.sqrt(1.0 - a * a) * (i_g * u)
                if not is_ctx:
                    h0 = st_ref[0, 0, d:d + 1, cl]
                    first_row = (T - 1) if reverse else 0
                    inp = inp + jnp.where(row_id == first_row, a * h0, 0.0)
                y_d = _linear_scan(a, inp, T, reverse)
                if is_ctx:
                    last_row = 0 if reverse else T - 1
                    ost_ref[s, 0, d:d + 1, cl] = y_d[last_row:last_row + 1, :]
                y_sum = y_d if y_sum is None else y_sum + y_d
            lru_gate = zx_scr[:, LRU_WIDTH + ct * LANES:LRU_WIDTH + (ct + 1) * LANES]
            mix_scr[:, M_LRU + ct * LANES:M_LRU + (ct + 1) * LANES] = (y_sum * lru_gate).astype(BF16)

        if not is_ctx:
            for c in range(past_len // ROW_CHUNK):
                r = rows_of(c)
                kt_scr[:, r] = ck_ref[0, 0, r, :].T.astype(BF16)
                v_scr[r, :] = cv_ref[0, 0, r, :].astype(BF16)
                dv_scr[r, :] = cdv_ref[0, 0, r, :].astype(BF16)
                for hd in range(DIFF_HEADS):
                    cl = slice(hd * LANES, (hd + 1) * LANES)
                    dkt_scr[cl, r] = cdk_ref[0, 0, r, cl].T.astype(BF16)
        for c in range(n_chunks):
            r = rows_of(c)
            rk = pl.ds(past_len + c * ROW_CHUNK, ROW_CHUNK)
            hc = h_scr[r, :]
            zkv = _dot(hc, win(C_GK, 2 * GQA_KV_WIDTH))
            k = _rms_heads(zkv[:, 0:GQA_KV_WIDTH], gk_ref[0])
            v = zkv[:, GQA_KV_WIDTH:]
            zdk = _dot(hc, win(C_DK, DIFF_WIDTH))
            zdv = _dot(hc, win(C_DV, DIFF_WIDTH))
            if is_ctx:
                ogk_ref[s, 0, r, :] = k
                ogv_ref[s, 0, r, :] = v
                odk_ref[s, 0, r, :] = zdk
                odv_ref[s, 0, r, :] = zdv
            else:
                k = _rope(k, cos_ref[r, :], sin_ref[r, :])
            kt_scr[:, rk] = k.T.astype(BF16)
            v_scr[rk, :] = v.astype(BF16)
            dv_scr[rk, :] = zdv.astype(BF16)
            for hd in range(DIFF_HEADS):
                cl = slice(hd * LANES, (hd + 1) * LANES)
                dk = zdk[:, cl]
                if not is_ctx:
                    dk = _rope(dk, cos_ref[r, :], sin_ref[r, :])
                dkt_scr[cl, rk] = dk.T.astype(BF16)

        for c in range(n_chunks):
            r = rows_of(c)
            hc = h_scr[r, :]
            zq = _dot(hc, win(C_GQ, GQA_WIDTH))
            zdq = _dot(hc, win(C_DQ, DIFF_WIDTH))
            for j in range(GQA_WIDTH // LANES):
                cl = slice(j * LANES, (j + 1) * LANES)
                q = _rms_heads(zq[:, cl], gq_ref[0])
                if not is_ctx:
                    q = _rope(q, cos_ref[r, :], sin_ref[r, :])
                q_scr[r, cl] = (q * QK_SCALE).astype(BF16)
            for hd in range(DIFF_HEADS):
                cl = slice(hd * LANES, (hd + 1) * LANES)
                dq = zdq[:, cl]
                if not is_ctx:
                    dq = _rope(dq, cos_ref[r, :], sin_ref[r, :])
                dq_scr[r, cl] = (dq * QK_SCALE).astype(BF16)

        zero_bf = jnp.zeros((ROW_CHUNK, LANES), BF16)

        def attn_chunk(c, carry):
            r = pl.ds(pl.multiple_of(c * ROW_CHUNK, ROW_CHUNK), ROW_CHUNK)
            hc = h_scr[r, :]
            g_gate = _silu(_dot(hc, win(C_GG, GQA_WIDTH)))
            kt = kt_scr[...]
            v = v_scr[...]
            for j in range(GQA_WIDTH // LANES):
                cl = slice(j * LANES, (j + 1) * LANES)
                q = q_scr[r, cl]
                o_lo = _attend(jnp.where(lo_lanes, q, zero_bf), kt, v)
                o_hi = _attend(jnp.where(lo_lanes, zero_bf, q), kt, v)
                o = jnp.where(lo_lanes, o_lo, o_hi)
                mix_scr[r, M_GQA + j * LANES:M_GQA + (j + 1) * LANES] = (o * g_gate[:, cl]).astype(BF16)

            d_gate = _silu(_dot(hc, win(C_DG, DIFF_WIDTH)))
            for hd in range(DIFF_HEADS):
                cl = slice(hd * LANES, (hd + 1) * LANES)
                q = dq_scr[r, cl]
                kt_h = dkt_scr[cl, :]
                v_h = dv_scr[:, cl]
                o1 = _attend(jnp.where(lo_lanes, q, zero_bf), kt_h, v_h)
                o2 = _attend(jnp.where(lo_lanes, zero_bf, q), kt_h, v_h)
                o = _rms(o1 - lam * o2, gsub_ref[0]) * (1.0 - lam_i)
                mix_scr[r, M_DIFF + hd * LANES:M_DIFF + (hd + 1) * LANES] = (o * d_gate[:, cl]).astype(BF16)
            return carry

        lax.fori_loop(0, n_chunks, attn_chunk, 0)

        for c in range(n_chunks):
            r = rows_of(c)
            out = _dot(mix_scr[r, :], wout_ref[0])
            y_ref[s, r, :] = y_ref[s, r, :] + gate * _rms(out, gpost_ref[0])

    if n_seq == 1:
        one_sequence(0)
    else:
        def seq_body(s, carry):
            one_sequence(s)
            return carry
        lax.fori_loop(0, n_seq, seq_body, 0)


def _rope_tables(seq_len):
    quarter = HEAD_DIM // 4
    inv = jnp.power(ROPE_BASE, -jnp.arange(quarter, dtype=F32) / quarter)
    t = jnp.arange(seq_len)
    row = (t // GRID_W).astype(F32)[:, None] * inv[None]
    col = (t % GRID_W).astype(F32)[:, None] * inv[None]
    cos = jnp.concatenate([jnp.cos(row), jnp.cos(row), jnp.cos(col), jnp.cos(col)], axis=-1)
    sin = jnp.concatenate([-jnp.sin(row), jnp.sin(row), -jnp.sin(col), jnp.sin(col)], axis=-1)
    reps = LANES // HEAD_DIM
    return jnp.tile(cos, (1, reps)), jnp.tile(sin, (1, reps))


def _block_diag(w):
    nb, bw, _ = w.shape
    eye = jnp.eye(nb, dtype=w.dtype)
    return (eye[:, None, :, None] * w[:, :, None, :]).reshape(nb * bw, nb * bw)


def _layer_call(x, mod, weights, extras, *, is_ctx, n_seq, layer0, n_layers):
    batch, seq_len, d_model = x.shape
    depth = weights[2].shape[0]
    past_len = 0 if is_ctx else extras[0].shape[2]
    tk = past_len + seq_len
    grid = (batch // n_seq, n_layers)
    single = dict(pipeline_mode=pl.Buffered(1))
    w_mode = single if n_layers == 1 else {}

    def per_layer(a):
        nd = a.ndim
        return pl.BlockSpec((1,) + a.shape[1:], lambda b, l: (l + layer0,) + (0,) * (nd - 1), **w_mode)

    y_spec = pl.BlockSpec((n_seq, seq_len, d_model), lambda b, l: (b, 0, 0))
    if is_ctx:
        x_spec = y_spec
        mod_spec = pl.BlockSpec((1, 1, 1, mod.shape[-1]), lambda b, l: (l + layer0, 0, 0, 0))
    else:
        x_spec = pl.BlockSpec((n_seq, seq_len, d_model), lambda b, l: (b, 0, 0), **single)
        mod_spec = pl.BlockSpec((1, 1, 1, mod.shape[-1]), lambda b, l: (l + layer0, b + 1, 0, 0))
    in_specs = [x_spec, mod_spec] + [per_layer(w) for w in weights]
    args = [x, mod] + list(weights)
    if not is_ctx:
        ck, cv, cdk, cdv, st, cos, sin = extras
        for a in (ck, cv, cdk, cdv, st):
            in_specs.append(
                pl.BlockSpec((1, 1) + a.shape[2:], lambda b, l: (b, l + layer0, 0, 0), **single))
        for a in (cos, sin):
            in_specs.append(pl.BlockSpec(a.shape, lambda b, l: (0, 0), **single))
        args += list(extras)

    out_shape = [jax.ShapeDtypeStruct(x.shape, F32)]
    out_specs = [y_spec]
    if is_ctx:
        for width in (GQA_KV_WIDTH, GQA_KV_WIDTH, DIFF_WIDTH, DIFF_WIDTH):
            out_shape.append(jax.ShapeDtypeStruct((batch, depth, seq_len, width), F32))
            out_specs.append(pl.BlockSpec((n_seq, 1, seq_len, width), lambda b, l: (b, l, 0, 0)))
        out_shape.append(jax.ShapeDtypeStruct((batch, depth, 2, LRU_WIDTH), F32))
        out_specs.append(pl.BlockSpec((n_seq, 1, 2, LRU_WIDTH), lambda b, l: (b, l, 0, 0)))

    scratch = [
        pltpu.VMEM((seq_len, d_model), BF16),
        pltpu.VMEM((seq_len, D_MIX), BF16),
        pltpu.VMEM((seq_len, 2 * LRU_WIDTH), F32),
        pltpu.VMEM((seq_len, GQA_WIDTH), BF16),
        pltpu.VMEM((GQA_KV_WIDTH, tk), BF16),
        pltpu.VMEM((tk, GQA_KV_WIDTH), BF16),
        pltpu.VMEM((seq_len, DIFF_WIDTH), BF16),
        pltpu.VMEM((DIFF_WIDTH, tk), BF16),
        pltpu.VMEM((tk, DIFF_WIDTH), BF16),
    ]
    kern = functools.partial(_layer_kernel, is_ctx=is_ctx, n_seq=n_seq, seq_len=seq_len,
                             past_len=past_len, depth=depth, layer0=layer0)
    return pl.pallas_call(
        kern,
        grid=grid,
        in_specs=in_specs,
        out_specs=out_specs,
        out_shape=out_shape,
        scratch_shapes=scratch,
        compiler_params=pltpu.CompilerParams(
            dimension_semantics=("arbitrary", "arbitrary"),
            vmem_limit_bytes=VMEM_LIMIT),
        name="ctx_pass" if is_ctx else f"denoise_layer{layer0}",
    )(*args)


def kernel(x_prompt, x_sample, cache_gqa_k, cache_gqa_v, cache_diff_k, cache_diff_v, state_lru, c, c_ctx, w_mod, b_mod, g_pre, g_post, w_in, w_out, lru_conv_w, lru_conv_b, lru_wa, lru_ba, lru_wx, lru_bx, lru_lambda, gqa_gq, gqa_gk, diff_lam, diff_gsub):
    depth, d_model, _ = w_in.shape
    dec_batch = x_sample.shape[0]

    cond = jnp.concatenate(
        [c_ctx[None, :], c, jnp.zeros((MOD_ROWS - 1 - dec_batch, d_model), F32)], axis=0)
    n_mod = w_mod.shape[-1]
    mod_tile = n_mod // 3
    mod = pl.pallas_call(
        _mod_kernel,
        grid=(depth, n_mod // mod_tile),
        in_specs=[pl.BlockSpec((MOD_ROWS, d_model), lambda l, j: (0, 0)),
                  pl.BlockSpec((1, d_model, mod_tile), lambda l, j: (l, 0, j)),
                  pl.BlockSpec((1, 1, mod_tile), lambda l, j: (l, 0, j))],
        out_specs=pl.BlockSpec((1, MOD_ROWS, mod_tile), lambda l, j: (l, 0, j)),
        out_shape=jax.ShapeDtypeStruct((depth, MOD_ROWS, n_mod), F32),
        name="adaln_mod",
    )(cond, w_mod, b_mod[:, None, :])
    mod = mod[:, :, None, :]

    perm = jnp.arange(GQA_WIDTH).reshape(2, 2, HEAD_DIM).transpose(1, 0, 2).reshape(-1)
    cols = jnp.arange(N_IN)
    cols = cols.at[C_GQ:C_GQ + GQA_WIDTH].set(C_GQ + perm)
    cols = cols.at[C_GG:C_GG + GQA_WIDTH].set(C_GG + perm)
    w_in_p = w_in[:, :, cols].astype(BF16)
    rows = jnp.arange(D_MIX).at[M_GQA:M_GQA + GQA_WIDTH].set(M_GQA + perm)
    w_out_p = w_out[:, rows, :].astype(BF16)
    bd = jax.vmap(jax.vmap(_block_diag))
    wa_d, wx_d = bd(lru_wa), bd(lru_wx)
    n_ct = LRU_WIDTH // LANES
    wg = jnp.stack([
        jnp.concatenate([m[:, d, ct * LANES:(ct + 1) * LANES, ct * LANES:(ct + 1) * LANES]
                         for d in range(2) for m in (wa_d, wx_d)], axis=-1)
        for ct in range(n_ct)], axis=1).astype(BF16)
    bg = jnp.stack([
        jnp.concatenate([m[:, d, ct * LANES:(ct + 1) * LANES]
                         for d in range(2) for m in (lru_ba, lru_bx)], axis=-1)
        for ct in range(n_ct)], axis=1)[:, :, None, :]
    reps = LANES // HEAD_DIM
    weights = [g_pre[:, None, :], g_post[:, None, :], w_in_p, w_out_p, lru_conv_w,
               lru_conv_b[:, None, :], wg, bg, lru_lambda,
               jnp.tile(gqa_gq, (1, reps))[:, None, :], jnp.tile(gqa_gk, (1, reps))[:, None, :],
               diff_lam, diff_gsub[:, None, :]]

    y_prompt, gk, gv, dk, dv, st = _layer_call(x_prompt, mod, weights, None, is_ctx=True, n_seq=2,
                                               layer0=0, n_layers=depth)
    b, _, t, _ = gk.shape
    new_gqa_k = gk.reshape(b, depth, t, GQA_KV_HEADS, HEAD_DIM)
    new_gqa_v = gv.reshape(b, depth, t, GQA_KV_HEADS, HEAD_DIM)
    new_diff_k = dk.reshape(b, depth, t, DIFF_HEADS, 2, HEAD_DIM)
    new_diff_v = dv.reshape(b, depth, t, DIFF_HEADS, 2 * HEAD_DIM)

    db, _, past = cache_gqa_k.shape[:3]
    cos, sin = _rope_tables(x_sample.shape[1])
    extras = [cache_gqa_k.reshape(db, depth, past, GQA_KV_WIDTH),
              cache_gqa_v.reshape(db, depth, past, GQA_KV_WIDTH),
              cache_diff_k.reshape(db, depth, past, DIFF_WIDTH),
              cache_diff_v.reshape(db, depth, past, DIFF_WIDTH),
              state_lru, cos, sin]
    y_sample = x_sample
    for lyr in range(depth):
        (y_sample,) = _layer_call(y_sample, mod, weights, extras, is_ctx=False, n_seq=1,
                                  layer0=lyr, n_layers=1)

    return (y_prompt, y_sample, new_gqa_k, new_gqa_v, new_diff_k, new_diff_v, st)
```

```python
import functools
import math

import jax
import jax.numpy as jnp
from jax import lax
from jax.experimental import pallas as pl
from jax.experimental.pallas import tpu as pltpu

F32 = jnp.float32
BF16 = jnp.bfloat16

GRID_W = 64
HEAD_DIM = 64
EPS = 1e-6
ROPE_BASE = 10000.0
LRU_WIDTH = 256
LRU_BLOCKS = 4
LRU_C = 8.0
CONV_W = 4
CONV_LEFT = 2
GQA_Q_HEADS = 4
GQA_KV_HEADS = 2
GQA_WIDTH = GQA_Q_HEADS * HEAD_DIM
GQA_KV_WIDTH = GQA_KV_HEADS * HEAD_DIM
DIFF_HEADS = 4
DIFF_WIDTH = DIFF_HEADS * 2 * HEAD_DIM
QK_SCALE = HEAD_DIM ** -0.5

C_LRU_X = 0
C_LRU_G = C_LRU_X + LRU_WIDTH
C_GQ = C_LRU_G + LRU_WIDTH
C_GK = C_GQ + GQA_WIDTH
C_GV = C_GK + GQA_KV_WIDTH
C_GG = C_GV + GQA_KV_WIDTH
C_DQ = C_GG + GQA_WIDTH
C_DK = C_DQ + DIFF_WIDTH
C_DV = C_DK + DIFF_WIDTH
C_DG = C_DV + DIFF_WIDTH
N_IN = C_DG + DIFF_WIDTH
M_LRU = 0
M_GQA = LRU_WIDTH
M_DIFF = LRU_WIDTH + GQA_WIDTH
D_MIX = LRU_WIDTH + GQA_WIDTH + DIFF_WIDTH

LANES = 128
ROW_CHUNK = 256
MOD_ROWS = 16
VMEM_LIMIT = 56 * 1024 * 1024


def _lambda_init(layer):
    return 0.8 - 0.6 * math.exp(-0.3 * layer)


def _dot(a, b):
    return jnp.dot(a, b, preferred_element_type=F32)


def _silu(x):
    return x * jax.nn.sigmoid(x)


def _rms(x, g):
    ms = jnp.mean(x * x, axis=-1, keepdims=True)
    return x * lax.rsqrt(ms + EPS) * g


def _rms_heads(x, g2):
    lo = lax.broadcasted_iota(jnp.int32, x.shape, 1) < HEAD_DIM
    t = x * x
    s_lo = jnp.sum(jnp.where(lo, t, 0.0), axis=-1, keepdims=True)
    s_hi = jnp.sum(jnp.where(lo, 0.0, t), axis=-1, keepdims=True)
    ms = jnp.where(lo, s_lo, s_hi) * (1.0 / HEAD_DIM)
    return x * lax.rsqrt(ms + EPS) * g2


def _rope(x, cos, sin_signed):
    first = (lax.broadcasted_iota(jnp.int32, x.shape, 1) % 32) < 16
    partner = jnp.where(first, pltpu.roll(x, LANES - 16, 1), pltpu.roll(x, 16, 1))
    return x * cos + partner * sin_signed


def _shift_rows(x, k, fill, n_rows):
    rows = lax.broadcasted_iota(jnp.int32, x.shape, 0)
    if k > 0:
        return jnp.where(rows >= k, pltpu.roll(x, k, 0), fill)
    return jnp.where(rows < n_rows + k, pltpu.roll(x, n_rows + k, 0), fill)


def _linear_scan(a, x, n_rows, reverse):
    k = 1
    while k < n_rows:
        kk = -k if reverse else k
        x = a * _shift_rows(x, kk, 0.0, n_rows) + x
        if 2 * k < n_rows:
            a = a * _shift_rows(a, kk, 1.0, n_rows)
        k *= 2
    return x


def _attend(qm, kt, v):
    s = _dot(qm, kt)
    m = jnp.max(s, axis=-1, keepdims=True)
    e = jnp.exp(s - m)
    l = jnp.sum(e, axis=-1, keepdims=True)
    return _dot(e.astype(BF16), v) / l


def _mod_kernel(cond_ref, w_ref, b_ref, o_ref):
    o_ref[0] = jnp.dot(_silu(cond_ref[...]), w_ref[0], preferred_element_type=F32,
                       precision=lax.Precision.HIGHEST) + b_ref[0]


def _layer_kernel(*refs, is_ctx, n_seq, seq_len, past_len, depth, layer0):
    T = seq_len
    Tk = past_len + T
    n_chunks = T // ROW_CHUNK
    it = iter(refs)
    x_ref, mod_ref, gpre_ref, gpost_ref, win_ref, wout_ref = (next(it) for _ in range(6))
    convw_ref, convb_ref, wg_ref, bg_ref, lam_ref = (next(it) for _ in range(5))
    gq_ref, gk_ref, dlam_ref, gsub_ref = (next(it) for _ in range(4))
    if not is_ctx:
        ck_ref, cv_ref, cdk_ref, cdv_ref, st_ref, cos_ref, sin_ref = (next(it) for _ in range(7))
    y_ref = next(it)
    if is_ctx:
        ogk_ref, ogv_ref, odk_ref, odv_ref, ost_ref = (next(it) for _ in range(5))
    (h_scr, mix_scr, zx_scr, q_scr, kt_scr, v_scr, dq_scr, dkt_scr, dv_scr) = (next(it) for _ in range(9))

    @pl.when(pl.program_id(1) == 0)
    def _():
        y_ref[...] = x_ref[...]

    layer = pl.program_id(1) + layer0

    lam_i = jnp.float32(_lambda_init(depth - 1))
    for lyr in range(depth - 2, -1, -1):
        lam_i = jnp.where(layer == lyr, jnp.float32(_lambda_init(lyr)), lam_i)

    shift = mod_ref[0, 0, :, 0:1024]
    scale = mod_ref[0, 0, :, 1024:2048]
    gate = mod_ref[0, 0, :, 2048:3072]
    lo_lanes = lax.broadcasted_iota(jnp.int32, (ROW_CHUNK, LANES), 1) < HEAD_DIM

    dl = dlam_ref[0]
    lam = (jnp.exp(jnp.sum(dl[0:1] * dl[1:2], axis=-1, keepdims=True))
           - jnp.exp(jnp.sum(dl[2:3] * dl[3:4], axis=-1, keepdims=True)) + lam_i)

    def win(c0, n):
        return win_ref[0, :, c0:c0 + n]

    def rows_of(c):
        return pl.ds(c * ROW_CHUNK, ROW_CHUNK)

    def one_sequence(s):
        for c in range(n_chunks):
            r = rows_of(c)
            hn = _rms(y_ref[s, r, :], gpre_ref[0]) * (1.0 + scale) + shift
            h_scr[r, :] = hn.astype(BF16)

        for c in range(n_chunks):
            r = rows_of(c)
            zxg = _dot(h_scr[r, :], win(C_LRU_X, 2 * LRU_WIDTH))
            zx_scr[r, 0:LRU_WIDTH] = zxg[:, 0:LRU_WIDTH]
            zx_scr[r, LRU_WIDTH:] = _silu(zxg[:, LRU_WIDTH:])
        row_id = lax.broadcasted_iota(jnp.int32, (T, LANES), 0)
        for ct in range(LRU_WIDTH // LANES):
            cl = slice(ct * LANES, (ct + 1) * LANES)
            zx = zx_scr[:, cl]
            u = convb_ref[0, :, cl] + zx * convw_ref[0, CONV_LEFT:CONV_LEFT + 1, cl]
            for j in range(CONV_W):
                if j != CONV_LEFT:
                    u = u + _shift_rows(zx, CONV_LEFT - j, 0.0, T) * convw_ref[0, j:j + 1, cl]
            u_bf = u.astype(BF16)
            y_sum = None
            for d in range(2):
                reverse = d == 1
                gates = _dot(u_bf, wg_ref[0, ct, :, d * 2 * LANES:(d + 1) * 2 * LANES])
                gates = gates + bg_ref[0, ct, :, d * 2 * LANES:(d + 1) * 2 * LANES]
                r_g = jax.nn.sigmoid(gates[:, 0:LANES])
                i_g = jax.nn.sigmoid(gates[:, LANES:])
                lam_d = lam_ref[0, d:d + 1, cl]
                sp = jnp.maximum(-lam_d, 0.0) + jnp.log1p(jnp.exp(-jnp.abs(lam_d)))
                a = jnp.exp(-LRU_C * r_g * sp)
                inp = jnp.sqrt(1.0 - a * a) * (i_g * u)
                if not is_ctx:
                    h0 = st_ref[0, 0, d:d + 1, cl]
                    first_row = (T - 1) if reverse else 0
                    inp = inp + jnp.where(row_id == first_row, a * h0, 0.0)
                y_d = _linear_scan(a, inp, T, reverse)
                if is_ctx:
                    last_row = 0 if reverse else T - 1
                    ost_ref[s, 0, d:d + 1, cl] = y_d[last_row:last_row + 1, :]
                y_sum = y_d if y_sum is None else y_sum + y_d
            lru_gate = zx_scr[:, LRU_WIDTH + ct * LANES:LRU_WIDTH + (ct + 1) * LANES]
            mix_scr[:, M_LRU + ct * LANES:M_LRU + (ct + 1) * LANES] = (y_sum * lru_gate).astype(BF16)

        if not is_ctx:
            for c in range(past_len // ROW_CHUNK):
                r = rows_of(c)
                kt_scr[:, r] = ck_ref[0, 0, :, r].astype(BF16)
                v_scr[r, :] = cv_ref[0, 0, :, r].T.astype(BF16)
                dkt_scr[:, r] = cdk_ref[0, 0, :, r].astype(BF16)
                for hd in range(DIFF_HEADS):
                    rows_h = pl.ds(c * ROW_CHUNK * DIFF_HEADS + hd, ROW_CHUNK, stride=DIFF_HEADS)
                    dv_scr[r, hd * LANES:(hd + 1) * LANES] = cdv_ref[0, 0, rows_h, :].astype(BF16)
        for c in range(n_chunks):
            r = rows_of(c)
            rk = pl.ds(past_len + c * ROW_CHUNK, ROW_CHUNK)
            hc = h_scr[r, :]
            zkv = _dot(hc, win(C_GK, 2 * GQA_KV_WIDTH))
            k = _rms_heads(zkv[:, 0:GQA_KV_WIDTH], gk_ref[0])
            v = zkv[:, GQA_KV_WIDTH:]
            zdk = _dot(hc, win(C_DK, DIFF_WIDTH))
            zdv = _dot(hc, win(C_DV, DIFF_WIDTH))
            if not is_ctx:
                k = _rope(k, cos_ref[r, :], sin_ref[r, :])
            k_t = k.T
            kt_scr[:, rk] = k_t.astype(BF16)
            v_scr[rk, :] = v.astype(BF16)
            dv_scr[rk, :] = zdv.astype(BF16)
            if is_ctx:
                ogk_ref[s, 0, :, r] = k_t
                ogv_ref[s, 0, :, r] = v.T
            for hd in range(DIFF_HEADS):
                cl = slice(hd * LANES, (hd + 1) * LANES)
                dk = zdk[:, cl]
                if not is_ctx:
                    dk = _rope(dk, cos_ref[r, :], sin_ref[r, :])
                dk_t = dk.T
                dkt_scr[cl, rk] = dk_t.astype(BF16)
                if is_ctx:
                    odk_ref[s, 0, cl, r] = dk_t
                    rows_h = pl.ds(c * ROW_CHUNK * DIFF_HEADS + hd, ROW_CHUNK, stride=DIFF_HEADS)
                    odv_ref[s, 0, rows_h, :] = zdv[:, cl]

        for c in range(n_chunks):
            r = rows_of(c)
            hc = h_scr[r, :]
            zq = _dot(hc, win(C_GQ, GQA_WIDTH))
            zdq = _dot(hc, win(C_DQ, DIFF_WIDTH))
            for j in range(GQA_WIDTH // LANES):
                cl = slice(j * LANES, (j + 1) * LANES)
                q = _rms_heads(zq[:, cl], gq_ref[0])
                if not is_ctx:
                    q = _rope(q, cos_ref[r, :], sin_ref[r, :])
                q_scr[r, cl] = (q * QK_SCALE).astype(BF16)
            for hd in range(DIFF_HEADS):
                cl = slice(hd * LANES, (hd + 1) * LANES)
                dq = zdq[:, cl]
                if not is_ctx:
                    dq = _rope(dq, cos_ref[r, :], sin_ref[r, :])
                dq_scr[r, cl] = (dq * QK_SCALE).astype(BF16)

        zero_bf = jnp.zeros((ROW_CHUNK, LANES), BF16)

        def attn_chunk(c, carry):
            r = pl.ds(pl.multiple_of(c * ROW_CHUNK, ROW_CHUNK), ROW_CHUNK)
            hc = h_scr[r, :]
            g_gate = _silu(_dot(hc, win(C_GG, GQA_WIDTH)))
            kt = kt_scr[...]
            v = v_scr[...]
            for j in range(GQA_WIDTH // LANES):
                cl = slice(j * LANES, (j + 1) * LANES)
                q = q_scr[r, cl]
                o_lo = _attend(jnp.where(lo_lanes, q, zero_bf), kt, v)
                o_hi = _attend(jnp.where(lo_lanes, zero_bf, q), kt, v)
                o = jnp.where(lo_lanes, o_lo, o_hi)
                mix_scr[r, M_GQA + j * LANES:M_GQA + (j + 1) * LANES] = (o * g_gate[:, cl]).astype(BF16)

            d_gate = _silu(_dot(hc, win(C_DG, DIFF_WIDTH)))
            for hd in range(DIFF_HEADS):
                cl = slice(hd * LANES, (hd + 1) * LANES)
                q = dq_scr[r, cl]
                kt_h = dkt_scr[cl, :]
                v_h = dv_scr[:, cl]
                o1 = _attend(jnp.where(lo_lanes, q, zero_bf), kt_h, v_h)
                o2 = _attend(jnp.where(lo_lanes, zero_bf, q), kt_h, v_h)
                o = _rms(o1 - lam * o2, gsub_ref[0]) * (1.0 - lam_i)
                mix_scr[r, M_DIFF + hd * LANES:M_DIFF + (hd + 1) * LANES] = (o * d_gate[:, cl]).astype(BF16)
            return carry

        lax.fori_loop(0, n_chunks, attn_chunk, 0)

        for c in range(n_chunks):
            r = rows_of(c)
            out = _dot(mix_scr[r, :], wout_ref[0])
            y_ref[s, r, :] = y_ref[s, r, :] + gate * _rms(out, gpost_ref[0])

    if n_seq == 1:
        one_sequence(0)
    else:
        def seq_body(s, carry):
            one_sequence(s)
            return carry
        lax.fori_loop(0, n_seq, seq_body, 0)


def _rope_tables(seq_len):
    quarter = HEAD_DIM // 4
    inv = jnp.power(ROPE_BASE, -jnp.arange(quarter, dtype=F32) / quarter)
    t = jnp.arange(seq_len)
    row = (t // GRID_W).astype(F32)[:, None] * inv[None]
    col = (t % GRID_W).astype(F32)[:, None] * inv[None]
    cos = jnp.concatenate([jnp.cos(row), jnp.cos(row), jnp.cos(col), jnp.cos(col)], axis=-1)
    sin = jnp.concatenate([-jnp.sin(row), jnp.sin(row), -jnp.sin(col), jnp.sin(col)], axis=-1)
    reps = LANES // HEAD_DIM
    return jnp.tile(cos, (1, reps)), jnp.tile(sin, (1, reps))


def _block_diag(w):
    nb, bw, _ = w.shape
    eye = jnp.eye(nb, dtype=w.dtype)
    return (eye[:, None, :, None] * w[:, :, None, :]).reshape(nb * bw, nb * bw)


def _layer_call(x, mod, weights, extras, *, is_ctx, n_seq, layer0, n_layers):
    batch, seq_len, d_model = x.shape
    depth = weights[2].shape[0]
    past_len = 0 if is_ctx else extras[0].shape[3]
    tk = past_len + seq_len
    grid = (batch // n_seq, n_layers)
    single = dict(pipeline_mode=pl.Buffered(1))
    w_mode = single if n_layers == 1 else {}

    def per_layer(a):
        nd = a.ndim
        return pl.BlockSpec((1,) + a.shape[1:], lambda b, l: (l + layer0,) + (0,) * (nd - 1), **w_mode)

    y_spec = pl.BlockSpec((n_seq, seq_len, d_model), lambda b, l: (b, 0, 0))
    if is_ctx:
        x_spec = y_spec
        mod_spec = pl.BlockSpec((1, 1, 1, mod.shape[-1]), lambda b, l: (l + layer0, 0, 0, 0))
    else:
        x_spec = pl.BlockSpec((n_seq, seq_len, d_model), lambda b, l: (b, 0, 0), **single)
        mod_spec = pl.BlockSpec((1, 1, 1, mod.shape[-1]), lambda b, l: (l + layer0, b + 1, 0, 0))
    in_specs = [x_spec, mod_spec] + [per_layer(w) for w in weights]
    args = [x, mod] + list(weights)
    if not is_ctx:
        ck, cv, cdk, cdv, st, cos, sin = extras
        for a in (ck, cv, cdk, cdv, st):
            in_specs.append(
                pl.BlockSpec((1, 1) + a.shape[2:], lambda b, l: (b, l + layer0, 0, 0), **single))
        for a in (cos, sin):
            in_specs.append(pl.BlockSpec(a.shape, lambda b, l: (0, 0), **single))
        args += list(extras)

    out_shape = [jax.ShapeDtypeStruct(x.shape, F32)]
    out_specs = [y_spec]
    if is_ctx:
        for rows_, cols_ in ((GQA_KV_WIDTH, seq_len), (GQA_KV_WIDTH, seq_len), (DIFF_WIDTH, seq_len),
                             (seq_len * DIFF_HEADS, DIFF_WIDTH // DIFF_HEADS)):
            out_shape.append(jax.ShapeDtypeStruct((batch, depth, rows_, cols_), F32))
            out_specs.append(pl.BlockSpec((n_seq, 1, rows_, cols_), lambda b, l: (b, l, 0, 0)))
        out_shape.append(jax.ShapeDtypeStruct((batch, depth, 2, LRU_WIDTH), F32))
        out_specs.append(pl.BlockSpec((n_seq, 1, 2, LRU_WIDTH), lambda b, l: (b, l, 0, 0)))

    scratch = [
        pltpu.VMEM((seq_len, d_model), BF16),
        pltpu.VMEM((seq_len, D_MIX), BF16),
        pltpu.VMEM((seq_len, 2 * LRU_WIDTH), F32),
        pltpu.VMEM((seq_len, GQA_WIDTH), BF16),
        pltpu.VMEM((GQA_KV_WIDTH, tk), BF16),
        pltpu.VMEM((tk, GQA_KV_WIDTH), BF16),
        pltpu.VMEM((seq_len, DIFF_WIDTH), BF16),
        pltpu.VMEM((DIFF_WIDTH, tk), BF16),
        pltpu.VMEM((tk, DIFF_WIDTH), BF16),
    ]
    kern = functools.partial(_layer_kernel, is_ctx=is_ctx, n_seq=n_seq, seq_len=seq_len,
                             past_len=past_len, depth=depth, layer0=layer0)
    return pl.pallas_call(
        kern,
        grid=grid,
        in_specs=in_specs,
        out_specs=out_specs,
        out_shape=out_shape,
        scratch_shapes=scratch,
        compiler_params=pltpu.CompilerParams(
            dimension_semantics=("arbitrary", "arbitrary"),
            vmem_limit_bytes=VMEM_LIMIT),
        name="ctx_pass" if is_ctx else f"denoise_layer{layer0}",
    )(*args)


def kernel(x_prompt, x_sample, cache_gqa_k, cache_gqa_v, cache_diff_k, cache_diff_v, state_lru, c, c_ctx, w_mod, b_mod, g_pre, g_post, w_in, w_out, lru_conv_w, lru_conv_b, lru_wa, lru_ba, lru_wx, lru_bx, lru_lambda, gqa_gq, gqa_gk, diff_lam, diff_gsub):
    depth, d_model, _ = w_in.shape
    dec_batch = x_sample.shape[0]

    cond = jnp.concatenate(
        [c_ctx[None, :], c, jnp.zeros((MOD_ROWS - 1 - dec_batch, d_model), F32)], axis=0)
    n_mod = w_mod.shape[-1]
    mod_tile = n_mod // 3
    mod = pl.pallas_call(
        _mod_kernel,
        grid=(depth, n_mod // mod_tile),
        in_specs=[pl.BlockSpec((MOD_ROWS, d_model), lambda l, j: (0, 0)),
                  pl.BlockSpec((1, d_model, mod_tile), lambda l, j: (l, 0, j)),
                  pl.BlockSpec((1, 1, mod_tile), lambda l, j: (l, 0, j))],
        out_specs=pl.BlockSpec((1, MOD_ROWS, mod_tile), lambda l, j: (l, 0, j)),
        out_shape=jax.ShapeDtypeStruct((depth, MOD_ROWS, n_mod), F32),
        name="adaln_mod",
    )(cond, w_mod, b_mod[:, None, :])
    mod = mod[:, :, None, :]

    def swap_heads(a, axis, start):
        sl = lambda lo, hi: lax.slice_in_dim(a, lo, hi, axis=axis)
        h = HEAD_DIM
        return jnp.concatenate(
            [sl(0, start + h), sl(start + 2 * h, start + 3 * h), sl(start + h, start + 2 * h),
             sl(start + 3 * h, a.shape[axis])], axis=axis)

    w_in_p = swap_heads(swap_heads(w_in, 2, C_GQ), 2, C_GG).astype(BF16)
    w_out_p = swap_heads(w_out, 1, M_GQA).astype(BF16)
    bd = jax.vmap(jax.vmap(_block_diag))
    wa_d, wx_d = bd(lru_wa), bd(lru_wx)
    n_ct = LRU_WIDTH // LANES
    wg = jnp.stack([
        jnp.concatenate([m[:, d, ct * LANES:(ct + 1) * LANES, ct * LANES:(ct + 1) * LANES]
                         for d in range(2) for m in (wa_d, wx_d)], axis=-1)
        for ct in range(n_ct)], axis=1).astype(BF16)
    bg = jnp.stack([
        jnp.concatenate([m[:, d, ct * LANES:(ct + 1) * LANES]
                         for d in range(2) for m in (lru_ba, lru_bx)], axis=-1)
        for ct in range(n_ct)], axis=1)[:, :, None, :]
    reps = LANES // HEAD_DIM
    weights = [g_pre[:, None, :], g_post[:, None, :], w_in_p, w_out_p, lru_conv_w,
               lru_conv_b[:, None, :], wg, bg, lru_lambda,
               jnp.tile(gqa_gq, (1, reps))[:, None, :], jnp.tile(gqa_gk, (1, reps))[:, None, :],
               diff_lam, diff_gsub[:, None, :]]

    y_prompt, gk, gv, dk, dv, st = _layer_call(x_prompt, mod, weights, None, is_ctx=True, n_seq=2,
                                               layer0=0, n_layers=depth)
    b, t = x_prompt.shape[:2]
    new_gqa_k = gk.reshape(b, depth, GQA_KV_HEADS, HEAD_DIM, t).transpose(0, 1, 4, 2, 3)
    new_gqa_v = gv.reshape(b, depth, GQA_KV_HEADS, HEAD_DIM, t).transpose(0, 1, 4, 2, 3)
    new_diff_k = dk.reshape(b, depth, DIFF_HEADS, 2, HEAD_DIM, t).transpose(0, 1, 5, 2, 3, 4)
    new_diff_v = dv.reshape(b, depth, t, DIFF_HEADS, 2 * HEAD_DIM)

    db, _, past = cache_gqa_k.shape[:3]
    cos, sin = _rope_tables(x_sample.shape[1])
    extras = [cache_gqa_k.transpose(0, 1, 3, 4, 2).reshape(db, depth, GQA_KV_WIDTH, past),
              cache_gqa_v.transpose(0, 1, 3, 4, 2).reshape(db, depth, GQA_KV_WIDTH, past),
              cache_diff_k.transpose(0, 1, 3, 4, 5, 2).reshape(db, depth, DIFF_WIDTH, past),
              cache_diff_v.reshape(db, depth, past * DIFF_HEADS, 2 * HEAD_DIM),
              state_lru, cos, sin]
    y_sample = x_sample
    for lyr in range(depth):
        (y_sample,) = _layer_call(y_sample, mod, weights, extras, is_ctx=False, n_seq=1,
                                  layer0=lyr, n_layers=1)

    return (y_prompt, y_sample, new_gqa_k, new_gqa_v, new_diff_k, new_diff_v, st)
```

```python
import functools
import math

import jax
import jax.numpy as jnp
from jax import lax
from jax.experimental import pallas as pl
from jax.experimental.pallas import tpu as pltpu

F32 = jnp.float32
BF16 = jnp.bfloat16

GRID_W = 64
HEAD_DIM = 64
EPS = 1e-6
ROPE_BASE = 10000.0
LRU_WIDTH = 256
LRU_BLOCKS = 4
LRU_C = 8.0
CONV_W = 4
CONV_LEFT = 2
GQA_Q_HEADS = 4
GQA_KV_HEADS = 2
GQA_WIDTH = GQA_Q_HEADS * HEAD_DIM
GQA_KV_WIDTH = GQA_KV_HEADS * HEAD_DIM
DIFF_HEADS = 4
DIFF_WIDTH = DIFF_HEADS * 2 * HEAD_DIM
QK_SCALE = HEAD_DIM ** -0.5

C_LRU_X = 0
C_LRU_G = C_LRU_X + LRU_WIDTH
C_GQ = C_LRU_G + LRU_WIDTH
C_GK = C_GQ + GQA_WIDTH
C_GV = C_GK + GQA_KV_WIDTH
C_GG = C_GV + GQA_KV_WIDTH
C_DQ = C_GG + GQA_WIDTH
C_DK = C_DQ + DIFF_WIDTH
C_DV = C_DK + DIFF_WIDTH
C_DG = C_DV + DIFF_WIDTH
N_IN = C_DG + DIFF_WIDTH
M_LRU = 0
M_GQA = LRU_WIDTH
M_DIFF = LRU_WIDTH + GQA_WIDTH
D_MIX = LRU_WIDTH + GQA_WIDTH + DIFF_WIDTH

LANES = 128
ROW_CHUNK = 256
MOD_ROWS = 16
VMEM_LIMIT = 56 * 1024 * 1024


def _lambda_init(layer):
    return 0.8 - 0.6 * math.exp(-0.3 * layer)


def _dot(a, b):
    return jnp.dot(a, b, preferred_element_type=F32)


def _silu(x):
    return x * jax.nn.sigmoid(x)


def _rms(x, g):
    ms = jnp.mean(x * x, axis=-1, keepdims=True)
    return x * lax.rsqrt(ms + EPS) * g


def _rms_heads(x, g2):
    lo = lax.broadcasted_iota(jnp.int32, x.shape, 1) < HEAD_DIM
    t = x * x
    s_lo = jnp.sum(jnp.where(lo, t, 0.0), axis=-1, keepdims=True)
    s_hi = jnp.sum(jnp.where(lo, 0.0, t), axis=-1, keepdims=True)
    ms = jnp.where(lo, s_lo, s_hi) * (1.0 / HEAD_DIM)
    return x * lax.rsqrt(ms + EPS) * g2


def _rope(x, cos, sin_signed):
    first = (lax.broadcasted_iota(jnp.int32, x.shape, 1) % 32) < 16
    partner = jnp.where(first, pltpu.roll(x, LANES - 16, 1), pltpu.roll(x, 16, 1))
    return x * cos + partner * sin_signed


def _shift_rows(x, k, fill, n_rows):
    rows = lax.broadcasted_iota(jnp.int32, x.shape, 0)
    if k > 0:
        return jnp.where(rows >= k, pltpu.roll(x, k, 0), fill)
    return jnp.where(rows < n_rows + k, pltpu.roll(x, n_rows + k, 0), fill)


def _linear_scan(a, x, n_rows, reverse):
    k = 1
    while k < n_rows:
        kk = -k if reverse else k
        x = a * _shift_rows(x, kk, 0.0, n_rows) + x
        if 2 * k < n_rows:
            a = a * _shift_rows(a, kk, 1.0, n_rows)
        k *= 2
    return x


def _attend_t(k, qt, vt):
    s = _dot(k, qt)
    m = jnp.max(s, axis=0, keepdims=True)
    e = jnp.exp(s - m)
    l = jnp.sum(e, axis=0, keepdims=True)
    return _dot(vt, e.astype(BF16)) / l


def _mod_kernel(cond_ref, w_ref, b_ref, o_ref):
    o_ref[0] = jnp.dot(_silu(cond_ref[...]), w_ref[0], preferred_element_type=F32,
                       precision=lax.Precision.HIGHEST) + b_ref[0]


def _layer_kernel(*refs, is_ctx, n_seq, seq_len, past_len, depth, layer0):
    T = seq_len
    Tk = past_len + T
    n_chunks = T // ROW_CHUNK
    it = iter(refs)
    x_ref, mod_ref, gpre_ref, gpost_ref, win_ref, wout_ref = (next(it) for _ in range(6))
    convw_ref, convb_ref, wg_ref, bg_ref, lam_ref = (next(it) for _ in range(5))
    gq_ref, gk_ref, dlam_ref, gsub_ref = (next(it) for _ in range(4))
    if not is_ctx:
        ck_ref, cv_ref, cdk_ref, cdv_ref, st_ref, cos_ref, sin_ref = (next(it) for _ in range(7))
    y_ref = next(it)
    if is_ctx:
        ogk_ref, ogv_ref, odk_ref, odv_ref, ost_ref = (next(it) for _ in range(5))
    (h_scr, mix_scr, zx_scr, qt_scr, k_scr, vt_scr, dqt_scr, dk_scr, dvt_scr) = (next(it) for _ in range(9))

    @pl.when(pl.program_id(1) == 0)
    def _():
        y_ref[...] = x_ref[...]

    layer = pl.program_id(1) + layer0

    lam_i = jnp.float32(_lambda_init(depth - 1))
    for lyr in range(depth - 2, -1, -1):
        lam_i = jnp.where(layer == lyr, jnp.float32(_lambda_init(lyr)), lam_i)

    shift = mod_ref[0, 0, :, 0:1024]
    scale = mod_ref[0, 0, :, 1024:2048]
    gate = mod_ref[0, 0, :, 2048:3072]
    dl = dlam_ref[0]
    lam = (jnp.exp(jnp.sum(dl[0:1] * dl[1:2], axis=-1, keepdims=True))
           - jnp.exp(jnp.sum(dl[2:3] * dl[3:4], axis=-1, keepdims=True)) + lam_i)

    def win(c0, n):
        return win_ref[0, :, c0:c0 + n]

    def rows_of(c):
        return pl.ds(c * ROW_CHUNK, ROW_CHUNK)

    def one_sequence(s):
        for c in range(n_chunks):
            r = rows_of(c)
            hn = _rms(y_ref[s, r, :], gpre_ref[0]) * (1.0 + scale) + shift
            h_scr[r, :] = hn.astype(BF16)

        for c in range(n_chunks):
            r = rows_of(c)
            zxg = _dot(h_scr[r, :], win(C_LRU_X, 2 * LRU_WIDTH))
            zx_scr[r, 0:LRU_WIDTH] = zxg[:, 0:LRU_WIDTH]
            zx_scr[r, LRU_WIDTH:] = _silu(zxg[:, LRU_WIDTH:])
        row_id = lax.broadcasted_iota(jnp.int32, (T, LANES), 0)
        for ct in range(LRU_WIDTH // LANES):
            cl = slice(ct * LANES, (ct + 1) * LANES)
            zx = zx_scr[:, cl]
            u = convb_ref[0, :, cl] + zx * convw_ref[0, CONV_LEFT:CONV_LEFT + 1, cl]
            for j in range(CONV_W):
                if j != CONV_LEFT:
                    u = u + _shift_rows(zx, CONV_LEFT - j, 0.0, T) * convw_ref[0, j:j + 1, cl]
            u_bf = u.astype(BF16)
            y_sum = None
            for d in range(2):
                reverse = d == 1
                gates = _dot(u_bf, wg_ref[0, ct, :, d * 2 * LANES:(d + 1) * 2 * LANES])
                gates = gates + bg_ref[0, ct, :, d * 2 * LANES:(d + 1) * 2 * LANES]
                r_g = jax.nn.sigmoid(gates[:, 0:LANES])
                i_g = jax.nn.sigmoid(gates[:, LANES:])
                lam_d = lam_ref[0, d:d + 1, cl]
                sp = jnp.maximum(-lam_d, 0.0) + jnp.log1p(jnp.exp(-jnp.abs(lam_d)))
                a = jnp.exp(-LRU_C * r_g * sp)
                inp = jnp.sqrt(1.0 - a * a) * (i_g * u)
                if not is_ctx:
                    h0 = st_ref[0, 0, d:d + 1, cl]
                    first_row = (T - 1) if reverse else 0
                    inp = inp + jnp.where(row_id == first_row, a * h0, 0.0)
                y_d = _linear_scan(a, inp, T, reverse)
                if is_ctx:
                    last_row = 0 if reverse else T - 1
                    ost_ref[s, 0, d:d + 1, cl] = y_d[last_row:last_row + 1, :]
                y_sum = y_d if y_sum is None else y_sum + y_d
            lru_gate = zx_scr[:, LRU_WIDTH + ct * LANES:LRU_WIDTH + (ct + 1) * LANES]
            mix_scr[:, M_LRU + ct * LANES:M_LRU + (ct + 1) * LANES] = (y_sum * lru_gate).astype(BF16)

        if not is_ctx:
            for c in range(past_len // ROW_CHUNK):
                r = rows_of(c)
                k_scr[r, :] = ck_ref[0, 0, :, r].T.astype(BF16)
                vt_scr[:, r] = cv_ref[0, 0, :, r].astype(BF16)
                for hd in range(DIFF_HEADS):
                    cl = slice(hd * LANES, (hd + 1) * LANES)
                    dk_scr[r, cl] = cdk_ref[0, 0, cl, r].T.astype(BF16)
                    rows_h = pl.ds(c * ROW_CHUNK * DIFF_HEADS + hd, ROW_CHUNK, stride=DIFF_HEADS)
                    dvt_scr[cl, r] = cdv_ref[0, 0, rows_h, :].T.astype(BF16)
        for c in range(n_chunks):
            r = rows_of(c)
            rk = pl.ds(past_len + c * ROW_CHUNK, ROW_CHUNK)
            hc = h_scr[r, :]
            zkv = _dot(hc, win(C_GK, 2 * GQA_KV_WIDTH))
            k = _rms_heads(zkv[:, 0:GQA_KV_WIDTH], gk_ref[0])
            v_t = zkv[:, GQA_KV_WIDTH:].T
            zdk = _dot(hc, win(C_DK, DIFF_WIDTH))
            zdv = _dot(hc, win(C_DV, DIFF_WIDTH))
            if is_ctx:
                ogk_ref[s, 0, :, r] = k.T
                ogv_ref[s, 0, :, r] = v_t
            else:
                k = _rope(k, cos_ref[r, :], sin_ref[r, :])
            k_scr[rk, :] = k.astype(BF16)
            vt_scr[:, rk] = v_t.astype(BF16)
            for hd in range(DIFF_HEADS):
                cl = slice(hd * LANES, (hd + 1) * LANES)
                dk = zdk[:, cl]
                if is_ctx:
                    odk_ref[s, 0, cl, r] = dk.T
                    rows_h = pl.ds(c * ROW_CHUNK * DIFF_HEADS + hd, ROW_CHUNK, stride=DIFF_HEADS)
                    odv_ref[s, 0, rows_h, :] = zdv[:, cl]
                else:
                    dk = _rope(dk, cos_ref[r, :], sin_ref[r, :])
                dk_scr[rk, cl] = dk.astype(BF16)
                dvt_scr[cl, rk] = zdv[:, cl].T.astype(BF16)

        for c in range(n_chunks):
            r = rows_of(c)
            hc = h_scr[r, :]
            zq = _dot(hc, win(C_GQ, GQA_WIDTH))
            zdq = _dot(hc, win(C_DQ, DIFF_WIDTH))
            for j in range(GQA_WIDTH // LANES):
                cl = slice(j * LANES, (j + 1) * LANES)
                q = _rms_heads(zq[:, cl], gq_ref[0])
                if not is_ctx:
                    q = _rope(q, cos_ref[r, :], sin_ref[r, :])
                qt_scr[c, cl, :] = (q * QK_SCALE).T.astype(BF16)
            for hd in range(DIFF_HEADS):
                cl = slice(hd * LANES, (hd + 1) * LANES)
                dq = zdq[:, cl]
                if not is_ctx:
                    dq = _rope(dq, cos_ref[r, :], sin_ref[r, :])
                dqt_scr[c, cl, :] = (dq * QK_SCALE).T.astype(BF16)

        zero_half = jnp.zeros((HEAD_DIM, ROW_CHUNK), BF16)

        def attn_chunk(c, carry):
            r = pl.ds(pl.multiple_of(c * ROW_CHUNK, ROW_CHUNK), ROW_CHUNK)
            hc = h_scr[r, :]
            g_gate = _silu(_dot(hc, win(C_GG, GQA_WIDTH)))
            k_all = k_scr[...]
            vt_all = vt_scr[...]
            group = GQA_Q_HEADS // GQA_KV_HEADS
            for kv in range(GQA_KV_HEADS):
                qts = []
                for i in range(group):
                    head = kv * group + i
                    q_h = qt_scr[c, head * HEAD_DIM:(head + 1) * HEAD_DIM, :]
                    qts.append(jnp.concatenate([q_h, zero_half] if kv == 0 else [zero_half, q_h], axis=0))
                o_t = _attend_t(k_all, jnp.concatenate(qts, axis=1), vt_all)
                o = jnp.concatenate(
                    [o_t[kv * HEAD_DIM:(kv + 1) * HEAD_DIM, i * ROW_CHUNK:(i + 1) * ROW_CHUNK]
                     for i in range(group)], axis=0).T
                cl = slice(kv * LANES, (kv + 1) * LANES)
                mix_scr[r, M_GQA + kv * LANES:M_GQA + (kv + 1) * LANES] = (o * g_gate[:, cl]).astype(BF16)

            d_gate = _silu(_dot(hc, win(C_DG, DIFF_WIDTH)))
            for hd in range(DIFF_HEADS):
                cl = slice(hd * LANES, (hd + 1) * LANES)
                k_h = dk_scr[:, cl]
                vt_h = dvt_scr[cl, :]
                q1 = dqt_scr[c, hd * LANES:hd * LANES + HEAD_DIM, :]
                q2 = dqt_scr[c, hd * LANES + HEAD_DIM:(hd + 1) * LANES, :]
                qt = jnp.concatenate([jnp.concatenate([q1, zero_half], axis=0),
                                      jnp.concatenate([zero_half, q2], axis=0)], axis=1)
                o_t = _attend_t(k_h, qt, vt_h)
                o = _rms((o_t[:, 0:ROW_CHUNK] - lam * o_t[:, ROW_CHUNK:]).T, gsub_ref[0]) * (1.0 - lam_i)
                mix_scr[r, M_DIFF + hd * LANES:M_DIFF + (hd + 1) * LANES] = (o * d_gate[:, cl]).astype(BF16)
            return carry

        lax.fori_loop(0, n_chunks, attn_chunk, 0)

        for c in range(n_chunks):
            r = rows_of(c)
            out = _dot(mix_scr[r, :], wout_ref[0])
            y_ref[s, r, :] = y_ref[s, r, :] + gate * _rms(out, gpost_ref[0])

    if n_seq == 1:
        one_sequence(0)
    else:
        def seq_body(s, carry):
            one_sequence(s)
            return carry
        lax.fori_loop(0, n_seq, seq_body, 0)


def _rope_tables(seq_len):
    quarter = HEAD_DIM // 4
    inv = jnp.power(ROPE_BASE, -jnp.arange(quarter, dtype=F32) / quarter)
    t = jnp.arange(seq_len)
    row = (t // GRID_W).astype(F32)[:, None] * inv[None]
    col = (t % GRID_W).astype(F32)[:, None] * inv[None]
    cos = jnp.concatenate([jnp.cos(row), jnp.cos(row), jnp.cos(col), jnp.cos(col)], axis=-1)
    sin = jnp.concatenate([-jnp.sin(row), jnp.sin(row), -jnp.sin(col), jnp.sin(col)], axis=-1)
    reps = LANES // HEAD_DIM
    return jnp.tile(cos, (1, reps)), jnp.tile(sin, (1, reps))


def _block_diag(w):
    nb, bw, _ = w.shape
    eye = jnp.eye(nb, dtype=w.dtype)
    return (eye[:, None, :, None] * w[:, :, None, :]).reshape(nb * bw, nb * bw)


def _layer_call(x, mod, weights, extras, *, is_ctx, n_seq, layer0, n_layers):
    batch, seq_len, d_model = x.shape
    depth = weights[2].shape[0]
    past_len = 0 if is_ctx else extras[0].shape[3]
    tk = past_len + seq_len
    n_chunks = seq_len // ROW_CHUNK
    grid = (batch // n_seq, n_layers)
    single = dict(pipeline_mode=pl.Buffered(1))
    w_mode = single if n_layers == 1 else {}

    def per_layer(a):
        nd = a.ndim
        return pl.BlockSpec((1,) + a.shape[1:], lambda b, l: (l + layer0,) + (0,) * (nd - 1), **w_mode)

    y_spec = pl.BlockSpec((n_seq, seq_len, d_model), lambda b, l: (b, 0, 0))
    if is_ctx:
        x_spec = y_spec
        mod_spec = pl.BlockSpec((1, 1, 1, mod.shape[-1]), lambda b, l: (l + layer0, 0, 0, 0))
    else:
        x_spec = pl.BlockSpec((n_seq, seq_len, d_model), lambda b, l: (b, 0, 0), **single)
        mod_spec = pl.BlockSpec((1, 1, 1, mod.shape[-1]), lambda b, l: (l + layer0, b + 1, 0, 0))
    in_specs = [x_spec, mod_spec] + [per_layer(w) for w in weights]
    args = [x, mod] + list(weights)
    if not is_ctx:
        ck, cv, cdk, cdv, st, cos, sin = extras
        for a in (ck, cv, cdk, cdv, st):
            in_specs.append(
                pl.BlockSpec((1, 1) + a.shape[2:], lambda b, l: (b, l + layer0, 0, 0), **single))
        for a in (cos, sin):
            in_specs.append(pl.BlockSpec(a.shape, lambda b, l: (0, 0), **single))
        args += list(extras)

    out_shape = [jax.ShapeDtypeStruct(x.shape, F32)]
    out_specs = [y_spec]
    if is_ctx:
        for rows_, cols_ in ((GQA_KV_WIDTH, seq_len), (GQA_KV_WIDTH, seq_len), (DIFF_WIDTH, seq_len),
                             (seq_len * DIFF_HEADS, DIFF_WIDTH // DIFF_HEADS)):
            out_shape.append(jax.ShapeDtypeStruct((batch, depth, rows_, cols_), F32))
            out_specs.append(pl.BlockSpec((n_seq, 1, rows_, cols_), lambda b, l: (b, l, 0, 0)))
        out_shape.append(jax.ShapeDtypeStruct((batch, depth, 2, LRU_WIDTH), F32))
        out_specs.append(pl.BlockSpec((n_seq, 1, 2, LRU_WIDTH), lambda b, l: (b, l, 0, 0)))

    scratch = [
        pltpu.VMEM((seq_len, d_model), BF16),
        pltpu.VMEM((seq_len, D_MIX), BF16),
        pltpu.VMEM((seq_len, 2 * LRU_WIDTH), F32),
        pltpu.VMEM((n_chunks, GQA_WIDTH, ROW_CHUNK), BF16),
        pltpu.VMEM((tk, GQA_KV_WIDTH), BF16),
        pltpu.VMEM((GQA_KV_WIDTH, tk), BF16),
        pltpu.VMEM((n_chunks, DIFF_WIDTH, ROW_CHUNK), BF16),
        pltpu.VMEM((tk, DIFF_WIDTH), BF16),
        pltpu.VMEM((DIFF_WIDTH, tk), BF16),
    ]
    kern = functools.partial(_layer_kernel, is_ctx=is_ctx, n_seq=n_seq, seq_len=seq_len,
                             past_len=past_len, depth=depth, layer0=layer0)
    return pl.pallas_call(
        kern,
        grid=grid,
        in_specs=in_specs,
        out_specs=out_specs,
        out_shape=out_shape,
        scratch_shapes=scratch,
        compiler_params=pltpu.CompilerParams(
            dimension_semantics=("arbitrary", "arbitrary"),
            vmem_limit_bytes=VMEM_LIMIT),
        name="ctx_pass" if is_ctx else f"denoise_layer{layer0}",
    )(*args)


def kernel(x_prompt, x_sample, cache_gqa_k, cache_gqa_v, cache_diff_k, cache_diff_v, state_lru, c, c_ctx, w_mod, b_mod, g_pre, g_post, w_in, w_out, lru_conv_w, lru_conv_b, lru_wa, lru_ba, lru_wx, lru_bx, lru_lambda, gqa_gq, gqa_gk, diff_lam, diff_gsub):
    depth, d_model, _ = w_in.shape
    dec_batch = x_sample.shape[0]

    cond = jnp.concatenate(
        [c_ctx[None, :], c, jnp.zeros((MOD_ROWS - 1 - dec_batch, d_model), F32)], axis=0)
    n_mod = w_mod.shape[-1]
    mod_tile = n_mod // 3
    mod = pl.pallas_call(
        _mod_kernel,
        grid=(depth, n_mod // mod_tile),
        in_specs=[pl.BlockSpec((MOD_ROWS, d_model), lambda l, j: (0, 0)),
                  pl.BlockSpec((1, d_model, mod_tile), lambda l, j: (l, 0, j)),
                  pl.BlockSpec((1, 1, mod_tile), lambda l, j: (l, 0, j))],
        out_specs=pl.BlockSpec((1, MOD_ROWS, mod_tile), lambda l, j: (l, 0, j)),
        out_shape=jax.ShapeDtypeStruct((depth, MOD_ROWS, n_mod), F32),
        name="adaln_mod",
    )(cond, w_mod, b_mod[:, None, :])
    mod = mod[:, :, None, :]

    w_in_p = w_in.astype(BF16)
    w_out_p = w_out.astype(BF16)
    bd = jax.vmap(jax.vmap(_block_diag))
    wa_d, wx_d = bd(lru_wa), bd(lru_wx)
    n_ct = LRU_WIDTH // LANES
    wg = jnp.stack([
        jnp.concatenate([m[:, d, ct * LANES:(ct + 1) * LANES, ct * LANES:(ct + 1) * LANES]
                         for d in range(2) for m in (wa_d, wx_d)], axis=-1)
        for ct in range(n_ct)], axis=1).astype(BF16)
    bg = jnp.stack([
        jnp.concatenate([m[:, d, ct * LANES:(ct + 1) * LANES]
                         for d in range(2) for m in (lru_ba, lru_bx)], axis=-1)
        for ct in range(n_ct)], axis=1)[:, :, None, :]
    reps = LANES // HEAD_DIM
    weights = [g_pre[:, None, :], g_post[:, None, :], w_in_p, w_out_p, lru_conv_w,
               lru_conv_b[:, None, :], wg, bg, lru_lambda,
               jnp.tile(gqa_gq, (1, reps))[:, None, :], jnp.tile(gqa_gk, (1, reps))[:, None, :],
               diff_lam, diff_gsub[:, None, :]]

    y_prompt, gk, gv, dk, dv, st = _layer_call(x_prompt, mod, weights, None, is_ctx=True, n_seq=2,
                                               layer0=0, n_layers=depth)
    b, t = x_prompt.shape[:2]
    new_gqa_k = gk.reshape(b, depth, GQA_KV_HEADS, HEAD_DIM, t).transpose(0, 1, 4, 2, 3)
    new_gqa_v = gv.reshape(b, depth, GQA_KV_HEADS, HEAD_DIM, t).transpose(0, 1, 4, 2, 3)
    new_diff_k = dk.reshape(b, depth, DIFF_HEADS, 2, HEAD_DIM, t).transpose(0, 1, 5, 2, 3, 4)
    new_diff_v = dv.reshape(b, depth, t, DIFF_HEADS, 2 * HEAD_DIM)

    db, _, past = cache_gqa_k.shape[:3]
    cos, sin = _rope_tables(x_sample.shape[1])
    extras = [cache_gqa_k.transpose(0, 1, 3, 4, 2).reshape(db, depth, GQA_KV_WIDTH, past),
              cache_gqa_v.transpose(0, 1, 3, 4, 2).reshape(db, depth, GQA_KV_WIDTH, past),
              cache_diff_k.transpose(0, 1, 3, 4, 5, 2).reshape(db, depth, DIFF_WIDTH, past),
              cache_diff_v.reshape(db, depth, past * DIFF_HEADS, 2 * HEAD_DIM),
              state_lru, cos, sin]
    y_sample = x_sample
    for lyr in range(depth):
        (y_sample,) = _layer_call(y_sample, mod, weights, extras, is_ctx=False, n_seq=1,
                                  layer0=lyr, n_layers=1)

    return (y_prompt, y_sample, new_gqa_k, new_gqa_v, new_diff_k, new_diff_v, st)
```

```python
import functools
import math

import jax
import jax.numpy as jnp
from jax import lax
from jax.experimental import pallas as pl
from jax.experimental.pallas import tpu as pltpu

F32 = jnp.float32
BF16 = jnp.bfloat16

GRID_W = 64
HEAD_DIM = 64
EPS = 1e-6
ROPE_BASE = 10000.0
LRU_WIDTH = 256
LRU_BLOCKS = 4
LRU_C = 8.0
CONV_W = 4
CONV_LEFT = 2
GQA_Q_HEADS = 4
GQA_KV_HEADS = 2
GQA_WIDTH = GQA_Q_HEADS * HEAD_DIM
GQA_KV_WIDTH = GQA_KV_HEADS * HEAD_DIM
DIFF_HEADS = 4
DIFF_WIDTH = DIFF_HEADS * 2 * HEAD_DIM
QK_SCALE = HEAD_DIM ** -0.5 * math.log2(math.e)

C_LRU_X = 0
C_LRU_G = C_LRU_X + LRU_WIDTH
C_GQ = C_LRU_G + LRU_WIDTH
C_GK = C_GQ + GQA_WIDTH
C_GV = C_GK + GQA_KV_WIDTH
C_GG = C_GV + GQA_KV_WIDTH
C_DQ = C_GG + GQA_WIDTH
C_DK = C_DQ + DIFF_WIDTH
C_DV = C_DK + DIFF_WIDTH
C_DG = C_DV + DIFF_WIDTH
N_IN = C_DG + DIFF_WIDTH
M_LRU = 0
M_GQA = LRU_WIDTH
M_DIFF = LRU_WIDTH + GQA_WIDTH
D_MIX = LRU_WIDTH + GQA_WIDTH + DIFF_WIDTH

LANES = 128
SUBLANES = 8
ROW_CHUNK = 256
KEY_CHUNK = 256
MOD_ROWS = 16
VMEM_LIMIT = 56 * 1024 * 1024


def _lambda_init(layer):
    return 0.8 - 0.6 * math.exp(-0.3 * layer)


def _dot(a, b):
    return jnp.dot(a, b, preferred_element_type=F32)


def _silu(x):
    return x * jax.nn.sigmoid(x)


def _rms(x, g):
    ms = jnp.mean(x * x, axis=-1, keepdims=True)
    return x * lax.rsqrt(ms + EPS) * g


def _rms_heads(x, g2):
    lo = lax.broadcasted_iota(jnp.int32, x.shape, 1) < HEAD_DIM
    t = x * x
    s_lo = jnp.sum(jnp.where(lo, t, 0.0), axis=-1, keepdims=True)
    s_hi = jnp.sum(jnp.where(lo, 0.0, t), axis=-1, keepdims=True)
    ms = jnp.where(lo, s_lo, s_hi) * (1.0 / HEAD_DIM)
    return x * lax.rsqrt(ms + EPS) * g2


def _rope(x, cos, sin_signed):
    first = (lax.broadcasted_iota(jnp.int32, x.shape, 1) % 32) < 16
    partner = jnp.where(first, pltpu.roll(x, LANES - 16, 1), pltpu.roll(x, 16, 1))
    return x * cos + partner * sin_signed


def _shift_rows(x, k, fill, n_rows):
    rows = lax.broadcasted_iota(jnp.int32, x.shape, 0)
    if k > 0:
        return jnp.where(rows >= k, pltpu.roll(x, k, 0), fill)
    return jnp.where(rows < n_rows + k, pltpu.roll(x, n_rows + k, 0), fill)


def _linear_scan(a, x, n_rows, reverse):
    nb = n_rows // SUBLANES
    lanes = a.shape[-1]
    a3 = a.reshape(nb, SUBLANES, lanes)
    x3 = x.reshape(nb, SUBLANES, lanes)
    sub = lax.broadcasted_iota(jnp.int32, (nb, SUBLANES, lanes), 1)
    k = 1
    while k < SUBLANES:
        keep = (sub < SUBLANES - k) if reverse else (sub >= k)
        shift = SUBLANES - k if reverse else k
        x3 = a3 * jnp.where(keep, pltpu.roll(x3, shift, 1), 0.0) + x3
        a3 = a3 * jnp.where(keep, pltpu.roll(a3, shift, 1), 1.0)
        k *= 2
    blocks = [None] * nb
    carry = None
    for j in (range(nb - 1, -1, -1) if reverse else range(nb)):
        xb = x3[j]
        if carry is not None:
            xb = xb + a3[j] * carry
        blocks[j] = xb
        carry = xb[0:1] if reverse else xb[SUBLANES - 1:SUBLANES]
    return jnp.concatenate(blocks, axis=0)


def _mod_kernel(cond_ref, w_ref, b_ref, o_ref):
    o_ref[0] = jnp.dot(_silu(cond_ref[...]), w_ref[0], preferred_element_type=F32,
                       precision=lax.Precision.HIGHEST) + b_ref[0]


def _layer_kernel(*refs, is_ctx, n_seq, seq_len, past_len, depth, layer0):
    T = seq_len
    Tk = past_len + T
    n_chunks = T // ROW_CHUNK
    it = iter(refs)
    x_ref, mod_ref, gpre_ref, gpost_ref, win_ref, wout_ref = (next(it) for _ in range(6))
    convw_ref, convb_ref, wg_ref, bg_ref, lam_ref = (next(it) for _ in range(5))
    gq_ref, gk_ref, dlam_ref, gsub_ref = (next(it) for _ in range(4))
    if not is_ctx:
        ck_ref, cv_ref, cdk_ref, cdv_ref, st_ref, cos_ref, sin_ref = (next(it) for _ in range(7))
    y_ref = next(it)
    if is_ctx:
        ogk_ref, ogv_ref, odk_ref, odv_ref, ost_ref = (next(it) for _ in range(5))
    (h_scr, mix_scr, zx_scr, qt_scr, k_scr, vt_scr, dqt_scr, dk_scr, dvt_scr, s_scr, e_scr) = (
        next(it) for _ in range(11))

    @pl.when(pl.program_id(1) == 0)
    def _():
        y_ref[...] = x_ref[...]

    layer = pl.program_id(1) + layer0

    lam_i = jnp.float32(_lambda_init(depth - 1))
    for lyr in range(depth - 2, -1, -1):
        lam_i = jnp.where(layer == lyr, jnp.float32(_lambda_init(lyr)), lam_i)

    shift = mod_ref[0, 0, :, 0:1024]
    scale = mod_ref[0, 0, :, 1024:2048]
    gate = mod_ref[0, 0, :, 2048:3072]
    dl = dlam_ref[0]
    lam = (jnp.exp(jnp.sum(dl[0:1] * dl[1:2], axis=-1, keepdims=True))
           - jnp.exp(jnp.sum(dl[2:3] * dl[3:4], axis=-1, keepdims=True)) + lam_i)

    def win(c0, n):
        return win_ref[0, :, c0:c0 + n]

    def rows_of(c):
        return pl.ds(c * ROW_CHUNK, ROW_CHUNK)

    def one_sequence(s):
        for c in range(n_chunks):
            r = rows_of(c)
            hn = _rms(y_ref[s, r, :], gpre_ref[0]) * (1.0 + scale) + shift
            h_scr[r, :] = hn.astype(BF16)

        for c in range(n_chunks):
            r = rows_of(c)
            zxg = _dot(h_scr[r, :], win(C_LRU_X, 2 * LRU_WIDTH))
            zx_scr[r, 0:LRU_WIDTH] = zxg[:, 0:LRU_WIDTH]
            zx_scr[r, LRU_WIDTH:] = _silu(zxg[:, LRU_WIDTH:])
        row_id = lax.broadcasted_iota(jnp.int32, (T, LANES), 0)
        for ct in range(LRU_WIDTH // LANES):
            cl = slice(ct * LANES, (ct + 1) * LANES)
            zx = zx_scr[:, cl]
            u = convb_ref[0, :, cl] + zx * convw_ref[0, CONV_LEFT:CONV_LEFT + 1, cl]
            for j in range(CONV_W):
                if j != CONV_LEFT:
                    u = u + _shift_rows(zx, CONV_LEFT - j, 0.0, T) * convw_ref[0, j:j + 1, cl]
            u_bf = u.astype(BF16)
            y_sum = None
            for d in range(2):
                reverse = d == 1
                gates = _dot(u_bf, wg_ref[0, ct, :, d * 2 * LANES:(d + 1) * 2 * LANES])
                gates = gates + bg_ref[0, ct, :, d * 2 * LANES:(d + 1) * 2 * LANES]
                r_g = jax.nn.sigmoid(gates[:, 0:LANES])
                i_g = jax.nn.sigmoid(gates[:, LANES:])
                lam_d = lam_ref[0, d:d + 1, cl]
                sp = jnp.maximum(-lam_d, 0.0) + jnp.log1p(jnp.exp(-jnp.abs(lam_d)))
                a = jnp.exp(-LRU_C * r_g * sp)
                inp = jnp.sqrt(1.0 - a * a) * (i_g * u)
                if not is_ctx:
                    h0 = st_ref[0, 0, d:d + 1, cl]
                    first_row = (T - 1) if reverse else 0
                    inp = inp + jnp.where(row_id == first_row, a * h0, 0.0)
                y_d = _linear_scan(a, inp, T, reverse)
                if is_ctx:
                    last_row = 0 if reverse else T - 1
                    ost_ref[s, 0, d:d + 1, cl] = y_d[last_row:last_row + 1, :]
                y_sum = y_d if y_sum is None else y_sum + y_d
            lru_gate = zx_scr[:, LRU_WIDTH + ct * LANES:LRU_WIDTH + (ct + 1) * LANES]
            mix_scr[:, M_LRU + ct * LANES:M_LRU + (ct + 1) * LANES] = (y_sum * lru_gate).astype(BF16)

        if not is_ctx:
            for c in range(past_len // ROW_CHUNK):
                r = rows_of(c)
                k_scr[r, :] = ck_ref[0, 0, :, r].T.astype(BF16)
                vt_scr[:, r] = cv_ref[0, 0, :, r].astype(BF16)
                for hd in range(DIFF_HEADS):
                    cl = slice(hd * LANES, (hd + 1) * LANES)
                    dk_scr[r, cl] = cdk_ref[0, 0, cl, r].T.astype(BF16)
                    rows_h = pl.ds(c * ROW_CHUNK * DIFF_HEADS + hd, ROW_CHUNK, stride=DIFF_HEADS)
                    dvt_scr[cl, r] = cdv_ref[0, 0, rows_h, :].T.astype(BF16)
        for c in range(n_chunks):
            r = rows_of(c)
            rk = pl.ds(past_len + c * ROW_CHUNK, ROW_CHUNK)
            hc = h_scr[r, :]
            zkv = _dot(hc, win(C_GK, 2 * GQA_KV_WIDTH))
            k = _rms_heads(zkv[:, 0:GQA_KV_WIDTH], gk_ref[0])
            v_t = zkv[:, GQA_KV_WIDTH:].T
            zdk = _dot(hc, win(C_DK, DIFF_WIDTH))
            zdv = _dot(hc, win(C_DV, DIFF_WIDTH))
            if is_ctx:
                ogk_ref[s, 0, :, r] = k.T
                ogv_ref[s, 0, :, r] = v_t
            else:
                k = _rope(k, cos_ref[r, :], sin_ref[r, :])
            k_scr[rk, :] = k.astype(BF16)
            vt_scr[:, rk] = v_t.astype(BF16)
            for hd in range(DIFF_HEADS):
                cl = slice(hd * LANES, (hd + 1) * LANES)
                dk = zdk[:, cl]
                if is_ctx:
                    odk_ref[s, 0, cl, r] = dk.T
                    rows_h = pl.ds(c * ROW_CHUNK * DIFF_HEADS + hd, ROW_CHUNK, stride=DIFF_HEADS)
                    odv_ref[s, 0, rows_h, :] = zdv[:, cl]
                else:
                    dk = _rope(dk, cos_ref[r, :], sin_ref[r, :])
                dk_scr[rk, cl] = dk.astype(BF16)
                dvt_scr[cl, rk] = zdv[:, cl].T.astype(BF16)

        for c in range(n_chunks):
            r = rows_of(c)
            hc = h_scr[r, :]
            zq = _dot(hc, win(C_GQ, GQA_WIDTH))
            zdq = _dot(hc, win(C_DQ, DIFF_WIDTH))
            for j in range(GQA_WIDTH // LANES):
                cl = slice(j * LANES, (j + 1) * LANES)
                q = _rms_heads(zq[:, cl], gq_ref[0])
                if not is_ctx:
                    q = _rope(q, cos_ref[r, :], sin_ref[r, :])
                qt_scr[c, cl, :] = (q * QK_SCALE).T.astype(BF16)
            for hd in range(DIFF_HEADS):
                cl = slice(hd * LANES, (hd + 1) * LANES)
                dq = zdq[:, cl]
                if not is_ctx:
                    dq = _rope(dq, cos_ref[r, :], sin_ref[r, :])
                dqt_scr[c, cl, :] = (dq * QK_SCALE).T.astype(BF16)

        zero_half = jnp.zeros((HEAD_DIM, ROW_CHUNK), BF16)

        group = GQA_Q_HEADS // GQA_KV_HEADS
        n_pairs = GQA_KV_HEADS + DIFF_HEADS
        n_kc = Tk // KEY_CHUNK

        def attn_chunk(c, carry):
            r = pl.ds(pl.multiple_of(c * ROW_CHUNK, ROW_CHUNK), ROW_CHUNK)
            hc = h_scr[r, :]

            def pair_qt(p):
                if p < GQA_KV_HEADS:
                    heads = [qt_scr[c, (p * group + i) * HEAD_DIM:(p * group + i + 1) * HEAD_DIM, :]
                             for i in range(group)]
                    cols = [jnp.concatenate([q, zero_half] if p == 0 else [zero_half, q], axis=0)
                            for q in heads]
                else:
                    hd = p - GQA_KV_HEADS
                    q1 = dqt_scr[c, hd * LANES:hd * LANES + HEAD_DIM, :]
                    q2 = dqt_scr[c, hd * LANES + HEAD_DIM:(hd + 1) * LANES, :]
                    cols = [jnp.concatenate([q1, zero_half], axis=0),
                            jnp.concatenate([zero_half, q2], axis=0)]
                return jnp.concatenate(cols, axis=1)

            def pair_keys(p, i):
                rk = slice(i * KEY_CHUNK, (i + 1) * KEY_CHUNK)
                if p < GQA_KV_HEADS:
                    return k_scr[rk, :]
                hd = p - GQA_KV_HEADS
                return dk_scr[rk, hd * LANES:(hd + 1) * LANES]

            def pair_values_t(p):
                if p < GQA_KV_HEADS:
                    return vt_scr[...]
                hd = p - GQA_KV_HEADS
                return dvt_scr[hd * LANES:(hd + 1) * LANES, :]

            def scores(p, qt, i, m):
                sc = _dot(pair_keys(p, i), qt)
                s_scr[p % 2, i * KEY_CHUNK:(i + 1) * KEY_CHUNK, :] = sc
                mi = jnp.max(sc, axis=0, keepdims=True)
                return mi if m is None else jnp.maximum(m, mi)

            def finish(p, o_t, gates):
                if p < GQA_KV_HEADS:
                    o = jnp.concatenate(
                        [o_t[p * HEAD_DIM:(p + 1) * HEAD_DIM, i * ROW_CHUNK:(i + 1) * ROW_CHUNK]
                         for i in range(group)], axis=0).T
                    col = M_GQA + p * LANES
                    gt = gates[0][:, p * LANES:(p + 1) * LANES]
                else:
                    hd = p - GQA_KV_HEADS
                    o = _rms((o_t[:, 0:ROW_CHUNK] - lam * o_t[:, ROW_CHUNK:]).T, gsub_ref[0]) * (1.0 - lam_i)
                    col = M_DIFF + hd * LANES
                    gt = gates[1][:, hd * LANES:(hd + 1) * LANES]
                mix_scr[r, col:col + LANES] = (o * gt).astype(BF16)

            qt_next = pair_qt(0)
            m_next = None
            for i in range(n_kc):
                m_next = scores(0, qt_next, i, m_next)
            gates = (_silu(_dot(hc, win(C_GG, GQA_WIDTH))), _silu(_dot(hc, win(C_DG, DIFF_WIDTH))))
            for p in range(n_pairs):
                m = m_next
                m_next = None
                if p + 1 < n_pairs:
                    qt_next = pair_qt(p + 1)
                l = None
                for i in range(n_kc):
                    if p + 1 < n_pairs:
                        m_next = scores(p + 1, qt_next, i, m_next)
                    rk = slice(i * KEY_CHUNK, (i + 1) * KEY_CHUNK)
                    e = jnp.exp2(s_scr[p % 2, rk, :] - m)
                    li = jnp.sum(e, axis=0, keepdims=True)
                    l = li if l is None else l + li
                    e_scr[p % 2, rk, :] = e.astype(BF16)
                finish(p, _dot(pair_values_t(p), e_scr[p % 2]) / l, gates)
            return carry

        lax.fori_loop(0, n_chunks, attn_chunk, 0)

        for c in range(n_chunks):
            r = rows_of(c)
            out = _dot(mix_scr[r, :], wout_ref[0])
            y_ref[s, r, :] = y_ref[s, r, :] + gate * _rms(out, gpost_ref[0])

    if n_seq == 1:
        one_sequence(0)
    else:
        def seq_body(s, carry):
            one_sequence(s)
            return carry
        lax.fori_loop(0, n_seq, seq_body, 0)


def _rope_tables(seq_len):
    quarter = HEAD_DIM // 4
    inv = jnp.power(ROPE_BASE, -jnp.arange(quarter, dtype=F32) / quarter)
    t = jnp.arange(seq_len)
    row = (t // GRID_W).astype(F32)[:, None] * inv[None]
    col = (t % GRID_W).astype(F32)[:, None] * inv[None]
    cos = jnp.concatenate([jnp.cos(row), jnp.cos(row), jnp.cos(col), jnp.cos(col)], axis=-1)
    sin = jnp.concatenate([-jnp.sin(row), jnp.sin(row), -jnp.sin(col), jnp.sin(col)], axis=-1)
    reps = LANES // HEAD_DIM
    return jnp.tile(cos, (1, reps)), jnp.tile(sin, (1, reps))


def _block_diag(w):
    nb, bw, _ = w.shape
    eye = jnp.eye(nb, dtype=w.dtype)
    return (eye[:, None, :, None] * w[:, :, None, :]).reshape(nb * bw, nb * bw)


def _layer_call(x, mod, weights, extras, *, is_ctx, n_seq, layer0, n_layers):
    batch, seq_len, d_model = x.shape
    depth = weights[2].shape[0]
    past_len = 0 if is_ctx else extras[0].shape[3]
    tk = past_len + seq_len
    n_chunks = seq_len // ROW_CHUNK
    grid = (batch // n_seq, n_layers)
    single = dict(pipeline_mode=pl.Buffered(1))
    w_mode = single if n_layers == 1 else {}

    def per_layer(a):
        nd = a.ndim
        return pl.BlockSpec((1,) + a.shape[1:], lambda b, l: (l + layer0,) + (0,) * (nd - 1), **w_mode)

    y_spec = pl.BlockSpec((n_seq, seq_len, d_model), lambda b, l: (b, 0, 0))
    if is_ctx:
        x_spec = y_spec
        mod_spec = pl.BlockSpec((1, 1, 1, mod.shape[-1]), lambda b, l: (l + layer0, 0, 0, 0))
    else:
        x_spec = pl.BlockSpec((n_seq, seq_len, d_model), lambda b, l: (b, 0, 0), **single)
        mod_spec = pl.BlockSpec((1, 1, 1, mod.shape[-1]), lambda b, l: (l + layer0, b + 1, 0, 0))
    in_specs = [x_spec, mod_spec] + [per_layer(w) for w in weights]
    args = [x, mod] + list(weights)
    if not is_ctx:
        ck, cv, cdk, cdv, st, cos, sin = extras
        for a in (ck, cv, cdk, cdv, st):
            in_specs.append(
                pl.BlockSpec((1, 1) + a.shape[2:], lambda b, l: (b, l + layer0, 0, 0), **single))
        for a in (cos, sin):
            in_specs.append(pl.BlockSpec(a.shape, lambda b, l: (0, 0), **single))
        args += list(extras)

    out_shape = [jax.ShapeDtypeStruct(x.shape, F32)]
    out_specs = [y_spec]
    if is_ctx:
        for rows_, cols_ in ((GQA_KV_WIDTH, seq_len), (GQA_KV_WIDTH, seq_len), (DIFF_WIDTH, seq_len),
                             (seq_len * DIFF_HEADS, DIFF_WIDTH // DIFF_HEADS)):
            out_shape.append(jax.ShapeDtypeStruct((batch, depth, rows_, cols_), F32))
            out_specs.append(pl.BlockSpec((n_seq, 1, rows_, cols_), lambda b, l: (b, l, 0, 0)))
        out_shape.append(jax.ShapeDtypeStruct((batch, depth, 2, LRU_WIDTH), F32))
        out_specs.append(pl.BlockSpec((n_seq, 1, 2, LRU_WIDTH), lambda b, l: (b, l, 0, 0)))

    scratch = [
        pltpu.VMEM((seq_len, d_model), BF16),
        pltpu.VMEM((seq_len, D_MIX), BF16),
        pltpu.VMEM((seq_len, 2 * LRU_WIDTH), F32),
        pltpu.VMEM((n_chunks, GQA_WIDTH, ROW_CHUNK), BF16),
        pltpu.VMEM((tk, GQA_KV_WIDTH), BF16),
        pltpu.VMEM((GQA_KV_WIDTH, tk), BF16),
        pltpu.VMEM((n_chunks, DIFF_WIDTH, ROW_CHUNK), BF16),
        pltpu.VMEM((tk, DIFF_WIDTH), BF16),
        pltpu.VMEM((DIFF_WIDTH, tk), BF16),
        pltpu.VMEM((2, tk, 2 * ROW_CHUNK), F32),
        pltpu.VMEM((2, tk, 2 * ROW_CHUNK), BF16),
    ]
    kern = functools.partial(_layer_kernel, is_ctx=is_ctx, n_seq=n_seq, seq_len=seq_len,
                             past_len=past_len, depth=depth, layer0=layer0)
    return pl.pallas_call(
        kern,
        grid=grid,
        in_specs=in_specs,
        out_specs=out_specs,
        out_shape=out_shape,
        scratch_shapes=scratch,
        compiler_params=pltpu.CompilerParams(
            dimension_semantics=("arbitrary", "arbitrary"),
            vmem_limit_bytes=VMEM_LIMIT),
        name="ctx_pass" if is_ctx else f"denoise_layer{layer0}",
    )(*args)


def kernel(x_prompt, x_sample, cache_gqa_k, cache_gqa_v, cache_diff_k, cache_diff_v, state_lru, c, c_ctx, w_mod, b_mod, g_pre, g_post, w_in, w_out, lru_conv_w, lru_conv_b, lru_wa, lru_ba, lru_wx, lru_bx, lru_lambda, gqa_gq, gqa_gk, diff_lam, diff_gsub):
    depth, d_model, _ = w_in.shape
    dec_batch = x_sample.shape[0]

    cond = jnp.concatenate(
        [c_ctx[None, :], c, jnp.zeros((MOD_ROWS - 1 - dec_batch, d_model), F32)], axis=0)
    n_mod = w_mod.shape[-1]
    mod_tile = n_mod // 3
    mod = pl.pallas_call(
        _mod_kernel,
        grid=(depth, n_mod // mod_tile),
        in_specs=[pl.BlockSpec((MOD_ROWS, d_model), lambda l, j: (0, 0)),
                  pl.BlockSpec((1, d_model, mod_tile), lambda l, j: (l, 0, j)),
                  pl.BlockSpec((1, 1, mod_tile), lambda l, j: (l, 0, j))],
        out_specs=pl.BlockSpec((1, MOD_ROWS, mod_tile), lambda l, j: (l, 0, j)),
        out_shape=jax.ShapeDtypeStruct((depth, MOD_ROWS, n_mod), F32),
        name="adaln_mod",
    )(cond, w_mod, b_mod[:, None, :])
    mod = mod[:, :, None, :]

    w_in_p = w_in.astype(BF16)
    w_out_p = w_out.astype(BF16)
    bd = jax.vmap(jax.vmap(_block_diag))
    wa_d, wx_d = bd(lru_wa), bd(lru_wx)
    n_ct = LRU_WIDTH // LANES
    wg = jnp.stack([
        jnp.concatenate([m[:, d, ct * LANES:(ct + 1) * LANES, ct * LANES:(ct + 1) * LANES]
                         for d in range(2) for m in (wa_d, wx_d)], axis=-1)
        for ct in range(n_ct)], axis=1).astype(BF16)
    bg = jnp.stack([
        jnp.concatenate([m[:, d, ct * LANES:(ct + 1) * LANES]
                         for d in range(2) for m in (lru_ba, lru_bx)], axis=-1)
        for ct in range(n_ct)], axis=1)[:, :, None, :]
    reps = LANES // HEAD_DIM
    weights = [g_pre[:, None, :], g_post[:, None, :], w_in_p, w_out_p, lru_conv_w,
               lru_conv_b[:, None, :], wg, bg, lru_lambda,
               jnp.tile(gqa_gq, (1, reps))[:, None, :], jnp.tile(gqa_gk, (1, reps))[:, None, :],
               diff_lam, diff_gsub[:, None, :]]

    y_prompt, gk, gv, dk, dv, st = _layer_call(x_prompt, mod, weights, None, is_ctx=True, n_seq=2,
                                               layer0=0, n_layers=depth)
    b, t = x_prompt.shape[:2]
    new_gqa_k = gk.reshape(b, depth, GQA_KV_HEADS, HEAD_DIM, t).transpose(0, 1, 4, 2, 3)
    new_gqa_v = gv.reshape(b, depth, GQA_KV_HEADS, HEAD_DIM, t).transpose(0, 1, 4, 2, 3)
    new_diff_k = dk.reshape(b, depth, DIFF_HEADS, 2, HEAD_DIM, t).transpose(0, 1, 5, 2, 3, 4)
    new_diff_v = dv.reshape(b, depth, t, DIFF_HEADS, 2 * HEAD_DIM)

    db, _, past = cache_gqa_k.shape[:3]
    cos, sin = _rope_tables(x_sample.shape[1])
    extras = [cache_gqa_k.transpose(0, 1, 3, 4, 2).reshape(db, depth, GQA_KV_WIDTH, past),
              cache_gqa_v.transpose(0, 1, 3, 4, 2).reshape(db, depth, GQA_KV_WIDTH, past),
              cache_diff_k.transpose(0, 1, 3, 4, 5, 2).reshape(db, depth, DIFF_WIDTH, past),
              cache_diff_v.reshape(db, depth, past * DIFF_HEADS, 2 * HEAD_DIM),
              state_lru, cos, sin]
    y_sample = x_sample
    for lyr in range(depth):
        (y_sample,) = _layer_call(y_sample, mod, weights, extras, is_ctx=False, n_seq=1,
                                  layer0=lyr, n_layers=1)

    return (y_prompt, y_sample, new_gqa_k, new_gqa_v, new_diff_k, new_diff_v, st)
```

```python
import functools
import math

import jax
import jax.numpy as jnp
from jax import lax
from jax.experimental import pallas as pl
from jax.experimental.pallas import tpu as pltpu

F32 = jnp.float32
BF16 = jnp.bfloat16

GRID_W = 64
HEAD_DIM = 64
EPS = 1e-6
ROPE_BASE = 10000.0
LRU_WIDTH = 256
LRU_BLOCKS = 4
LRU_C = 8.0
CONV_W = 4
CONV_LEFT = 2
GQA_Q_HEADS = 4
GQA_KV_HEADS = 2
GQA_WIDTH = GQA_Q_HEADS * HEAD_DIM
GQA_KV_WIDTH = GQA_KV_HEADS * HEAD_DIM
DIFF_HEADS = 4
DIFF_WIDTH = DIFF_HEADS * 2 * HEAD_DIM
QK_SCALE = HEAD_DIM ** -0.5 * math.log2(math.e)

C_LRU_X = 0
C_LRU_G = C_LRU_X + LRU_WIDTH
C_GQ = C_LRU_G + LRU_WIDTH
C_GK = C_GQ + GQA_WIDTH
C_GV = C_GK + GQA_KV_WIDTH
C_GG = C_GV + GQA_KV_WIDTH
C_DQ = C_GG + GQA_WIDTH
C_DK = C_DQ + DIFF_WIDTH
C_DV = C_DK + DIFF_WIDTH
C_DG = C_DV + DIFF_WIDTH
N_IN = C_DG + DIFF_WIDTH
M_LRU = 0
M_GQA = LRU_WIDTH
M_DIFF = LRU_WIDTH + GQA_WIDTH
D_MIX = LRU_WIDTH + GQA_WIDTH + DIFF_WIDTH

LANES = 128
SUBLANES = 8
ROW_CHUNK = 256
KEY_CHUNK = 256
MOD_ROWS = 16
VMEM_LIMIT = 56 * 1024 * 1024


def _lambda_init(layer):
    return 0.8 - 0.6 * math.exp(-0.3 * layer)


def _dot(a, b):
    return jnp.dot(a, b, preferred_element_type=F32)


def _silu(x):
    return x * jax.nn.sigmoid(x)


def _rms(x, g):
    ms = jnp.mean(x * x, axis=-1, keepdims=True)
    return x * lax.rsqrt(ms + EPS) * g


def _rms_heads(x, g2):
    lo = lax.broadcasted_iota(jnp.int32, x.shape, 1) < HEAD_DIM
    t = x * x
    s_lo = jnp.sum(jnp.where(lo, t, 0.0), axis=-1, keepdims=True)
    s_hi = jnp.sum(jnp.where(lo, 0.0, t), axis=-1, keepdims=True)
    ms = jnp.where(lo, s_lo, s_hi) * (1.0 / HEAD_DIM)
    return x * lax.rsqrt(ms + EPS) * g2


def _rope(x, cos, sin_signed):
    first = (lax.broadcasted_iota(jnp.int32, x.shape, 1) % 32) < 16
    partner = jnp.where(first, pltpu.roll(x, LANES - 16, 1), pltpu.roll(x, 16, 1))
    return x * cos + partner * sin_signed


def _shift_rows(x, k, fill, n_rows):
    rows = lax.broadcasted_iota(jnp.int32, x.shape, 0)
    if k > 0:
        return jnp.where(rows >= k, pltpu.roll(x, k, 0), fill)
    return jnp.where(rows < n_rows + k, pltpu.roll(x, n_rows + k, 0), fill)


def _linear_scan(a, x, n_rows, reverse):
    nb = n_rows // SUBLANES
    lanes = a.shape[-1]
    a3 = a.reshape(nb, SUBLANES, lanes)
    x3 = x.reshape(nb, SUBLANES, lanes)
    sub = lax.broadcasted_iota(jnp.int32, (nb, SUBLANES, lanes), 1)
    k = 1
    while k < SUBLANES:
        keep = (sub < SUBLANES - k) if reverse else (sub >= k)
        shift = SUBLANES - k if reverse else k
        x3 = a3 * jnp.where(keep, pltpu.roll(x3, shift, 1), 0.0) + x3
        a3 = a3 * jnp.where(keep, pltpu.roll(a3, shift, 1), 1.0)
        k *= 2
    blocks = [None] * nb
    carry = None
    for j in (range(nb - 1, -1, -1) if reverse else range(nb)):
        xb = x3[j]
        if carry is not None:
            xb = xb + a3[j] * carry
        blocks[j] = xb
        carry = xb[0:1] if reverse else xb[SUBLANES - 1:SUBLANES]
    return jnp.concatenate(blocks, axis=0)


def _mod_kernel(cond_ref, w_ref, b_ref, o_ref):
    o_ref[0] = jnp.dot(_silu(cond_ref[...]), w_ref[0], preferred_element_type=F32,
                       precision=lax.Precision.HIGHEST) + b_ref[0]


def _layer_kernel(*refs, is_ctx, n_seq, seq_len, past_len, depth, layer0):
    T = seq_len
    Tk = past_len + T
    n_chunks = T // ROW_CHUNK
    it = iter(refs)
    x_ref, mod_ref, gpre_ref, gpost_ref, win_ref, wout_ref = (next(it) for _ in range(6))
    convw_ref, convb_ref, wg_ref, bg_ref, lam_ref = (next(it) for _ in range(5))
    gqc_ref, gk_ref, dlam_ref, gsub_ref = (next(it) for _ in range(4))
    if not is_ctx:
        (ck_ref, cv_ref, cdk_ref, cdv_ref, st_ref, cos_ref, sin_ref, cost_ref, sint_ref) = (
            next(it) for _ in range(9))
    y_ref = next(it)
    if is_ctx:
        ogk_ref, ogv_ref, odk_ref, odv_ref, ost_ref = (next(it) for _ in range(5))
    (h_scr, mix_scr, zx_scr, qt_scr, k_scr, vt_scr, dqt_scr, dk_scr, dvt_scr, s_scr, e_scr) = (
        next(it) for _ in range(11))

    @pl.when(pl.program_id(1) == 0)
    def _():
        y_ref[...] = x_ref[...]

    layer = pl.program_id(1) + layer0

    lam_i = jnp.float32(_lambda_init(depth - 1))
    for lyr in range(depth - 2, -1, -1):
        lam_i = jnp.where(layer == lyr, jnp.float32(_lambda_init(lyr)), lam_i)

    shift = mod_ref[0, 0, :, 0:1024]
    scale = mod_ref[0, 0, :, 1024:2048]
    gate = mod_ref[0, 0, :, 2048:3072]
    dl = dlam_ref[0]
    lam = (jnp.exp(jnp.sum(dl[0:1] * dl[1:2], axis=-1, keepdims=True))
           - jnp.exp(jnp.sum(dl[2:3] * dl[3:4], axis=-1, keepdims=True)) + lam_i)

    def win(c0, n):
        return win_ref[0, :, c0:c0 + n]

    def rows_of(c):
        return pl.ds(c * ROW_CHUNK, ROW_CHUNK)

    def one_sequence(s):
        def prenorm_and_lru_in(c):
            r = rows_of(c)
            hn = _rms(y_ref[s, r, :], gpre_ref[0]) * (1.0 + scale) + shift
            hc = hn.astype(BF16)
            h_scr[r, :] = hc
            zxg = _dot(hc, win(C_LRU_X, 2 * LRU_WIDTH))
            zx_scr[r, 0:LRU_WIDTH] = zxg[:, 0:LRU_WIDTH]
            zx_scr[r, LRU_WIDTH:] = _silu(zxg[:, LRU_WIDTH:])

        row_id = lax.broadcasted_iota(jnp.int32, (T, LANES), 0)
        lru_state = {}

        def lru_piece(ct, d):
            cl = slice(ct * LANES, (ct + 1) * LANES)
            reverse = d == 1
            if d == 0:
                zx = zx_scr[:, cl]
                u = convb_ref[0, :, cl] + zx * convw_ref[0, CONV_LEFT:CONV_LEFT + 1, cl]
                for j in range(CONV_W):
                    if j != CONV_LEFT:
                        u = u + _shift_rows(zx, CONV_LEFT - j, 0.0, T) * convw_ref[0, j:j + 1, cl]
                lru_state[ct] = (u, u.astype(BF16), None)
            u, u_bf, y_prev = lru_state[ct]
            gates = _dot(u_bf, wg_ref[0, ct, :, d * 2 * LANES:(d + 1) * 2 * LANES])
            gates = gates + bg_ref[0, ct, :, d * 2 * LANES:(d + 1) * 2 * LANES]
            r_g = jax.nn.sigmoid(gates[:, 0:LANES])
            i_g = jax.nn.sigmoid(gates[:, LANES:])
            lam_d = lam_ref[0, d:d + 1, cl]
            sp = jnp.maximum(-lam_d, 0.0) + jnp.log1p(jnp.exp(-jnp.abs(lam_d)))
            a = jnp.exp(-LRU_C * r_g * sp)
            inp = jnp.sqrt(1.0 - a * a) * (i_g * u)
            if not is_ctx:
                h0 = st_ref[0, 0, d:d + 1, cl]
                first_row = (T - 1) if reverse else 0
                inp = inp + jnp.where(row_id == first_row, a * h0, 0.0)
            y_d = _linear_scan(a, inp, T, reverse)
            if is_ctx:
                last_row = 0 if reverse else T - 1
                ost_ref[s, 0, d:d + 1, cl] = y_d[last_row:last_row + 1, :]
            if d == 0:
                lru_state[ct] = (u, u_bf, y_d)
            else:
                lru_gate = zx_scr[:, LRU_WIDTH + ct * LANES:LRU_WIDTH + (ct + 1) * LANES]
                mix_scr[:, M_LRU + ct * LANES:M_LRU + (ct + 1) * LANES] = (
                    (y_prev + y_d) * lru_gate).astype(BF16)

        def cached_kv(c):
            r = rows_of(c)
            k_scr[r, :] = ck_ref[0, 0, :, r].T.astype(BF16)
            vt_scr[:, r] = cv_ref[0, 0, :, r].astype(BF16)
            for hd in range(DIFF_HEADS):
                cl = slice(hd * LANES, (hd + 1) * LANES)
                dk_scr[r, cl] = cdk_ref[0, 0, cl, r].T.astype(BF16)
                rows_h = pl.ds(c * ROW_CHUNK * DIFF_HEADS + hd, ROW_CHUNK, stride=DIFF_HEADS)
                dvt_scr[cl, r] = cdv_ref[0, 0, rows_h, :].T.astype(BF16)

        def new_kv(c):
            r = rows_of(c)
            rk = pl.ds(past_len + c * ROW_CHUNK, ROW_CHUNK)
            hc = h_scr[r, :]
            zkv = _dot(hc, win(C_GK, 2 * GQA_KV_WIDTH))
            k = _rms_heads(zkv[:, 0:GQA_KV_WIDTH], gk_ref[0])
            v_t = zkv[:, GQA_KV_WIDTH:].T
            zdk = _dot(hc, win(C_DK, DIFF_WIDTH))
            zdv = _dot(hc, win(C_DV, DIFF_WIDTH))
            if is_ctx:
                ogk_ref[s, 0, :, r] = k.T
                ogv_ref[s, 0, :, r] = v_t
            else:
                k = _rope(k, cos_ref[r, :], sin_ref[r, :])
            k_scr[rk, :] = k.astype(BF16)
            vt_scr[:, rk] = v_t.astype(BF16)
            for hd in range(DIFF_HEADS):
                cl = slice(hd * LANES, (hd + 1) * LANES)
                dk = zdk[:, cl]
                if is_ctx:
                    odk_ref[s, 0, cl, r] = dk.T
                    rows_h = pl.ds(c * ROW_CHUNK * DIFF_HEADS + hd, ROW_CHUNK, stride=DIFF_HEADS)
                    odv_ref[s, 0, rows_h, :] = zdv[:, cl]
                else:
                    dk = _rope(dk, cos_ref[r, :], sin_ref[r, :])
                dk_scr[rk, cl] = dk.astype(BF16)
                dvt_scr[cl, rk] = zdv[:, cl].T.astype(BF16)

        def new_q_piece(cq, g):
            start = cq * ROW_CHUNK
            rq = pl.ds(start if isinstance(cq, int) else pl.multiple_of(start, ROW_CHUNK), ROW_CHUNK)
            c0 = C_GQ if g == 0 else C_DQ + (g - 1) * 2 * LANES
            z = _dot(h_scr[rq, :], win(c0, 2 * LANES))
            for j in range(2):
                t = z[:, j * LANES:(j + 1) * LANES].T
                if g == 0:
                    gain = jnp.concatenate([gqc_ref[0]] * (ROW_CHUNK // LANES), axis=1)
                    halves = []
                    for hh in range(LANES // HEAD_DIM):
                        th = t[hh * HEAD_DIM:(hh + 1) * HEAD_DIM]
                        ms = jnp.sum(th * th, axis=0, keepdims=True) * (1.0 / HEAD_DIM)
                        halves.append(th * lax.rsqrt(ms + EPS) * gain)
                    t = jnp.concatenate(halves, axis=0)
                if not is_ctx:
                    half = HEAD_DIM // 4
                    blocks = [t[i * half:(i + 1) * half] for i in range(LANES // half)]
                    partner = jnp.concatenate(
                        [blocks[i + 1 - 2 * (i % 2)] for i in range(len(blocks))], axis=0)
                    t = t * cost_ref[cq] + partner * sint_ref[cq]
                t = (t * QK_SCALE).astype(BF16)
                if g == 0:
                    qt_scr[cq, j * LANES:(j + 1) * LANES, :] = t
                else:
                    hd = (g - 1) * 2 + j
                    dqt_scr[cq, hd * LANES:(hd + 1) * LANES, :] = t

        n_q_pieces = 1 + DIFF_HEADS // 2
        q_in_attention = n_chunks > 1

        for c in range(n_chunks):
            prenorm_and_lru_in(c)
        lru_pieces = [(ct, d) for ct in range(LRU_WIDTH // LANES) for d in range(2)]
        n_cached = past_len // ROW_CHUNK
        for i in range(max(len(lru_pieces), n_chunks, n_cached)):
            if i < n_chunks:
                new_kv(i)
            if i < len(lru_pieces):
                lru_piece(*lru_pieces[i])
            if i < n_cached:
                cached_kv(i)
        for g in range(n_q_pieces):
            new_q_piece(0, g)

        zero_half = jnp.zeros((HEAD_DIM, ROW_CHUNK), BF16)

        group = GQA_Q_HEADS // GQA_KV_HEADS
        n_pairs = GQA_KV_HEADS + DIFF_HEADS
        n_kc = Tk // KEY_CHUNK

        def attn_chunk(c, carry):
            r = pl.ds(pl.multiple_of(c * ROW_CHUNK, ROW_CHUNK), ROW_CHUNK)
            hc = h_scr[r, :]

            def pair_qt(p):
                if p < GQA_KV_HEADS:
                    heads = [qt_scr[c, (p * group + i) * HEAD_DIM:(p * group + i + 1) * HEAD_DIM, :]
                             for i in range(group)]
                    cols = [jnp.concatenate([q, zero_half] if p == 0 else [zero_half, q], axis=0)
                            for q in heads]
                else:
                    hd = p - GQA_KV_HEADS
                    q1 = dqt_scr[c, hd * LANES:hd * LANES + HEAD_DIM, :]
                    q2 = dqt_scr[c, hd * LANES + HEAD_DIM:(hd + 1) * LANES, :]
                    cols = [jnp.concatenate([q1, zero_half], axis=0),
                            jnp.concatenate([zero_half, q2], axis=0)]
                return jnp.concatenate(cols, axis=1)

            def pair_keys(p, i):
                rk = slice(i * KEY_CHUNK, (i + 1) * KEY_CHUNK)
                if p < GQA_KV_HEADS:
                    return k_scr[rk, :]
                hd = p - GQA_KV_HEADS
                return dk_scr[rk, hd * LANES:(hd + 1) * LANES]

            def pair_values_t(p):
                if p < GQA_KV_HEADS:
                    return vt_scr[...]
                hd = p - GQA_KV_HEADS
                return dvt_scr[hd * LANES:(hd + 1) * LANES, :]

            def scores(p, qt, i, m):
                sc = _dot(pair_keys(p, i), qt)
                s_scr[p % 2, i * KEY_CHUNK:(i + 1) * KEY_CHUNK, :] = sc
                mi = jnp.max(sc, axis=0, keepdims=True)
                return mi if m is None else jnp.maximum(m, mi)

            def finish(p, o_t, gates):
                if p < GQA_KV_HEADS:
                    o = jnp.concatenate(
                        [o_t[p * HEAD_DIM:(p + 1) * HEAD_DIM, i * ROW_CHUNK:(i + 1) * ROW_CHUNK]
                         for i in range(group)], axis=0).T
                    col = M_GQA + p * LANES
                    gt = gates[0][:, p * LANES:(p + 1) * LANES]
                else:
                    hd = p - GQA_KV_HEADS
                    o = _rms((o_t[:, 0:ROW_CHUNK] - lam * o_t[:, ROW_CHUNK:]).T, gsub_ref[0]) * (1.0 - lam_i)
                    col = M_DIFF + hd * LANES
                    gt = gates[1][:, hd * LANES:(hd + 1) * LANES]
                mix_scr[r, col:col + LANES] = (o * gt).astype(BF16)

            qt_next = pair_qt(0)
            m_next = None
            for i in range(n_kc):
                m_next = scores(0, qt_next, i, m_next)
            gates = (_silu(_dot(hc, win(C_GG, GQA_WIDTH))), _silu(_dot(hc, win(C_DG, DIFF_WIDTH))))
            for p in range(n_pairs):
                m = m_next
                m_next = None
                if p + 1 < n_pairs:
                    qt_next = pair_qt(p + 1)
                l = None
                for i in range(n_kc):
                    if p + 1 < n_pairs:
                        m_next = scores(p + 1, qt_next, i, m_next)
                    rk = slice(i * KEY_CHUNK, (i + 1) * KEY_CHUNK)
                    e = jnp.exp2(s_scr[p % 2, rk, :] - m)
                    li = jnp.sum(e, axis=0, keepdims=True)
                    l = li if l is None else l + li
                    e_scr[p % 2, rk, :] = e.astype(BF16)
                if q_in_attention and p % 2 == 0 and p // 2 < n_q_pieces:
                    new_q_piece(jnp.where(c + 1 == n_chunks, 0, c + 1), p // 2)
                finish(p, _dot(pair_values_t(p), e_scr[p % 2]) / l, gates)
            return carry

        lax.fori_loop(0, n_chunks, attn_chunk, 0)

        for c in range(n_chunks):
            r = rows_of(c)
            out = _dot(mix_scr[r, :], wout_ref[0])
            y_ref[s, r, :] = y_ref[s, r, :] + gate * _rms(out, gpost_ref[0])

    if n_seq == 1:
        one_sequence(0)
    else:
        def seq_body(s, carry):
            one_sequence(s)
            return carry
        lax.fori_loop(0, n_seq, seq_body, 0)


def _rope_tables(seq_len):
    quarter = HEAD_DIM // 4
    inv = jnp.power(ROPE_BASE, -jnp.arange(quarter, dtype=F32) / quarter)
    t = jnp.arange(seq_len)
    row = (t // GRID_W).astype(F32)[:, None] * inv[None]
    col = (t % GRID_W).astype(F32)[:, None] * inv[None]
    cos = jnp.concatenate([jnp.cos(row), jnp.cos(row), jnp.cos(col), jnp.cos(col)], axis=-1)
    sin = jnp.concatenate([-jnp.sin(row), jnp.sin(row), -jnp.sin(col), jnp.sin(col)], axis=-1)
    reps = LANES // HEAD_DIM
    return jnp.tile(cos, (1, reps)), jnp.tile(sin, (1, reps))


def _block_diag(w):
    nb, bw, _ = w.shape
    eye = jnp.eye(nb, dtype=w.dtype)
    return (eye[:, None, :, None] * w[:, :, None, :]).reshape(nb * bw, nb * bw)


def _layer_call(x, mod, weights, extras, *, is_ctx, n_seq, layer0, n_layers):
    batch, seq_len, d_model = x.shape
    depth = weights[2].shape[0]
    past_len = 0 if is_ctx else extras[0].shape[3]
    tk = past_len + seq_len
    n_chunks = seq_len // ROW_CHUNK
    grid = (batch // n_seq, n_layers)
    single = dict(pipeline_mode=pl.Buffered(1))
    w_mode = single if n_layers == 1 else {}

    def per_layer(a):
        nd = a.ndim
        return pl.BlockSpec((1,) + a.shape[1:], lambda b, l: (l + layer0,) + (0,) * (nd - 1), **w_mode)

    y_spec = pl.BlockSpec((n_seq, seq_len, d_model), lambda b, l: (b, 0, 0))
    if is_ctx:
        x_spec = y_spec
        mod_spec = pl.BlockSpec((1, 1, 1, mod.shape[-1]), lambda b, l: (l + layer0, 0, 0, 0))
    else:
        x_spec = pl.BlockSpec((n_seq, seq_len, d_model), lambda b, l: (b, 0, 0), **single)
        mod_spec = pl.BlockSpec((1, 1, 1, mod.shape[-1]), lambda b, l: (l + layer0, b + 1, 0, 0))
    in_specs = [x_spec, mod_spec] + [per_layer(w) for w in weights]
    args = [x, mod] + list(weights)
    if not is_ctx:
        for a in extras[:5]:
            in_specs.append(
                pl.BlockSpec((1, 1) + a.shape[2:], lambda b, l: (b, l + layer0, 0, 0), **single))
        for a in extras[5:]:
            in_specs.append(pl.BlockSpec(a.shape, lambda b, l, nd=a.ndim: (0,) * nd, **single))
        args += list(extras)

    out_shape = [jax.ShapeDtypeStruct(x.shape, F32)]
    out_specs = [y_spec]
    if is_ctx:
        for rows_, cols_ in ((GQA_KV_WIDTH, seq_len), (GQA_KV_WIDTH, seq_len), (DIFF_WIDTH, seq_len),
                             (seq_len * DIFF_HEADS, DIFF_WIDTH // DIFF_HEADS)):
            out_shape.append(jax.ShapeDtypeStruct((batch, depth, rows_, cols_), F32))
            out_specs.append(pl.BlockSpec((n_seq, 1, rows_, cols_), lambda b, l: (b, l, 0, 0)))
        out_shape.append(jax.ShapeDtypeStruct((batch, depth, 2, LRU_WIDTH), F32))
        out_specs.append(pl.BlockSpec((n_seq, 1, 2, LRU_WIDTH), lambda b, l: (b, l, 0, 0)))

    scratch = [
        pltpu.VMEM((seq_len, d_model), BF16),
        pltpu.VMEM((seq_len, D_MIX), BF16),
        pltpu.VMEM((seq_len, 2 * LRU_WIDTH), F32),
        pltpu.VMEM((n_chunks, GQA_WIDTH, ROW_CHUNK), BF16),
        pltpu.VMEM((tk, GQA_KV_WIDTH), BF16),
        pltpu.VMEM((GQA_KV_WIDTH, tk), BF16),
        pltpu.VMEM((n_chunks, DIFF_WIDTH, ROW_CHUNK), BF16),
        pltpu.VMEM((tk, DIFF_WIDTH), BF16),
        pltpu.VMEM((DIFF_WIDTH, tk), BF16),
        pltpu.VMEM((2, tk, 2 * ROW_CHUNK), F32),
        pltpu.VMEM((2, tk, 2 * ROW_CHUNK), BF16),
    ]
    kern = functools.partial(_layer_kernel, is_ctx=is_ctx, n_seq=n_seq, seq_len=seq_len,
                             past_len=past_len, depth=depth, layer0=layer0)
    return pl.pallas_call(
        kern,
        grid=grid,
        in_specs=in_specs,
        out_specs=out_specs,
        out_shape=out_shape,
        scratch_shapes=scratch,
        compiler_params=pltpu.CompilerParams(
            dimension_semantics=("arbitrary", "arbitrary"),
            vmem_limit_bytes=VMEM_LIMIT),
        name="ctx_pass" if is_ctx else f"denoise_layer{layer0}",
    )(*args)


def kernel(x_prompt, x_sample, cache_gqa_k, cache_gqa_v, cache_diff_k, cache_diff_v, state_lru, c, c_ctx, w_mod, b_mod, g_pre, g_post, w_in, w_out, lru_conv_w, lru_conv_b, lru_wa, lru_ba, lru_wx, lru_bx, lru_lambda, gqa_gq, gqa_gk, diff_lam, diff_gsub):
    depth, d_model, _ = w_in.shape
    dec_batch = x_sample.shape[0]

    cond = jnp.concatenate(
        [c_ctx[None, :], c, jnp.zeros((MOD_ROWS - 1 - dec_batch, d_model), F32)], axis=0)
    n_mod = w_mod.shape[-1]
    mod_tile = n_mod // 3
    mod = pl.pallas_call(
        _mod_kernel,
        grid=(depth, n_mod // mod_tile),
        in_specs=[pl.BlockSpec((MOD_ROWS, d_model), lambda l, j: (0, 0)),
                  pl.BlockSpec((1, d_model, mod_tile), lambda l, j: (l, 0, j)),
                  pl.BlockSpec((1, 1, mod_tile), lambda l, j: (l, 0, j))],
        out_specs=pl.BlockSpec((1, MOD_ROWS, mod_tile), lambda l, j: (l, 0, j)),
        out_shape=jax.ShapeDtypeStruct((depth, MOD_ROWS, n_mod), F32),
        name="adaln_mod",
    )(cond, w_mod, b_mod[:, None, :])
    mod = mod[:, :, None, :]

    w_in_p = w_in.astype(BF16)
    w_out_p = w_out.astype(BF16)
    bd = jax.vmap(jax.vmap(_block_diag))
    wa_d, wx_d = bd(lru_wa), bd(lru_wx)
    n_ct = LRU_WIDTH // LANES
    wg = jnp.stack([
        jnp.concatenate([m[:, d, ct * LANES:(ct + 1) * LANES, ct * LANES:(ct + 1) * LANES]
                         for d in range(2) for m in (wa_d, wx_d)], axis=-1)
        for ct in range(n_ct)], axis=1).astype(BF16)
    bg = jnp.stack([
        jnp.concatenate([m[:, d, ct * LANES:(ct + 1) * LANES]
                         for d in range(2) for m in (lru_ba, lru_bx)], axis=-1)
        for ct in range(n_ct)], axis=1)[:, :, None, :]
    reps = LANES // HEAD_DIM
    weights = [g_pre[:, None, :], g_post[:, None, :], w_in_p, w_out_p, lru_conv_w,
               lru_conv_b[:, None, :], wg, bg, lru_lambda,
               jnp.broadcast_to(gqa_gq[:, :, None], (depth, HEAD_DIM, LANES)),
               jnp.tile(gqa_gk, (1, reps))[:, None, :],
               diff_lam, diff_gsub[:, None, :]]

    y_prompt, gk, gv, dk, dv, st = _layer_call(x_prompt, mod, weights, None, is_ctx=True, n_seq=2,
                                               layer0=0, n_layers=depth)
    b, t = x_prompt.shape[:2]
    new_gqa_k = gk.reshape(b, depth, GQA_KV_HEADS, HEAD_DIM, t).transpose(0, 1, 4, 2, 3)
    new_gqa_v = gv.reshape(b, depth, GQA_KV_HEADS, HEAD_DIM, t).transpose(0, 1, 4, 2, 3)
    new_diff_k = dk.reshape(b, depth, DIFF_HEADS, 2, HEAD_DIM, t).transpose(0, 1, 5, 2, 3, 4)
    new_diff_v = dv.reshape(b, depth, t, DIFF_HEADS, 2 * HEAD_DIM)

    db, _, past = cache_gqa_k.shape[:3]
    cos, sin = _rope_tables(x_sample.shape[1])

    def feature_major_chunks(tab):
        return tab.T.reshape(LANES, -1, ROW_CHUNK).transpose(1, 0, 2)

    extras = [cache_gqa_k.transpose(0, 1, 3, 4, 2).reshape(db, depth, GQA_KV_WIDTH, past),
              cache_gqa_v.transpose(0, 1, 3, 4, 2).reshape(db, depth, GQA_KV_WIDTH, past),
              cache_diff_k.transpose(0, 1, 3, 4, 5, 2).reshape(db, depth, DIFF_WIDTH, past),
              cache_diff_v.reshape(db, depth, past * DIFF_HEADS, 2 * HEAD_DIM),
              state_lru, cos, sin, feature_major_chunks(cos), feature_major_chunks(sin)]
    y_sample = x_sample
    for lyr in range(depth):
        (y_sample,) = _layer_call(y_sample, mod, weights, extras, is_ctx=False, n_seq=1,
                                  layer0=lyr, n_layers=1)

    return (y_prompt, y_sample, new_gqa_k, new_gqa_v, new_diff_k, new_diff_v, st)
```

```python
import functools
import math

import jax
import jax.numpy as jnp
from jax import lax
from jax.experimental import pallas as pl
from jax.experimental.pallas import tpu as pltpu

F32 = jnp.float32
BF16 = jnp.bfloat16

GRID_W = 64
HEAD_DIM = 64
EPS = 1e-6
ROPE_BASE = 10000.0
LRU_WIDTH = 256
LRU_BLOCKS = 4
LRU_C = 8.0
CONV_W = 4
CONV_LEFT = 2
GQA_Q_HEADS = 4
GQA_KV_HEADS = 2
GQA_WIDTH = GQA_Q_HEADS * HEAD_DIM
GQA_KV_WIDTH = GQA_KV_HEADS * HEAD_DIM
DIFF_HEADS = 4
DIFF_WIDTH = DIFF_HEADS * 2 * HEAD_DIM
QK_SCALE = HEAD_DIM ** -0.5 * math.log2(math.e)

C_LRU_X = 0
C_LRU_G = C_LRU_X + LRU_WIDTH
C_GQ = C_LRU_G + LRU_WIDTH
C_GK = C_GQ + GQA_WIDTH
C_GV = C_GK + GQA_KV_WIDTH
C_GG = C_GV + GQA_KV_WIDTH
C_DQ = C_GG + GQA_WIDTH
C_DK = C_DQ + DIFF_WIDTH
C_DV = C_DK + DIFF_WIDTH
C_DG = C_DV + DIFF_WIDTH
N_IN = C_DG + DIFF_WIDTH
M_LRU = 0
M_GQA = LRU_WIDTH
M_DIFF = LRU_WIDTH + GQA_WIDTH
D_MIX = LRU_WIDTH + GQA_WIDTH + DIFF_WIDTH

LANES = 128
SUBLANES = 8
ROW_CHUNK = 256
KEY_CHUNK = 256
MOD_ROWS = 16
VMEM_LIMIT = 56 * 1024 * 1024


def _lambda_init(layer):
    return 0.8 - 0.6 * math.exp(-0.3 * layer)


def _dot(a, b):
    return jnp.dot(a, b, preferred_element_type=F32)


def _silu(x):
    return x * jax.nn.sigmoid(x)


def _rms(x, g):
    ms = jnp.mean(x * x, axis=-1, keepdims=True)
    return x * lax.rsqrt(ms + EPS) * g


def _rms_heads(x, g2):
    lo = lax.broadcasted_iota(jnp.int32, x.shape, 1) < HEAD_DIM
    t = x * x
    s_lo = jnp.sum(jnp.where(lo, t, 0.0), axis=-1, keepdims=True)
    s_hi = jnp.sum(jnp.where(lo, 0.0, t), axis=-1, keepdims=True)
    ms = jnp.where(lo, s_lo, s_hi) * (1.0 / HEAD_DIM)
    return x * lax.rsqrt(ms + EPS) * g2


def _rope(x, cos, sin_signed):
    first = (lax.broadcasted_iota(jnp.int32, x.shape, 1) % 32) < 16
    partner = jnp.where(first, pltpu.roll(x, LANES - 16, 1), pltpu.roll(x, 16, 1))
    return x * cos + partner * sin_signed


def _shift_rows(x, k, fill, n_rows):
    rows = lax.broadcasted_iota(jnp.int32, x.shape, 0)
    if k > 0:
        return jnp.where(rows >= k, pltpu.roll(x, k, 0), fill)
    return jnp.where(rows < n_rows + k, pltpu.roll(x, n_rows + k, 0), fill)


def _linear_scan(a, x, n_rows, reverse):
    nb = n_rows // SUBLANES
    lanes = a.shape[-1]
    a3 = a.reshape(nb, SUBLANES, lanes)
    x3 = x.reshape(nb, SUBLANES, lanes)
    sub = lax.broadcasted_iota(jnp.int32, (nb, SUBLANES, lanes), 1)
    k = 1
    while k < SUBLANES:
        keep = (sub < SUBLANES - k) if reverse else (sub >= k)
        shift = SUBLANES - k if reverse else k
        x3 = a3 * jnp.where(keep, pltpu.roll(x3, shift, 1), 0.0) + x3
        a3 = a3 * jnp.where(keep, pltpu.roll(a3, shift, 1), 1.0)
        k *= 2
    blocks = [None] * nb
    carry = None
    for j in (range(nb - 1, -1, -1) if reverse else range(nb)):
        xb = x3[j]
        if carry is not None:
            xb = xb + a3[j] * carry
        blocks[j] = xb
        carry = xb[0:1] if reverse else xb[SUBLANES - 1:SUBLANES]
    return jnp.concatenate(blocks, axis=0)


_PHASE_END = "phase-end"


def _mod_kernel(cond_ref, w_ref, b_ref, o_ref):
    o_ref[0] = _dot(_silu(cond_ref[...]).astype(BF16), w_ref[0].astype(BF16)) + b_ref[0]


def _layer_kernel(*refs, is_ctx, n_seq, seq_len, past_len, depth, layer0):
    T = seq_len
    Tk = past_len + T
    n_chunks = T // ROW_CHUNK
    it = iter(refs)
    x_ref, mod_ref, gpre_ref, gpost_ref, win_ref, wout_ref = (next(it) for _ in range(6))
    convw_ref, convb_ref, wg_ref, bg_ref, lam_ref = (next(it) for _ in range(5))
    gqc_ref, gk_ref, dlam_ref, gsub_ref = (next(it) for _ in range(4))
    if not is_ctx:
        (ck_ref, cv_ref, cdk_ref, cdv_ref, st_ref, cos_ref, sin_ref, cost_ref, sint_ref) = (
            next(it) for _ in range(9))
    y_ref = next(it)
    if is_ctx:
        ogk_ref, ogv_ref, odk_ref, odv_ref, ost_ref = (next(it) for _ in range(5))
    scratch_refs = [next(it) for _ in range(11)]

    @pl.when(pl.program_id(1) == 0)
    def _():
        y_ref[...] = x_ref[...]

    layer = pl.program_id(1) + layer0

    lam_i = jnp.float32(_lambda_init(depth - 1))
    for lyr in range(depth - 2, -1, -1):
        lam_i = jnp.where(layer == lyr, jnp.float32(_lambda_init(lyr)), lam_i)

    shift = mod_ref[0, 0, :, 0:1024]
    scale = mod_ref[0, 0, :, 1024:2048]
    gate = mod_ref[0, 0, :, 2048:3072]
    dl = dlam_ref[0]
    lam = (jnp.exp(jnp.sum(dl[0:1] * dl[1:2], axis=-1, keepdims=True))
           - jnp.exp(jnp.sum(dl[2:3] * dl[3:4], axis=-1, keepdims=True)) + lam_i)

    def win(c0, n):
        return win_ref[0, :, c0:c0 + n]

    def rows_of(c):
        return pl.ds(c * ROW_CHUNK, ROW_CHUNK)

    def sequence_steps(s):
        (h_scr, mix_scr, zx_scr, qt_scr, k_scr, vt_scr, dqt_scr, dk_scr, dvt_scr, s_scr, e_scr) = (
            ref.at[s] for ref in scratch_refs)

        def prenorm_and_lru_in(c):
            r = rows_of(c)
            hn = _rms(y_ref[s, r, :], gpre_ref[0]) * (1.0 + scale) + shift
            hc = hn.astype(BF16)
            h_scr[r, :] = hc
            zxg = _dot(hc, win(C_LRU_X, 2 * LRU_WIDTH))
            zx_scr[r, 0:LRU_WIDTH] = zxg[:, 0:LRU_WIDTH]
            zx_scr[r, LRU_WIDTH:] = _silu(zxg[:, LRU_WIDTH:])

        row_id = lax.broadcasted_iota(jnp.int32, (T, LANES), 0)
        lru_state = {}

        def lru_piece(ct, d):
            cl = slice(ct * LANES, (ct + 1) * LANES)
            reverse = d == 1
            if d == 0:
                zx = zx_scr[:, cl]
                u = convb_ref[0, :, cl] + zx * convw_ref[0, CONV_LEFT:CONV_LEFT + 1, cl]
                for j in range(CONV_W):
                    if j != CONV_LEFT:
                        u = u + _shift_rows(zx, CONV_LEFT - j, 0.0, T) * convw_ref[0, j:j + 1, cl]
                lru_state[ct] = (u, u.astype(BF16), None)
                yield
            u, u_bf, y_prev = lru_state[ct]
            gates = _dot(u_bf, wg_ref[0, ct, :, d * 2 * LANES:(d + 1) * 2 * LANES])
            gates = gates + bg_ref[0, ct, :, d * 2 * LANES:(d + 1) * 2 * LANES]
            r_g = jax.nn.sigmoid(gates[:, 0:LANES])
            i_g = jax.nn.sigmoid(gates[:, LANES:])
            yield
            lam_d = lam_ref[0, d:d + 1, cl]
            sp = jnp.maximum(-lam_d, 0.0) + jnp.log1p(jnp.exp(-jnp.abs(lam_d)))
            a = jnp.exp(-LRU_C * r_g * sp)
            inp = jnp.sqrt(1.0 - a * a) * (i_g * u)
            if not is_ctx:
                h0 = st_ref[0, 0, d:d + 1, cl]
                first_row = (T - 1) if reverse else 0
                inp = inp + jnp.where(row_id == first_row, a * h0, 0.0)
            yield
            y_d = _linear_scan(a, inp, T, reverse)
            yield
            if is_ctx:
                last_row = 0 if reverse else T - 1
                ost_ref[s, 0, d:d + 1, cl] = y_d[last_row:last_row + 1, :]
            if d == 0:
                lru_state[ct] = (u, u_bf, y_d)
            else:
                lru_gate = zx_scr[:, LRU_WIDTH + ct * LANES:LRU_WIDTH + (ct + 1) * LANES]
                mix_scr[:, M_LRU + ct * LANES:M_LRU + (ct + 1) * LANES] = (
                    (y_prev + y_d) * lru_gate).astype(BF16)
            yield

        def cached_kv(c):
            r = rows_of(c)
            k_scr[r, :] = ck_ref[0, 0, :, r].T.astype(BF16)
            vt_scr[:, r] = cv_ref[0, 0, :, r].astype(BF16)
            for hd in range(DIFF_HEADS):
                cl = slice(hd * LANES, (hd + 1) * LANES)
                dk_scr[r, cl] = cdk_ref[0, 0, cl, r].T.astype(BF16)
                rows_h = pl.ds(c * ROW_CHUNK * DIFF_HEADS + hd, ROW_CHUNK, stride=DIFF_HEADS)
                dvt_scr[cl, r] = cdv_ref[0, 0, rows_h, :].T.astype(BF16)

        def new_kv(c):
            r = rows_of(c)
            rk = pl.ds(past_len + c * ROW_CHUNK, ROW_CHUNK)
            hc = h_scr[r, :]
            zkv = _dot(hc, win(C_GK, 2 * GQA_KV_WIDTH))
            k = _rms_heads(zkv[:, 0:GQA_KV_WIDTH], gk_ref[0])
            v_t = zkv[:, GQA_KV_WIDTH:].T
            if is_ctx:
                ogk_ref[s, 0, :, r] = k.T
                ogv_ref[s, 0, :, r] = v_t
            else:
                k = _rope(k, cos_ref[r, :], sin_ref[r, :])
            k_scr[rk, :] = k.astype(BF16)
            vt_scr[:, rk] = v_t.astype(BF16)
            yield
            zdk = _dot(hc, win(C_DK, DIFF_WIDTH))
            yield
            zdv = _dot(hc, win(C_DV, DIFF_WIDTH))
            yield
            for hd in range(DIFF_HEADS):
                cl = slice(hd * LANES, (hd + 1) * LANES)
                dk = zdk[:, cl]
                if is_ctx:
                    odk_ref[s, 0, cl, r] = dk.T
                    rows_h = pl.ds(c * ROW_CHUNK * DIFF_HEADS + hd, ROW_CHUNK, stride=DIFF_HEADS)
                    odv_ref[s, 0, rows_h, :] = zdv[:, cl]
                else:
                    dk = _rope(dk, cos_ref[r, :], sin_ref[r, :])
                dk_scr[rk, cl] = dk.astype(BF16)
                dvt_scr[cl, rk] = zdv[:, cl].T.astype(BF16)
                yield

        def new_q_piece(cq, g):
            start = cq * ROW_CHUNK
            rq = pl.ds(start if isinstance(cq, int) else pl.multiple_of(start, ROW_CHUNK), ROW_CHUNK)
            c0 = C_GQ if g == 0 else C_DQ + (g - 1) * 2 * LANES
            z = _dot(h_scr[rq, :], win(c0, 2 * LANES))
            for j in range(2):
                t = z[:, j * LANES:(j + 1) * LANES].T
                if g == 0:
                    gain = jnp.concatenate([gqc_ref[0]] * (ROW_CHUNK // LANES), axis=1)
                    halves = []
                    for hh in range(LANES // HEAD_DIM):
                        th = t[hh * HEAD_DIM:(hh + 1) * HEAD_DIM]
                        ms = jnp.sum(th * th, axis=0, keepdims=True) * (1.0 / HEAD_DIM)
                        halves.append(th * lax.rsqrt(ms + EPS) * gain)
                    t = jnp.concatenate(halves, axis=0)
                if not is_ctx:
                    half = HEAD_DIM // 4
                    blocks = [t[i * half:(i + 1) * half] for i in range(LANES // half)]
                    partner = jnp.concatenate(
                        [blocks[i + 1 - 2 * (i % 2)] for i in range(len(blocks))], axis=0)
                    t = t * cost_ref[cq] + partner * sint_ref[cq]
                t = (t * QK_SCALE).astype(BF16)
                if g == 0:
                    qt_scr[cq, j * LANES:(j + 1) * LANES, :] = t
                else:
                    hd = (g - 1) * 2 + j
                    dqt_scr[cq, hd * LANES:(hd + 1) * LANES, :] = t

        n_q_pieces = 1 + DIFF_HEADS // 2
        q_in_attention = n_chunks > 1

        for c in range(n_chunks):
            prenorm_and_lru_in(c)
            yield
        lru_pieces = [(ct, d) for ct in range(LRU_WIDTH // LANES) for d in range(2)]
        n_cached = past_len // ROW_CHUNK
        for i in range(max(len(lru_pieces), n_chunks, n_cached)):
            if i < n_chunks:
                yield from new_kv(i)
            if i < len(lru_pieces):
                yield from lru_piece(*lru_pieces[i])
            if i < n_cached:
                cached_kv(i)
                yield
        for g in range(n_q_pieces):
            new_q_piece(0, g)
            yield
        yield _PHASE_END

        zero_half = jnp.zeros((HEAD_DIM, ROW_CHUNK), BF16)

        group = GQA_Q_HEADS // GQA_KV_HEADS
        n_pairs = GQA_KV_HEADS + DIFF_HEADS
        n_kc = Tk // KEY_CHUNK

        def attn_steps(c):
            start = c * ROW_CHUNK
            r = pl.ds(start if isinstance(c, int) else pl.multiple_of(start, ROW_CHUNK), ROW_CHUNK)
            hc = h_scr[r, :]

            def pair_qt(p):
                if p < GQA_KV_HEADS:
                    heads = [qt_scr[c, (p * group + i) * HEAD_DIM:(p * group + i + 1) * HEAD_DIM, :]
                             for i in range(group)]
                    cols = [jnp.concatenate([q, zero_half] if p == 0 else [zero_half, q], axis=0)
                            for q in heads]
                else:
                    hd = p - GQA_KV_HEADS
                    q1 = dqt_scr[c, hd * LANES:hd * LANES + HEAD_DIM, :]
                    q2 = dqt_scr[c, hd * LANES + HEAD_DIM:(hd + 1) * LANES, :]
                    cols = [jnp.concatenate([q1, zero_half], axis=0),
                            jnp.concatenate([zero_half, q2], axis=0)]
                return jnp.concatenate(cols, axis=1)

            def pair_keys(p, i):
                rk = slice(i * KEY_CHUNK, (i + 1) * KEY_CHUNK)
                if p < GQA_KV_HEADS:
                    return k_scr[rk, :]
                hd = p - GQA_KV_HEADS
                return dk_scr[rk, hd * LANES:(hd + 1) * LANES]

            def pair_values_t(p):
                if p < GQA_KV_HEADS:
                    return vt_scr[...]
                hd = p - GQA_KV_HEADS
                return dvt_scr[hd * LANES:(hd + 1) * LANES, :]

            def scores(p, qt, i, m):
                sc = _dot(pair_keys(p, i), qt)
                s_scr[p % 2, i * KEY_CHUNK:(i + 1) * KEY_CHUNK, :] = sc
                mi = jnp.max(sc, axis=0, keepdims=True)
                return mi if m is None else jnp.maximum(m, mi)

            def finish(p, o_t, gates):
                if p < GQA_KV_HEADS:
                    o = jnp.concatenate(
                        [o_t[p * HEAD_DIM:(p + 1) * HEAD_DIM, i * ROW_CHUNK:(i + 1) * ROW_CHUNK]
                         for i in range(group)], axis=0).T
                    col = M_GQA + p * LANES
                    gt = gates[0][:, p * LANES:(p + 1) * LANES]
                else:
                    hd = p - GQA_KV_HEADS
                    o = _rms((o_t[:, 0:ROW_CHUNK] - lam * o_t[:, ROW_CHUNK:]).T, gsub_ref[0]) * (1.0 - lam_i)
                    col = M_DIFF + hd * LANES
                    gt = gates[1][:, hd * LANES:(hd + 1) * LANES]
                mix_scr[r, col:col + LANES] = (o * gt).astype(BF16)

            qt_next = pair_qt(0)
            m_next = None
            for i in range(n_kc):
                m_next = scores(0, qt_next, i, m_next)
            gates = (_silu(_dot(hc, win(C_GG, GQA_WIDTH))), _silu(_dot(hc, win(C_DG, DIFF_WIDTH))))
            yield
            for p in range(n_pairs):
                m = m_next
                m_next = None
                if p + 1 < n_pairs:
                    qt_next = pair_qt(p + 1)
                l = None
                for i in range(n_kc):
                    if p + 1 < n_pairs:
                        m_next = scores(p + 1, qt_next, i, m_next)
                    rk = slice(i * KEY_CHUNK, (i + 1) * KEY_CHUNK)
                    e = jnp.exp2(s_scr[p % 2, rk, :] - m)
                    li = jnp.sum(e, axis=0, keepdims=True)
                    l = li if l is None else l + li
                    e_scr[p % 2, rk, :] = e.astype(BF16)
                    yield
                if q_in_attention and p % 2 == 0 and p // 2 < n_q_pieces:
                    new_q_piece(jnp.where(c + 1 == n_chunks, 0, c + 1), p // 2)
                finish(p, _dot(pair_values_t(p), e_scr[p % 2]) / l, gates)
                yield

        if n_chunks == 1:
            yield from attn_steps(0)
        else:
            def attn_chunk(c, carry):
                for _ in attn_steps(c):
                    pass
                return carry
            lax.fori_loop(0, n_chunks, attn_chunk, 0)
            yield

        for c in range(n_chunks):
            r = rows_of(c)
            halves = []
            for j in range(2):
                half_n = wout_ref.shape[-1] // 2
                halves.append(_dot(mix_scr[r, :], wout_ref[0, :, j * half_n:(j + 1) * half_n]))
                yield
            out = jnp.concatenate(halves, axis=1)
            y_ref[s, r, :] = y_ref[s, r, :] + gate * _rms(out, gpost_ref[0])
            yield

    def phase_done(gen):
        return next(gen, _PHASE_END) is _PHASE_END

    gens = [sequence_steps(s) for s in range(n_seq)]
    while not phase_done(gens[0]):
        pass
    for s in range(n_seq):
        cur_done = False
        nxt_done = s + 1 >= n_seq
        while not (cur_done and nxt_done):
            if not cur_done:
                cur_done = phase_done(gens[s])
            if not nxt_done:
                nxt_done = phase_done(gens[s + 1])


def _rope_tables(seq_len):
    quarter = HEAD_DIM // 4
    inv = jnp.power(ROPE_BASE, -jnp.arange(quarter, dtype=F32) / quarter)
    t = jnp.arange(seq_len)
    row = (t // GRID_W).astype(F32)[:, None] * inv[None]
    col = (t % GRID_W).astype(F32)[:, None] * inv[None]
    cos = jnp.concatenate([jnp.cos(row), jnp.cos(row), jnp.cos(col), jnp.cos(col)], axis=-1)
    sin = jnp.concatenate([-jnp.sin(row), jnp.sin(row), -jnp.sin(col), jnp.sin(col)], axis=-1)
    reps = LANES // HEAD_DIM
    return jnp.tile(cos, (1, reps)), jnp.tile(sin, (1, reps))


def _block_diag(w):
    nb, bw, _ = w.shape
    eye = jnp.eye(nb, dtype=w.dtype)
    return (eye[:, None, :, None] * w[:, :, None, :]).reshape(nb * bw, nb * bw)


def _layer_call(x, mod, weights, extras, *, is_ctx, n_seq, layer0, n_layers):
    batch, seq_len, d_model = x.shape
    depth = weights[2].shape[0]
    past_len = 0 if is_ctx else extras[0].shape[3]
    tk = past_len + seq_len
    n_chunks = seq_len // ROW_CHUNK
    grid = (batch // n_seq, n_layers)
    single = dict(pipeline_mode=pl.Buffered(1))
    w_mode = single if n_layers == 1 else {}

    def per_layer(a):
        nd = a.ndim
        return pl.BlockSpec((1,) + a.shape[1:], lambda b, l: (l + layer0,) + (0,) * (nd - 1), **w_mode)

    y_spec = pl.BlockSpec((n_seq, seq_len, d_model), lambda b, l: (b, 0, 0))
    if is_ctx:
        x_spec = y_spec
        mod_spec = pl.BlockSpec((1, 1, 1, mod.shape[-1]), lambda b, l: (l + layer0, 0, 0, 0))
    else:
        x_spec = pl.BlockSpec((n_seq, seq_len, d_model), lambda b, l: (b, 0, 0), **single)
        mod_spec = pl.BlockSpec((1, 1, 1, mod.shape[-1]), lambda b, l: (l + layer0, b + 1, 0, 0))
    in_specs = [x_spec, mod_spec] + [per_layer(w) for w in weights]
    args = [x, mod] + list(weights)
    if not is_ctx:
        for a in extras[:5]:
            in_specs.append(
                pl.BlockSpec((1, 1) + a.shape[2:], lambda b, l: (b, l + layer0, 0, 0), **single))
        for a in extras[5:]:
            in_specs.append(pl.BlockSpec(a.shape, lambda b, l, nd=a.ndim: (0,) * nd, **single))
        args += list(extras)

    out_shape = [jax.ShapeDtypeStruct(x.shape, F32)]
    out_specs = [y_spec]
    if is_ctx:
        for rows_, cols_ in ((GQA_KV_WIDTH, seq_len), (GQA_KV_WIDTH, seq_len), (DIFF_WIDTH, seq_len),
                             (seq_len * DIFF_HEADS, DIFF_WIDTH // DIFF_HEADS)):
            out_shape.append(jax.ShapeDtypeStruct((batch, depth, rows_, cols_), F32))
            out_specs.append(pl.BlockSpec((n_seq, 1, rows_, cols_), lambda b, l: (b, l, 0, 0)))
        out_shape.append(jax.ShapeDtypeStruct((batch, depth, 2, LRU_WIDTH), F32))
        out_specs.append(pl.BlockSpec((n_seq, 1, 2, LRU_WIDTH), lambda b, l: (b, l, 0, 0)))

    scratch = [pltpu.VMEM((n_seq,) + shape, dtype) for shape, dtype in (
        ((seq_len, d_model), BF16),
        ((seq_len, D_MIX), BF16),
        ((seq_len, 2 * LRU_WIDTH), F32),
        ((n_chunks, GQA_WIDTH, ROW_CHUNK), BF16),
        ((tk, GQA_KV_WIDTH), BF16),
        ((GQA_KV_WIDTH, tk), BF16),
        ((n_chunks, DIFF_WIDTH, ROW_CHUNK), BF16),
        ((tk, DIFF_WIDTH), BF16),
        ((DIFF_WIDTH, tk), BF16),
        ((2, tk, 2 * ROW_CHUNK), F32),
        ((2, tk, 2 * ROW_CHUNK), BF16),
    )]
    kern = functools.partial(_layer_kernel, is_ctx=is_ctx, n_seq=n_seq, seq_len=seq_len,
                             past_len=past_len, depth=depth, layer0=layer0)
    return pl.pallas_call(
        kern,
        grid=grid,
        in_specs=in_specs,
        out_specs=out_specs,
        out_shape=out_shape,
        scratch_shapes=scratch,
        compiler_params=pltpu.CompilerParams(
            dimension_semantics=("arbitrary", "arbitrary"),
            vmem_limit_bytes=VMEM_LIMIT),
        name="ctx_pass" if is_ctx else f"denoise_layer{layer0}",
    )(*args)


def kernel(x_prompt, x_sample, cache_gqa_k, cache_gqa_v, cache_diff_k, cache_diff_v, state_lru, c, c_ctx, w_mod, b_mod, g_pre, g_post, w_in, w_out, lru_conv_w, lru_conv_b, lru_wa, lru_ba, lru_wx, lru_bx, lru_lambda, gqa_gq, gqa_gk, diff_lam, diff_gsub):
    depth, d_model, _ = w_in.shape
    dec_batch = x_sample.shape[0]

    cond = jnp.concatenate(
        [c_ctx[None, :], c, jnp.zeros((MOD_ROWS - 1 - dec_batch, d_model), F32)], axis=0)
    n_mod = w_mod.shape[-1]
    mod_tile = n_mod // 3
    mod = pl.pallas_call(
        _mod_kernel,
        grid=(depth, n_mod // mod_tile),
        in_specs=[pl.BlockSpec((MOD_ROWS, d_model), lambda l, j: (0, 0)),
                  pl.BlockSpec((1, d_model, mod_tile), lambda l, j: (l, 0, j)),
                  pl.BlockSpec((1, 1, mod_tile), lambda l, j: (l, 0, j))],
        out_specs=pl.BlockSpec((1, MOD_ROWS, mod_tile), lambda l, j: (l, 0, j)),
        out_shape=jax.ShapeDtypeStruct((depth, MOD_ROWS, n_mod), F32),
        name="adaln_mod",
    )(cond, w_mod, b_mod[:, None, :])
    mod = mod[:, :, None, :]

    w_in_p = w_in.astype(BF16)
    w_out_p = w_out.astype(BF16)
    bd = jax.vmap(jax.vmap(_block_diag))
    wa_d, wx_d = bd(lru_wa), bd(lru_wx)
    n_ct = LRU_WIDTH // LANES
    wg = jnp.stack([
        jnp.concatenate([m[:, d, ct * LANES:(ct + 1) * LANES, ct * LANES:(ct + 1) * LANES]
                         for d in range(2) for m in (wa_d, wx_d)], axis=-1)
        for ct in range(n_ct)], axis=1).astype(BF16)
    bg = jnp.stack([
        jnp.concatenate([m[:, d, ct * LANES:(ct + 1) * LANES]
                         for d in range(2) for m in (lru_ba, lru_bx)], axis=-1)
        for ct in range(n_ct)], axis=1)[:, :, None, :]
    reps = LANES // HEAD_DIM
    weights = [g_pre[:, None, :], g_post[:, None, :], w_in_p, w_out_p, lru_conv_w,
               lru_conv_b[:, None, :], wg, bg, lru_lambda,
               jnp.broadcast_to(gqa_gq[:, :, None], (depth, HEAD_DIM, LANES)),
               jnp.tile(gqa_gk, (1, reps))[:, None, :],
               diff_lam, diff_gsub[:, None, :]]

    y_prompt, gk, gv, dk, dv, st = _layer_call(x_prompt, mod, weights, None, is_ctx=True, n_seq=2,
                                               layer0=0, n_layers=depth)
    b, t = x_prompt.shape[:2]
    new_gqa_k = gk.reshape(b, depth, GQA_KV_HEADS, HEAD_DIM, t).transpose(0, 1, 4, 2, 3)
    new_gqa_v = gv.reshape(b, depth, GQA_KV_HEADS, HEAD_DIM, t).transpose(0, 1, 4, 2, 3)
    new_diff_k = dk.reshape(b, depth, DIFF_HEADS, 2, HEAD_DIM, t).transpose(0, 1, 5, 2, 3, 4)
    new_diff_v = dv.reshape(b, depth, t, DIFF_HEADS, 2 * HEAD_DIM)

    db, _, past = cache_gqa_k.shape[:3]
    cos, sin = _rope_tables(x_sample.shape[1])

    def feature_major_chunks(tab):
        return tab.T.reshape(LANES, -1, ROW_CHUNK).transpose(1, 0, 2)

    extras = [cache_gqa_k.transpose(0, 1, 3, 4, 2).reshape(db, depth, GQA_KV_WIDTH, past),
              cache_gqa_v.transpose(0, 1, 3, 4, 2).reshape(db, depth, GQA_KV_WIDTH, past),
              cache_diff_k.transpose(0, 1, 3, 4, 5, 2).reshape(db, depth, DIFF_WIDTH, past),
              cache_diff_v.reshape(db, depth, past * DIFF_HEADS, 2 * HEAD_DIM),
              state_lru, cos, sin, feature_major_chunks(cos), feature_major_chunks(sin)]
    y_sample = x_sample
    for lyr in range(depth):
        (y_sample,) = _layer_call(y_sample, mod, weights, extras, is_ctx=False, n_seq=1,
                                  layer0=lyr, n_layers=1)

    return (y_prompt, y_sample, new_gqa_k, new_gqa_v, new_diff_k, new_diff_v, st)
```

```python
import functools
import math

import jax
import jax.numpy as jnp
from jax import lax
from jax.experimental import pallas as pl
from jax.experimental.pallas import tpu as pltpu

F32 = jnp.float32
BF16 = jnp.bfloat16

GRID_W = 64
HEAD_DIM = 64
EPS = 1e-6
ROPE_BASE = 10000.0
LRU_WIDTH = 256
LRU_BLOCKS = 4
LRU_C = 8.0
CONV_W = 4
CONV_LEFT = 2
GQA_Q_HEADS = 4
GQA_KV_HEADS = 2
GQA_WIDTH = GQA_Q_HEADS * HEAD_DIM
GQA_KV_WIDTH = GQA_KV_HEADS * HEAD_DIM
DIFF_HEADS = 4
DIFF_WIDTH = DIFF_HEADS * 2 * HEAD_DIM
QK_SCALE = HEAD_DIM ** -0.5 * math.log2(math.e)

C_LRU_X = 0
C_LRU_G = C_LRU_X + LRU_WIDTH
C_GQ = C_LRU_G + LRU_WIDTH
C_GK = C_GQ + GQA_WIDTH
C_GV = C_GK + GQA_KV_WIDTH
C_GG = C_GV + GQA_KV_WIDTH
C_DQ = C_GG + GQA_WIDTH
C_DK = C_DQ + DIFF_WIDTH
C_DV = C_DK + DIFF_WIDTH
C_DG = C_DV + DIFF_WIDTH
N_IN = C_DG + DIFF_WIDTH
M_LRU = 0
M_GQA = LRU_WIDTH
M_DIFF = LRU_WIDTH + GQA_WIDTH
D_MIX = LRU_WIDTH + GQA_WIDTH + DIFF_WIDTH

LANES = 128
SUBLANES = 8
ROW_CHUNK = 256
KEY_CHUNK = 256
MOD_ROWS = 16
VMEM_LIMIT = 58 * 1024 * 1024


def _lambda_init(layer):
    return 0.8 - 0.6 * math.exp(-0.3 * layer)


def _dot(a, b):
    return jnp.dot(a, b, preferred_element_type=F32)


def _silu(x):
    return x * jax.nn.sigmoid(x)


def _rms(x, g):
    ms = jnp.mean(x * x, axis=-1, keepdims=True)
    return x * lax.rsqrt(ms + EPS) * g


def _rms_heads(x, g2):
    lo = lax.broadcasted_iota(jnp.int32, x.shape, 1) < HEAD_DIM
    t = x * x
    s_lo = jnp.sum(jnp.where(lo, t, 0.0), axis=-1, keepdims=True)
    s_hi = jnp.sum(jnp.where(lo, 0.0, t), axis=-1, keepdims=True)
    ms = jnp.where(lo, s_lo, s_hi) * (1.0 / HEAD_DIM)
    return x * lax.rsqrt(ms + EPS) * g2


def _rope(x, cos, sin_signed):
    first = (lax.broadcasted_iota(jnp.int32, x.shape, 1) % 32) < 16
    partner = jnp.where(first, pltpu.roll(x, LANES - 16, 1), pltpu.roll(x, 16, 1))
    return x * cos + partner * sin_signed


def _shift_rows(x, k, fill, n_rows):
    rows = lax.broadcasted_iota(jnp.int32, x.shape, 0)
    if k > 0:
        return jnp.where(rows >= k, pltpu.roll(x, k, 0), fill)
    return jnp.where(rows < n_rows + k, pltpu.roll(x, n_rows + k, 0), fill)


def _linear_scan(a, x, n_rows, reverse):
    nb = n_rows // SUBLANES
    lanes = a.shape[-1]
    a3 = a.reshape(nb, SUBLANES, lanes)
    x3 = x.reshape(nb, SUBLANES, lanes)
    sub = lax.broadcasted_iota(jnp.int32, (nb, SUBLANES, lanes), 1)
    k = 1
    while k < SUBLANES:
        keep = (sub < SUBLANES - k) if reverse else (sub >= k)
        shift = SUBLANES - k if reverse else k
        x3 = a3 * jnp.where(keep, pltpu.roll(x3, shift, 1), 0.0) + x3
        a3 = a3 * jnp.where(keep, pltpu.roll(a3, shift, 1), 1.0)
        k *= 2
    blocks = [None] * nb
    carry = None
    for j in (range(nb - 1, -1, -1) if reverse else range(nb)):
        xb = x3[j]
        if carry is not None:
            xb = xb + a3[j] * carry
        blocks[j] = xb
        carry = xb[0:1] if reverse else xb[SUBLANES - 1:SUBLANES]
    return jnp.concatenate(blocks, axis=0)


_PHASE_END = "phase-end"


def _mod_kernel(cond_ref, w_ref, b_ref, o_ref):
    o_ref[0] = _dot(_silu(cond_ref[...]).astype(BF16), w_ref[0].astype(BF16)) + b_ref[0]


def _layer_kernel(*refs, is_ctx, n_seq, seq_len, past_len, depth, layer0, resident_weights):
    T = seq_len
    Tk = past_len + T
    n_chunks = T // ROW_CHUNK
    it = iter(refs)
    x_ref, mod_ref, gpre_ref, gpost_ref, win_ref, wout_ref = (next(it) for _ in range(6))
    convw_ref, convb_ref, wg_ref, bg_ref, lam_ref = (next(it) for _ in range(5))
    gqc_ref, gk_ref, dlam_ref, gsub_ref = (next(it) for _ in range(4))
    if not is_ctx:
        (ck_ref, cv_ref, cdk_ref, cdv_ref, st_ref, cos_ref, sin_ref, cost_ref, sint_ref) = (
            next(it) for _ in range(9))
    y_ref = next(it)
    if is_ctx:
        ogk_ref, ogv_ref, odk_ref, odv_ref, ost_ref = (next(it) for _ in range(5))
    scratch_refs = [next(it) for _ in range(11)]

    @pl.when(pl.program_id(1) == 0)
    def _():
        y_ref[...] = x_ref[...]

    layer = pl.program_id(1) + layer0
    wl = layer if resident_weights else 0

    lam_i = jnp.float32(_lambda_init(depth - 1))
    for lyr in range(depth - 2, -1, -1):
        lam_i = jnp.where(layer == lyr, jnp.float32(_lambda_init(lyr)), lam_i)

    shift = mod_ref[0, 0, :, 0:1024]
    scale = mod_ref[0, 0, :, 1024:2048]
    gate = mod_ref[0, 0, :, 2048:3072]
    dl = dlam_ref[0]
    lam = (jnp.exp(jnp.sum(dl[0:1] * dl[1:2], axis=-1, keepdims=True))
           - jnp.exp(jnp.sum(dl[2:3] * dl[3:4], axis=-1, keepdims=True)) + lam_i)

    def win(c0, n):
        return win_ref[wl, :, c0:c0 + n]

    def rows_of(c):
        return pl.ds(c * ROW_CHUNK, ROW_CHUNK)

    def sequence_steps(s):
        (h_scr, mix_scr, zx_scr, qt_scr, k_scr, vt_scr, dqt_scr, dk_scr, dvt_scr, s_scr, e_scr) = (
            ref.at[s] for ref in scratch_refs)

        def prenorm_and_lru_in(c):
            r = rows_of(c)
            hn = _rms(y_ref[s, r, :], gpre_ref[0]) * (1.0 + scale) + shift
            hc = hn.astype(BF16)
            h_scr[r, :] = hc
            zxg = _dot(hc, win(C_LRU_X, 2 * LRU_WIDTH))
            zx_scr[r, 0:LRU_WIDTH] = zxg[:, 0:LRU_WIDTH]
            zx_scr[r, LRU_WIDTH:] = _silu(zxg[:, LRU_WIDTH:])

        row_id = lax.broadcasted_iota(jnp.int32, (T, LANES), 0)
        lru_state = {}

        def lru_piece(ct, d):
            cl = slice(ct * LANES, (ct + 1) * LANES)
            reverse = d == 1
            if d == 0:
                zx = zx_scr[:, cl]
                u = convb_ref[0, :, cl] + zx * convw_ref[0, CONV_LEFT:CONV_LEFT + 1, cl]
                for j in range(CONV_W):
                    if j != CONV_LEFT:
                        u = u + _shift_rows(zx, CONV_LEFT - j, 0.0, T) * convw_ref[0, j:j + 1, cl]
                lru_state[ct] = (u, u.astype(BF16), None)
                yield
            u, u_bf, y_prev = lru_state[ct]
            gates = _dot(u_bf, wg_ref[wl, ct, :, d * 2 * LANES:(d + 1) * 2 * LANES])
            gates = gates + bg_ref[0, ct, :, d * 2 * LANES:(d + 1) * 2 * LANES]
            r_g = jax.nn.sigmoid(gates[:, 0:LANES])
            i_g = jax.nn.sigmoid(gates[:, LANES:])
            yield
            lam_d = lam_ref[0, d:d + 1, cl]
            sp = jnp.maximum(-lam_d, 0.0) + jnp.log1p(jnp.exp(-jnp.abs(lam_d)))
            a = jnp.exp(-LRU_C * r_g * sp)
            inp = jnp.sqrt(1.0 - a * a) * (i_g * u)
            if not is_ctx:
                h0 = st_ref[0, 0, d:d + 1, cl]
                first_row = (T - 1) if reverse else 0
                inp = inp + jnp.where(row_id == first_row, a * h0, 0.0)
            yield
            y_d = _linear_scan(a, inp, T, reverse)
            yield
            if is_ctx:
                last_row = 0 if reverse else T - 1
                ost_ref[s, 0, d:d + 1, cl] = y_d[last_row:last_row + 1, :]
            if d == 0:
                lru_state[ct] = (u, u_bf, y_d)
            else:
                lru_gate = zx_scr[:, LRU_WIDTH + ct * LANES:LRU_WIDTH + (ct + 1) * LANES]
                mix_scr[:, M_LRU + ct * LANES:M_LRU + (ct + 1) * LANES] = (
                    (y_prev + y_d) * lru_gate).astype(BF16)
            yield

        def cached_kv(c):
            r = rows_of(c)
            k_scr[r, :] = ck_ref[0, 0, :, r].T.astype(BF16)
            vt_scr[:, r] = cv_ref[0, 0, :, r].astype(BF16)
            for hd in range(DIFF_HEADS):
                cl = slice(hd * LANES, (hd + 1) * LANES)
                dk_scr[r, cl] = cdk_ref[0, 0, cl, r].T.astype(BF16)
                rows_h = pl.ds(c * ROW_CHUNK * DIFF_HEADS + hd, ROW_CHUNK, stride=DIFF_HEADS)
                dvt_scr[cl, r] = cdv_ref[0, 0, rows_h, :].T.astype(BF16)

        def new_kv(c):
            r = rows_of(c)
            rk = pl.ds(past_len + c * ROW_CHUNK, ROW_CHUNK)
            hc = h_scr[r, :]
            zkv = _dot(hc, win(C_GK, 2 * GQA_KV_WIDTH))
            k = _rms_heads(zkv[:, 0:GQA_KV_WIDTH], gk_ref[0])
            v_t = zkv[:, GQA_KV_WIDTH:].T
            if is_ctx:
                ogk_ref[s, 0, :, r] = k.T
                ogv_ref[s, 0, :, r] = v_t
            else:
                k = _rope(k, cos_ref[r, :], sin_ref[r, :])
            k_scr[rk, :] = k.astype(BF16)
            vt_scr[:, rk] = v_t.astype(BF16)
            yield
            zdk = _dot(hc, win(C_DK, DIFF_WIDTH))
            yield
            zdv = _dot(hc, win(C_DV, DIFF_WIDTH))
            yield
            for hd in range(DIFF_HEADS):
                cl = slice(hd * LANES, (hd + 1) * LANES)
                dk = zdk[:, cl]
                if is_ctx:
                    odk_ref[s, 0, cl, r] = dk.T
                    rows_h = pl.ds(c * ROW_CHUNK * DIFF_HEADS + hd, ROW_CHUNK, stride=DIFF_HEADS)
                    odv_ref[s, 0, rows_h, :] = zdv[:, cl]
                else:
                    dk = _rope(dk, cos_ref[r, :], sin_ref[r, :])
                dk_scr[rk, cl] = dk.astype(BF16)
                dvt_scr[cl, rk] = zdv[:, cl].T.astype(BF16)
                yield

        def new_q_piece(cq, g):
            start = cq * ROW_CHUNK
            rq = pl.ds(start if isinstance(cq, int) else pl.multiple_of(start, ROW_CHUNK), ROW_CHUNK)
            c0 = C_GQ if g == 0 else C_DQ + (g - 1) * 2 * LANES
            z = _dot(h_scr[rq, :], win(c0, 2 * LANES))
            for j in range(2):
                t = z[:, j * LANES:(j + 1) * LANES].T
                if g == 0:
                    gain = jnp.concatenate([gqc_ref[0]] * (ROW_CHUNK // LANES), axis=1)
                    halves = []
                    for hh in range(LANES // HEAD_DIM):
                        th = t[hh * HEAD_DIM:(hh + 1) * HEAD_DIM]
                        ms = jnp.sum(th * th, axis=0, keepdims=True) * (1.0 / HEAD_DIM)
                        halves.append(th * lax.rsqrt(ms + EPS) * gain)
                    t = jnp.concatenate(halves, axis=0)
                if not is_ctx:
                    half = HEAD_DIM // 4
                    blocks = [t[i * half:(i + 1) * half] for i in range(LANES // half)]
                    partner = jnp.concatenate(
                        [blocks[i + 1 - 2 * (i % 2)] for i in range(len(blocks))], axis=0)
                    t = t * cost_ref[cq] + partner * sint_ref[cq]
                t = (t * QK_SCALE).astype(BF16)
                if g == 0:
                    qt_scr[cq, j * LANES:(j + 1) * LANES, :] = t
                else:
                    hd = (g - 1) * 2 + j
                    dqt_scr[cq, hd * LANES:(hd + 1) * LANES, :] = t

        n_q_pieces = 1 + DIFF_HEADS // 2
        q_in_attention = n_chunks > 1

        for c in range(n_chunks):
            prenorm_and_lru_in(c)
            yield
        lru_pieces = [(ct, d) for ct in range(LRU_WIDTH // LANES) for d in range(2)]
        n_cached = past_len // ROW_CHUNK
        for i in range(max(len(lru_pieces), n_chunks, n_cached)):
            if i < n_chunks:
                yield from new_kv(i)
            if i < len(lru_pieces):
                yield from lru_piece(*lru_pieces[i])
            if i < n_cached:
                cached_kv(i)
                yield
        for g in range(n_q_pieces):
            new_q_piece(0, g)
            yield
        yield _PHASE_END

        zero_half = jnp.zeros((HEAD_DIM, ROW_CHUNK), BF16)

        group = GQA_Q_HEADS // GQA_KV_HEADS
        n_pairs = GQA_KV_HEADS + DIFF_HEADS
        n_kc = Tk // KEY_CHUNK

        def attn_steps(c):
            start = c * ROW_CHUNK
            r = pl.ds(start if isinstance(c, int) else pl.multiple_of(start, ROW_CHUNK), ROW_CHUNK)
            hc = h_scr[r, :]

            def pair_qt(p):
                if p < GQA_KV_HEADS:
                    heads = [qt_scr[c, (p * group + i) * HEAD_DIM:(p * group + i + 1) * HEAD_DIM, :]
                             for i in range(group)]
                    cols = [jnp.concatenate([q, zero_half] if p == 0 else [zero_half, q], axis=0)
                            for q in heads]
                else:
                    hd = p - GQA_KV_HEADS
                    q1 = dqt_scr[c, hd * LANES:hd * LANES + HEAD_DIM, :]
                    q2 = dqt_scr[c, hd * LANES + HEAD_DIM:(hd + 1) * LANES, :]
                    cols = [jnp.concatenate([q1, zero_half], axis=0),
                            jnp.concatenate([zero_half, q2], axis=0)]
                return jnp.concatenate(cols, axis=1)

            def pair_keys(p, i):
                rk = slice(i * KEY_CHUNK, (i + 1) * KEY_CHUNK)
                if p < GQA_KV_HEADS:
                    return k_scr[rk, :]
                hd = p - GQA_KV_HEADS
                return dk_scr[rk, hd * LANES:(hd + 1) * LANES]

            def pair_values_t(p):
                if p < GQA_KV_HEADS:
                    return vt_scr[...]
                hd = p - GQA_KV_HEADS
                return dvt_scr[hd * LANES:(hd + 1) * LANES, :]

            def scores(p, qt, i, m):
                sc = _dot(pair_keys(p, i), qt)
                s_scr[p % 2, i * KEY_CHUNK:(i + 1) * KEY_CHUNK, :] = sc
                mi = jnp.max(sc, axis=0, keepdims=True)
                return mi if m is None else jnp.maximum(m, mi)

            def finish(p, o_t, gates):
                if p < GQA_KV_HEADS:
                    o = jnp.concatenate(
                        [o_t[p * HEAD_DIM:(p + 1) * HEAD_DIM, i * ROW_CHUNK:(i + 1) * ROW_CHUNK]
                         for i in range(group)], axis=0).T
                    col = M_GQA + p * LANES
                    gt = gates[0][:, p * LANES:(p + 1) * LANES]
                else:
                    hd = p - GQA_KV_HEADS
                    o = _rms((o_t[:, 0:ROW_CHUNK] - lam * o_t[:, ROW_CHUNK:]).T, gsub_ref[0]) * (1.0 - lam_i)
                    col = M_DIFF + hd * LANES
                    gt = gates[1][:, hd * LANES:(hd + 1) * LANES]
                mix_scr[r, col:col + LANES] = (o * gt).astype(BF16)

            qt_next = pair_qt(0)
            m_next = None
            for i in range(n_kc):
                m_next = scores(0, qt_next, i, m_next)
            gates = (_silu(_dot(hc, win(C_GG, GQA_WIDTH))), _silu(_dot(hc, win(C_DG, DIFF_WIDTH))))
            yield
            for p in range(n_pairs):
                m = m_next
                m_next = None
                if p + 1 < n_pairs:
                    qt_next = pair_qt(p + 1)
                l = None
                for i in range(n_kc):
                    if p + 1 < n_pairs:
                        m_next = scores(p + 1, qt_next, i, m_next)
                    rk = slice(i * KEY_CHUNK, (i + 1) * KEY_CHUNK)
                    e = jnp.exp2(s_scr[p % 2, rk, :] - m)
                    li = jnp.sum(e, axis=0, keepdims=True)
                    l = li if l is None else l + li
                    e_scr[p % 2, rk, :] = e.astype(BF16)
                    yield
                if q_in_attention and p % 2 == 0 and p // 2 < n_q_pieces:
                    new_q_piece(jnp.where(c + 1 == n_chunks, 0, c + 1), p // 2)
                finish(p, _dot(pair_values_t(p), e_scr[p % 2]) / l, gates)
                yield

        if n_chunks == 1:
            yield from attn_steps(0)
        else:
            def attn_chunk(c, carry):
                for _ in attn_steps(c):
                    pass
                return carry
            lax.fori_loop(0, n_chunks, attn_chunk, 0)
            yield

        for c in range(n_chunks):
            r = rows_of(c)
            halves = []
            for j in range(2):
                half_n = wout_ref.shape[-1] // 2
                halves.append(_dot(mix_scr[r, :], wout_ref[wl, :, j * half_n:(j + 1) * half_n]))
                yield
            out = jnp.concatenate(halves, axis=1)
            y_ref[s, r, :] = y_ref[s, r, :] + gate * _rms(out, gpost_ref[0])
            yield

    def phase_done(gen):
        return next(gen, _PHASE_END) is _PHASE_END

    gens = [sequence_steps(s) for s in range(n_seq)]
    while not phase_done(gens[0]):
        pass
    for s in range(n_seq):
        cur_done = False
        nxt_done = s + 1 >= n_seq
        while not (cur_done and nxt_done):
            if not cur_done:
                cur_done = phase_done(gens[s])
            if not nxt_done:
                nxt_done = phase_done(gens[s + 1])


def _rope_tables(seq_len):
    quarter = HEAD_DIM // 4
    inv = jnp.power(ROPE_BASE, -jnp.arange(quarter, dtype=F32) / quarter)
    t = jnp.arange(seq_len)
    row = (t // GRID_W).astype(F32)[:, None] * inv[None]
    col = (t % GRID_W).astype(F32)[:, None] * inv[None]
    cos = jnp.concatenate([jnp.cos(row), jnp.cos(row), jnp.cos(col), jnp.cos(col)], axis=-1)
    sin = jnp.concatenate([-jnp.sin(row), jnp.sin(row), -jnp.sin(col), jnp.sin(col)], axis=-1)
    reps = LANES // HEAD_DIM
    return jnp.tile(cos, (1, reps)), jnp.tile(sin, (1, reps))


def _block_diag(w):
    nb, bw, _ = w.shape
    eye = jnp.eye(nb, dtype=w.dtype)
    return (eye[:, None, :, None] * w[:, :, None, :]).reshape(nb * bw, nb * bw)


def _layer_call(x, mod, weights, extras, *, is_ctx, n_seq, layer0, n_layers):
    batch, seq_len, d_model = x.shape
    depth = weights[2].shape[0]
    past_len = 0 if is_ctx else extras[0].shape[3]
    tk = past_len + seq_len
    n_chunks = seq_len // ROW_CHUNK
    grid = (batch // n_seq, n_layers)
    single = dict(pipeline_mode=pl.Buffered(1))
    w_mode = single if n_layers == 1 else {}
    resident_weights = n_layers > 1

    def per_layer(a):
        nd = a.ndim
        if resident_weights and a.dtype == BF16:
            return pl.BlockSpec(a.shape, lambda b, l: (0,) * nd, **single)
        return pl.BlockSpec((1,) + a.shape[1:], lambda b, l: (l + layer0,) + (0,) * (nd - 1), **w_mode)

    y_spec = pl.BlockSpec((n_seq, seq_len, d_model), lambda b, l: (b, 0, 0))
    x_spec = y_spec
    if is_ctx:
        mod_spec = pl.BlockSpec((1, 1, 1, mod.shape[-1]), lambda b, l: (l + layer0, 0, 0, 0))
    else:
        mod_spec = pl.BlockSpec((1, 1, 1, mod.shape[-1]), lambda b, l: (l + layer0, b + 1, 0, 0))
    in_specs = [x_spec, mod_spec] + [per_layer(w) for w in weights]
    args = [x, mod] + list(weights)
    if not is_ctx:
        for a in extras[:5]:
            in_specs.append(
                pl.BlockSpec((1, 1) + a.shape[2:], lambda b, l: (b, l + layer0, 0, 0)))
        for a in extras[5:]:
            in_specs.append(pl.BlockSpec(a.shape, lambda b, l, nd=a.ndim: (0,) * nd, **single))
        args += list(extras)

    out_shape = [jax.ShapeDtypeStruct(x.shape, F32)]
    out_specs = [y_spec]
    if is_ctx:
        for rows_, cols_ in ((GQA_KV_WIDTH, seq_len), (GQA_KV_WIDTH, seq_len), (DIFF_WIDTH, seq_len),
                             (seq_len * DIFF_HEADS, DIFF_WIDTH // DIFF_HEADS)):
            out_shape.append(jax.ShapeDtypeStruct((batch, depth, rows_, cols_), F32))
            out_specs.append(pl.BlockSpec((n_seq, 1, rows_, cols_), lambda b, l: (b, l, 0, 0)))
        out_shape.append(jax.ShapeDtypeStruct((batch, depth, 2, LRU_WIDTH), F32))
        out_specs.append(pl.BlockSpec((n_seq, 1, 2, LRU_WIDTH), lambda b, l: (b, l, 0, 0)))

    scratch = [pltpu.VMEM((n_seq,) + shape, dtype) for shape, dtype in (
        ((seq_len, d_model), BF16),
        ((seq_len, D_MIX), BF16),
        ((seq_len, 2 * LRU_WIDTH), F32),
        ((n_chunks, GQA_WIDTH, ROW_CHUNK), BF16),
        ((tk, GQA_KV_WIDTH), BF16),
        ((GQA_KV_WIDTH, tk), BF16),
        ((n_chunks, DIFF_WIDTH, ROW_CHUNK), BF16),
        ((tk, DIFF_WIDTH), BF16),
        ((DIFF_WIDTH, tk), BF16),
        ((2, tk, 2 * ROW_CHUNK), F32),
        ((2, tk, 2 * ROW_CHUNK), BF16),
    )]
    kern = functools.partial(_layer_kernel, is_ctx=is_ctx, n_seq=n_seq, seq_len=seq_len,
                             past_len=past_len, depth=depth, layer0=layer0,
                             resident_weights=resident_weights)
    return pl.pallas_call(
        kern,
        grid=grid,
        in_specs=in_specs,
        out_specs=out_specs,
        out_shape=out_shape,
        scratch_shapes=scratch,
        compiler_params=pltpu.CompilerParams(
            dimension_semantics=("arbitrary", "arbitrary"),
            vmem_limit_bytes=VMEM_LIMIT),
        name="ctx_pass" if is_ctx else f"denoise_layer{layer0}",
    )(*args)


def kernel(x_prompt, x_sample, cache_gqa_k, cache_gqa_v, cache_diff_k, cache_diff_v, state_lru, c, c_ctx, w_mod, b_mod, g_pre, g_post, w_in, w_out, lru_conv_w, lru_conv_b, lru_wa, lru_ba, lru_wx, lru_bx, lru_lambda, gqa_gq, gqa_gk, diff_lam, diff_gsub):
    depth, d_model, _ = w_in.shape
    dec_batch = x_sample.shape[0]

    cond = jnp.concatenate(
        [c_ctx[None, :], c, jnp.zeros((MOD_ROWS - 1 - dec_batch, d_model), F32)], axis=0)
    n_mod = w_mod.shape[-1]
    mod_tile = n_mod // 3
    mod = pl.pallas_call(
        _mod_kernel,
        grid=(depth, n_mod // mod_tile),
        in_specs=[pl.BlockSpec((MOD_ROWS, d_model), lambda l, j: (0, 0)),
                  pl.BlockSpec((1, d_model, mod_tile), lambda l, j: (l, 0, j)),
                  pl.BlockSpec((1, 1, mod_tile), lambda l, j: (l, 0, j))],
        out_specs=pl.BlockSpec((1, MOD_ROWS, mod_tile), lambda l, j: (l, 0, j)),
        out_shape=jax.ShapeDtypeStruct((depth, MOD_ROWS, n_mod), F32),
        name="adaln_mod",
    )(cond, w_mod, b_mod[:, None, :])
    mod = mod[:, :, None, :]

    w_in_p = w_in.astype(BF16)
    w_out_p = w_out.astype(BF16)
    bd = jax.vmap(jax.vmap(_block_diag))
    wa_d, wx_d = bd(lru_wa), bd(lru_wx)
    n_ct = LRU_WIDTH // LANES
    wg = jnp.stack([
        jnp.concatenate([m[:, d, ct * LANES:(ct + 1) * LANES, ct * LANES:(ct + 1) * LANES]
                         for d in range(2) for m in (wa_d, wx_d)], axis=-1)
        for ct in range(n_ct)], axis=1).astype(BF16)
    bg = jnp.stack([
        jnp.concatenate([m[:, d, ct * LANES:(ct + 1) * LANES]
                         for d in range(2) for m in (lru_ba, lru_bx)], axis=-1)
        for ct in range(n_ct)], axis=1)[:, :, None, :]
    reps = LANES // HEAD_DIM
    weights = [g_pre[:, None, :], g_post[:, None, :], w_in_p, w_out_p, lru_conv_w,
               lru_conv_b[:, None, :], wg, bg, lru_lambda,
               jnp.broadcast_to(gqa_gq[:, :, None], (depth, HEAD_DIM, LANES)),
               jnp.tile(gqa_gk, (1, reps))[:, None, :],
               diff_lam, diff_gsub[:, None, :]]

    y_prompt, gk, gv, dk, dv, st = _layer_call(x_prompt, mod, weights, None, is_ctx=True, n_seq=2,
                                               layer0=0, n_layers=depth)
    b, t = x_prompt.shape[:2]
    new_gqa_k = gk.reshape(b, depth, GQA_KV_HEADS, HEAD_DIM, t).transpose(0, 1, 4, 2, 3)
    new_gqa_v = gv.reshape(b, depth, GQA_KV_HEADS, HEAD_DIM, t).transpose(0, 1, 4, 2, 3)
    new_diff_k = dk.reshape(b, depth, DIFF_HEADS, 2, HEAD_DIM, t).transpose(0, 1, 5, 2, 3, 4)
    new_diff_v = dv.reshape(b, depth, t, DIFF_HEADS, 2 * HEAD_DIM)

    db, _, past = cache_gqa_k.shape[:3]
    cos, sin = _rope_tables(x_sample.shape[1])

    def feature_major_chunks(tab):
        return tab.T.reshape(LANES, -1, ROW_CHUNK).transpose(1, 0, 2)

    extras = [cache_gqa_k.transpose(0, 1, 3, 4, 2).reshape(db, depth, GQA_KV_WIDTH, past),
              cache_gqa_v.transpose(0, 1, 3, 4, 2).reshape(db, depth, GQA_KV_WIDTH, past),
              cache_diff_k.transpose(0, 1, 3, 4, 5, 2).reshape(db, depth, DIFF_WIDTH, past),
              cache_diff_v.reshape(db, depth, past * DIFF_HEADS, 2 * HEAD_DIM),
              state_lru, cos, sin, feature_major_chunks(cos), feature_major_chunks(sin)]
    y_sample = x_sample
    for lyr in range(depth):
        (y_sample,) = _layer_call(y_sample, mod, weights, extras, is_ctx=False, n_seq=1,
                                  layer0=lyr, n_layers=1)

    return (y_prompt, y_sample, new_gqa_k, new_gqa_v, new_diff_k, new_diff_v, st)
```

```python
import functools
import math

import jax
import jax.numpy as jnp
from jax import lax
from jax.experimental import pallas as pl
from jax.experimental.pallas import tpu as pltpu

F32 = jnp.float32
BF16 = jnp.bfloat16

GRID_W = 64
HEAD_DIM = 64
EPS = 1e-6
ROPE_BASE = 10000.0
LRU_WIDTH = 256
LRU_BLOCKS = 4
LRU_C = 8.0
CONV_W = 4
CONV_LEFT = 2
GQA_Q_HEADS = 4
GQA_KV_HEADS = 2
GQA_WIDTH = GQA_Q_HEADS * HEAD_DIM
GQA_KV_WIDTH = GQA_KV_HEADS * HEAD_DIM
DIFF_HEADS = 4
DIFF_WIDTH = DIFF_HEADS * 2 * HEAD_DIM
LOG2E = math.log2(math.e)
QK_SCALE = HEAD_DIM ** -0.5 * LOG2E

C_LRU_X = 0
C_LRU_G = C_LRU_X + LRU_WIDTH
C_GQ = C_LRU_G + LRU_WIDTH
C_GK = C_GQ + GQA_WIDTH
C_GV = C_GK + GQA_KV_WIDTH
C_GG = C_GV + GQA_KV_WIDTH
C_DQ = C_GG + GQA_WIDTH
C_DK = C_DQ + DIFF_WIDTH
C_DV = C_DK + DIFF_WIDTH
C_DG = C_DV + DIFF_WIDTH
N_IN = C_DG + DIFF_WIDTH
M_LRU = 0
M_GQA = LRU_WIDTH
M_DIFF = LRU_WIDTH + GQA_WIDTH
D_MIX = LRU_WIDTH + GQA_WIDTH + DIFF_WIDTH

LANES = 128
SUBLANES = 8
ROW_CHUNK = 256
KEY_CHUNK = 256
MOD_ROWS = 16
VMEM_LIMIT = 58 * 1024 * 1024


def _lambda_init(layer):
    return 0.8 - 0.6 * math.exp(-0.3 * layer)


def _dot(a, b):
    return jnp.dot(a, b, preferred_element_type=F32)


def _silu(x):
    return x * jax.nn.sigmoid(x)


def _rms(x, g):
    ms = jnp.mean(x * x, axis=-1, keepdims=True)
    return x * lax.rsqrt(ms + EPS) * g


def _rms_heads(x, g2):
    lo = lax.broadcasted_iota(jnp.int32, x.shape, 1) < HEAD_DIM
    t = x * x
    s_lo = jnp.sum(jnp.where(lo, t, 0.0), axis=-1, keepdims=True)
    s_hi = jnp.sum(jnp.where(lo, 0.0, t), axis=-1, keepdims=True)
    ms = jnp.where(lo, s_lo, s_hi) * (1.0 / HEAD_DIM)
    return x * lax.rsqrt(ms + EPS) * g2


def _rope(x, cos, sin_signed):
    first = (lax.broadcasted_iota(jnp.int32, x.shape, 1) % 32) < 16
    partner = jnp.where(first, pltpu.roll(x, LANES - 16, 1), pltpu.roll(x, 16, 1))
    return x * cos + partner * sin_signed


def _shift_rows(x, k, fill, n_rows):
    rows = lax.broadcasted_iota(jnp.int32, x.shape, 0)
    if k > 0:
        return jnp.where(rows >= k, pltpu.roll(x, k, 0), fill)
    return jnp.where(rows < n_rows + k, pltpu.roll(x, n_rows + k, 0), fill)


def _linear_scan(a, x, n_rows, reverse):
    nb = n_rows // SUBLANES
    lanes = a.shape[-1]
    a3 = a.reshape(nb, SUBLANES, lanes)
    x3 = x.reshape(nb, SUBLANES, lanes)
    sub = lax.broadcasted_iota(jnp.int32, (nb, SUBLANES, lanes), 1)
    k = 1
    while k < SUBLANES:
        keep = (sub < SUBLANES - k) if reverse else (sub >= k)
        shift = SUBLANES - k if reverse else k
        x3 = a3 * jnp.where(keep, pltpu.roll(x3, shift, 1), 0.0) + x3
        a3 = a3 * jnp.where(keep, pltpu.roll(a3, shift, 1), 1.0)
        k *= 2
    blocks = [None] * nb
    carry = None
    for j in (range(nb - 1, -1, -1) if reverse else range(nb)):
        xb = x3[j]
        if carry is not None:
            xb = xb + a3[j] * carry
        blocks[j] = xb
        carry = xb[0:1] if reverse else xb[SUBLANES - 1:SUBLANES]
    return jnp.concatenate(blocks, axis=0)


_PHASE_END = "phase-end"


def _mod_kernel(cond_ref, w_ref, b_ref, o_ref):
    o_ref[0] = _dot(_silu(cond_ref[...]).astype(BF16), w_ref[0].astype(BF16)) + b_ref[0]


def _layer_kernel(*refs, is_ctx, n_seq, seq_len, past_len, depth, layer0):
    T = seq_len
    Tk = past_len + T
    n_chunks = T // ROW_CHUNK
    it = iter(refs)
    x_ref, mod_ref, gpre_ref, gpost_ref, win_ref, wout_ref = (next(it) for _ in range(6))
    convw_ref, convb_ref, wg_ref, bg_ref, lam_ref = (next(it) for _ in range(5))
    gqc_ref, gk_ref, dlam_ref, gsub_ref = (next(it) for _ in range(4))
    if not is_ctx:
        (ck_ref, cv_ref, cdk_ref, cdv_ref, st_ref, cos_ref, sin_ref, cost_ref, sint_ref) = (
            next(it) for _ in range(9))
    y_ref = next(it)
    if is_ctx:
        ogk_ref, ogv_ref, odk_ref, odv_ref, ost_ref = (next(it) for _ in range(5))
    scratch_refs = [next(it) for _ in range(11)]

    @pl.when(pl.program_id(1) == 0)
    def _():
        y_ref[...] = x_ref[...]

    layer = pl.program_id(1) + layer0

    lam_i = jnp.float32(_lambda_init(depth - 1))
    for lyr in range(depth - 2, -1, -1):
        lam_i = jnp.where(layer == lyr, jnp.float32(_lambda_init(lyr)), lam_i)

    d_model = gpre_ref.shape[-1]
    shift = mod_ref[0, 0, :, 0:d_model]
    pre_gain = gpre_ref[0] * (1.0 + mod_ref[0, 0, :, d_model:2 * d_model])
    post_gain = gpost_ref[0] * mod_ref[0, 0, :, 2 * d_model:3 * d_model]
    dl = dlam_ref[0]
    lam = (jnp.exp(jnp.sum(dl[0:1] * dl[1:2], axis=-1, keepdims=True))
           - jnp.exp(jnp.sum(dl[2:3] * dl[3:4], axis=-1, keepdims=True)) + lam_i)

    def win(c0, n):
        return win_ref[0, :, c0:c0 + n]

    def rows_of(c):
        return pl.ds(c * ROW_CHUNK, ROW_CHUNK)

    def sequence_steps(s):
        (h_scr, mix_scr, zx_scr, qt_scr, k_scr, vt_scr, dqt_scr, dk_scr, dvt_scr, s_scr, e_scr) = (
            ref.at[s] for ref in scratch_refs)

        def prenorm_and_lru_in(c):
            r = rows_of(c)
            hn = _rms(y_ref[s, r, :], pre_gain) + shift
            hc = hn.astype(BF16)
            h_scr[r, :] = hc
            zxg = _dot(hc, win(C_LRU_X, 2 * LRU_WIDTH))
            zx_scr[r, 0:LRU_WIDTH] = zxg[:, 0:LRU_WIDTH]
            zx_scr[r, LRU_WIDTH:] = _silu(zxg[:, LRU_WIDTH:])

        row_id = lax.broadcasted_iota(jnp.int32, (T, LANES), 0)
        lru_state = {}

        def lru_piece(ct, d):
            cl = slice(ct * LANES, (ct + 1) * LANES)
            reverse = d == 1
            if d == 0:
                zx = zx_scr[:, cl]
                u = convb_ref[0, :, cl] + zx * convw_ref[0, CONV_LEFT:CONV_LEFT + 1, cl]
                for j in range(CONV_W):
                    if j != CONV_LEFT:
                        u = u + _shift_rows(zx, CONV_LEFT - j, 0.0, T) * convw_ref[0, j:j + 1, cl]
                lru_state[ct] = (u, u.astype(BF16), None)
                yield
            u, u_bf, y_prev = lru_state[ct]
            gates = _dot(u_bf, wg_ref[0, ct, :, d * 2 * LANES:(d + 1) * 2 * LANES])
            gates = gates + bg_ref[0, ct, :, d * 2 * LANES:(d + 1) * 2 * LANES]
            r_g = jax.nn.sigmoid(gates[:, 0:LANES])
            i_g = jax.nn.sigmoid(gates[:, LANES:])
            yield
            lam_d = lam_ref[0, d:d + 1, cl]
            sp = jnp.maximum(-lam_d, 0.0) + jnp.log1p(jnp.exp(-jnp.abs(lam_d)))
            a = jnp.exp2(r_g * (sp * (-LRU_C * LOG2E)))
            one_m_a2 = 1.0 - a * a
            root = jnp.where(one_m_a2 > 0.0, one_m_a2 * lax.rsqrt(one_m_a2), 0.0)
            inp = root * (i_g * u)
            if not is_ctx:
                h0 = st_ref[0, 0, d:d + 1, cl]
                first_row = (T - 1) if reverse else 0
                inp = inp + jnp.where(row_id == first_row, a * h0, 0.0)
            yield
            y_d = _linear_scan(a, inp, T, reverse)
            yield
            if is_ctx:
                last_row = 0 if reverse else T - 1
                ost_ref[s, 0, d:d + 1, cl] = y_d[last_row:last_row + 1, :]
            if d == 0:
                lru_state[ct] = (u, u_bf, y_d)
            else:
                lru_gate = zx_scr[:, LRU_WIDTH + ct * LANES:LRU_WIDTH + (ct + 1) * LANES]
                mix_scr[:, M_LRU + ct * LANES:M_LRU + (ct + 1) * LANES] = (
                    (y_prev + y_d) * lru_gate).astype(BF16)
            yield

        def cached_kv(c):
            r = rows_of(c)
            k_scr[r, :] = ck_ref[0, 0, :, r].T.astype(BF16)
            vt_scr[:, r] = cv_ref[0, 0, :, r].astype(BF16)
            for hd in range(DIFF_HEADS):
                cl = slice(hd * LANES, (hd + 1) * LANES)
                dk_scr[r, cl] = cdk_ref[0, 0, cl, r].T.astype(BF16)
                rows_h = pl.ds(c * ROW_CHUNK * DIFF_HEADS + hd, ROW_CHUNK, stride=DIFF_HEADS)
                dvt_scr[cl, r] = cdv_ref[0, 0, rows_h, :].T.astype(BF16)

        def new_kv(c):
            r = rows_of(c)
            rk = pl.ds(past_len + c * ROW_CHUNK, ROW_CHUNK)
            hc = h_scr[r, :]
            zkv = _dot(hc, win(C_GK, 2 * GQA_KV_WIDTH))
            k = _rms_heads(zkv[:, 0:GQA_KV_WIDTH], gk_ref[0])
            v_t = zkv[:, GQA_KV_WIDTH:].T
            if is_ctx:
                ogk_ref[s, 0, :, r] = k.T
                ogv_ref[s, 0, :, r] = v_t
            else:
                k = _rope(k, cos_ref[r, :], sin_ref[r, :])
            k_scr[rk, :] = k.astype(BF16)
            vt_scr[:, rk] = v_t.astype(BF16)
            yield
            zdk = _dot(hc, win(C_DK, DIFF_WIDTH))
            yield
            zdv = _dot(hc, win(C_DV, DIFF_WIDTH))
            yield
            for hd in range(DIFF_HEADS):
                cl = slice(hd * LANES, (hd + 1) * LANES)
                dk = zdk[:, cl]
                if is_ctx:
                    odk_ref[s, 0, cl, r] = dk.T
                    rows_h = pl.ds(c * ROW_CHUNK * DIFF_HEADS + hd, ROW_CHUNK, stride=DIFF_HEADS)
                    odv_ref[s, 0, rows_h, :] = zdv[:, cl]
                else:
                    dk = _rope(dk, cos_ref[r, :], sin_ref[r, :])
                dk_scr[rk, cl] = dk.astype(BF16)
                dvt_scr[cl, rk] = zdv[:, cl].T.astype(BF16)
                yield

        def new_q_piece(cq, g):
            start = cq * ROW_CHUNK
            rq = pl.ds(start if isinstance(cq, int) else pl.multiple_of(start, ROW_CHUNK), ROW_CHUNK)
            c0 = C_GQ if g == 0 else C_DQ + (g - 1) * 2 * LANES
            z = _dot(h_scr[rq, :], win(c0, 2 * LANES))
            for j in range(2):
                t = z[:, j * LANES:(j + 1) * LANES].T
                if g == 0:
                    gain = jnp.concatenate([gqc_ref[0]] * (ROW_CHUNK // LANES), axis=1)
                    halves = []
                    for hh in range(LANES // HEAD_DIM):
                        th = t[hh * HEAD_DIM:(hh + 1) * HEAD_DIM]
                        ms = jnp.sum(th * th, axis=0, keepdims=True) * (1.0 / HEAD_DIM)
                        halves.append(th * lax.rsqrt(ms + EPS) * gain)
                    t = jnp.concatenate(halves, axis=0)
                if not is_ctx:
                    half = HEAD_DIM // 4
                    blocks = [t[i * half:(i + 1) * half] for i in range(LANES // half)]
                    partner = jnp.concatenate(
                        [blocks[i + 1 - 2 * (i % 2)] for i in range(len(blocks))], axis=0)
                    t = t * cost_ref[cq] + partner * sint_ref[cq]
                t = (t * QK_SCALE).astype(BF16)
                if g == 0:
                    qt_scr[cq, j * LANES:(j + 1) * LANES, :] = t
                else:
                    hd = (g - 1) * 2 + j
                    dqt_scr[cq, hd * LANES:(hd + 1) * LANES, :] = t

        n_q_pieces = 1 + DIFF_HEADS // 2

        for c in range(n_chunks):
            prenorm_and_lru_in(c)
            yield
        lru_pieces = [(ct, d) for ct in range(LRU_WIDTH // LANES) for d in range(2)]
        n_cached = past_len // ROW_CHUNK
        for i in range(max(len(lru_pieces), n_chunks, n_cached)):
            if i < n_chunks:
                yield from new_kv(i)
            if i < len(lru_pieces):
                yield from lru_piece(*lru_pieces[i])
            if i < n_cached:
                cached_kv(i)
                yield
        for g in range(n_q_pieces):
            new_q_piece(0, g)
            yield
        yield _PHASE_END

        zero_half = jnp.zeros((HEAD_DIM, ROW_CHUNK), BF16)

        group = GQA_Q_HEADS // GQA_KV_HEADS
        n_pairs = GQA_KV_HEADS + DIFF_HEADS
        n_kc = Tk // KEY_CHUNK

        def pair_qt(cq, p):
            if p < GQA_KV_HEADS:
                heads = [qt_scr[cq, (p * group + i) * HEAD_DIM:(p * group + i + 1) * HEAD_DIM, :]
                         for i in range(group)]
                cols = [jnp.concatenate([q, zero_half] if p == 0 else [zero_half, q], axis=0)
                        for q in heads]
            else:
                hd = p - GQA_KV_HEADS
                q1 = dqt_scr[cq, hd * LANES:hd * LANES + HEAD_DIM, :]
                q2 = dqt_scr[cq, hd * LANES + HEAD_DIM:(hd + 1) * LANES, :]
                cols = [jnp.concatenate([q1, zero_half], axis=0),
                        jnp.concatenate([zero_half, q2], axis=0)]
            return jnp.concatenate(cols, axis=1)

        def attn_steps(c, next_q):
            start = c * ROW_CHUNK
            r = pl.ds(start if isinstance(c, int) else pl.multiple_of(start, ROW_CHUNK), ROW_CHUNK)
            hc = h_scr[r, :]

            def pair_keys(p, i):
                rk = slice(i * KEY_CHUNK, (i + 1) * KEY_CHUNK)
                if p < GQA_KV_HEADS:
                    return k_scr[rk, :]
                hd = p - GQA_KV_HEADS
                return dk_scr[rk, hd * LANES:(hd + 1) * LANES]

            def pair_values_t(p):
                if p < GQA_KV_HEADS:
                    return vt_scr[...]
                hd = p - GQA_KV_HEADS
                return dvt_scr[hd * LANES:(hd + 1) * LANES, :]

            def scores(p, qt, i, m):
                sc = _dot(pair_keys(p, i), qt)
                s_scr[p % 2, i * KEY_CHUNK:(i + 1) * KEY_CHUNK, :] = sc
                mi = jnp.max(sc, axis=0, keepdims=True)
                return mi if m is None else jnp.maximum(m, mi)

            def finish(p, o_t, gates):
                if p < GQA_KV_HEADS:
                    o = jnp.concatenate(
                        [o_t[p * HEAD_DIM:(p + 1) * HEAD_DIM, i * ROW_CHUNK:(i + 1) * ROW_CHUNK]
                         for i in range(group)], axis=0).T
                    col = M_GQA + p * LANES
                    gt = gates[0][:, p * LANES:(p + 1) * LANES]
                else:
                    hd = p - GQA_KV_HEADS
                    o = _rms((o_t[:, 0:ROW_CHUNK] - lam * o_t[:, ROW_CHUNK:]).T, gsub_ref[0]) * (1.0 - lam_i)
                    col = M_DIFF + hd * LANES
                    gt = gates[1][:, hd * LANES:(hd + 1) * LANES]
                mix_scr[r, col:col + LANES] = (o * gt).astype(BF16)

            qt_next = pair_qt(c, 0)
            m_next = None
            for i in range(n_kc):
                m_next = scores(0, qt_next, i, m_next)
            gates = (_silu(_dot(hc, win(C_GG, GQA_WIDTH))), _silu(_dot(hc, win(C_DG, DIFF_WIDTH))))
            yield
            for p in range(n_pairs):
                m = m_next
                m_next = None
                if p + 1 < n_pairs:
                    qt_next = pair_qt(c, p + 1)
                l = None
                for i in range(n_kc):
                    if p + 1 < n_pairs:
                        m_next = scores(p + 1, qt_next, i, m_next)
                    rk = slice(i * KEY_CHUNK, (i + 1) * KEY_CHUNK)
                    e = jnp.exp2(s_scr[p % 2, rk, :] - m)
                    li = jnp.sum(e, axis=0, keepdims=True)
                    l = li if l is None else l + li
                    e_scr[p % 2, rk, :] = e.astype(BF16)
                    yield
                if next_q and p % 2 == 0 and p // 2 < n_q_pieces:
                    new_q_piece(jnp.where(c + 1 == n_chunks, 0, c + 1), p // 2)
                finish(p, _dot(pair_values_t(p), e_scr[p % 2]) / l, gates)
                yield

        if n_chunks == 1:
            yield from attn_steps(0, next_q=False)
        else:
            def attn_chunk(c, carry):
                for _ in attn_steps(c, next_q=True):
                    pass
                return carry
            lax.fori_loop(0, n_chunks, attn_chunk, 0)
            yield

        for c in range(n_chunks):
            r = rows_of(c)
            halves = []
            for j in range(2):
                half_n = wout_ref.shape[-1] // 2
                halves.append(_dot(mix_scr[r, :], wout_ref[0, :, j * half_n:(j + 1) * half_n]))
                yield
            out = jnp.concatenate(halves, axis=1)
            y_ref[s, r, :] = y_ref[s, r, :] + _rms(out, post_gain)
            yield

    def phase_done(gen):
        return next(gen, _PHASE_END) is _PHASE_END

    gens = [sequence_steps(s) for s in range(n_seq)]
    while not phase_done(gens[0]):
        pass
    for s in range(n_seq):
        cur_done = False
        nxt_done = s + 1 >= n_seq
        while not (cur_done and nxt_done):
            if not cur_done:
                cur_done = phase_done(gens[s])
            if not nxt_done:
                nxt_done = phase_done(gens[s + 1])


def _rope_tables(seq_len):
    quarter = HEAD_DIM // 4
    inv = jnp.power(ROPE_BASE, -jnp.arange(quarter, dtype=F32) / quarter)
    t = jnp.arange(seq_len)
    row = (t // GRID_W).astype(F32)[:, None] * inv[None]
    col = (t % GRID_W).astype(F32)[:, None] * inv[None]
    cos = jnp.concatenate([jnp.cos(row), jnp.cos(row), jnp.cos(col), jnp.cos(col)], axis=-1)
    sin = jnp.concatenate([-jnp.sin(row), jnp.sin(row), -jnp.sin(col), jnp.sin(col)], axis=-1)
    reps = LANES // HEAD_DIM
    return jnp.tile(cos, (1, reps)), jnp.tile(sin, (1, reps))


def _block_diag(w):
    nb, bw, _ = w.shape
    eye = jnp.eye(nb, dtype=w.dtype)
    return (eye[:, None, :, None] * w[:, :, None, :]).reshape(nb * bw, nb * bw)


def _layer_call(x, mod, weights, extras, *, is_ctx, n_seq, layer0, n_layers):
    batch, seq_len, d_model = x.shape
    depth = weights[2].shape[0]
    past_len = 0 if is_ctx else extras[0].shape[3]
    tk = past_len + seq_len
    n_chunks = seq_len // ROW_CHUNK
    grid = (batch // n_seq, n_layers)
    single = dict(pipeline_mode=pl.Buffered(1))
    w_mode = single if n_layers == 1 else {}

    def per_layer(a):
        nd = a.ndim
        return pl.BlockSpec((1,) + a.shape[1:], lambda b, l: (l + layer0,) + (0,) * (nd - 1), **w_mode)

    y_spec = pl.BlockSpec((n_seq, seq_len, d_model), lambda b, l: (b, 0, 0))
    x_spec = y_spec
    if is_ctx:
        mod_spec = pl.BlockSpec((1, 1, 1, mod.shape[-1]), lambda b, l: (l + layer0, 0, 0, 0))
    else:
        mod_spec = pl.BlockSpec((1, 1, 1, mod.shape[-1]), lambda b, l: (l + layer0, b + 1, 0, 0))
    in_specs = [x_spec, mod_spec] + [per_layer(w) for w in weights]
    args = [x, mod] + list(weights)
    if not is_ctx:
        for a in extras[:5]:
            in_specs.append(
                pl.BlockSpec((1, 1) + a.shape[2:], lambda b, l: (b, l + layer0, 0, 0)))
        for a in extras[5:]:
            in_specs.append(pl.BlockSpec(a.shape, lambda b, l, nd=a.ndim: (0,) * nd, **single))
        args += list(extras)

    out_shape = [jax.ShapeDtypeStruct(x.shape, F32)]
    out_specs = [y_spec]
    if is_ctx:
        for rows_, cols_ in ((GQA_KV_WIDTH, seq_len), (GQA_KV_WIDTH, seq_len), (DIFF_WIDTH, seq_len),
                             (seq_len * DIFF_HEADS, DIFF_WIDTH // DIFF_HEADS)):
            out_shape.append(jax.ShapeDtypeStruct((batch, depth, rows_, cols_), F32))
            out_specs.append(pl.BlockSpec((n_seq, 1, rows_, cols_), lambda b, l: (b, l, 0, 0)))
        out_shape.append(jax.ShapeDtypeStruct((batch, depth, 2, LRU_WIDTH), F32))
        out_specs.append(pl.BlockSpec((n_seq, 1, 2, LRU_WIDTH), lambda b, l: (b, l, 0, 0)))

    scratch = [pltpu.VMEM((n_seq,) + shape, dtype) for shape, dtype in (
        ((seq_len, d_model), BF16),
        ((seq_len, D_MIX), BF16),
        ((seq_len, 2 * LRU_WIDTH), F32),
        ((n_chunks, GQA_WIDTH, ROW_CHUNK), BF16),
        ((tk, GQA_KV_WIDTH), BF16),
        ((GQA_KV_WIDTH, tk), BF16),
        ((n_chunks, DIFF_WIDTH, ROW_CHUNK), BF16),
        ((tk, DIFF_WIDTH), BF16),
        ((DIFF_WIDTH, tk), BF16),
        ((2, tk, 2 * ROW_CHUNK), F32),
        ((2, tk, 2 * ROW_CHUNK), BF16),
    )]
    kern = functools.partial(_layer_kernel, is_ctx=is_ctx, n_seq=n_seq, seq_len=seq_len,
                             past_len=past_len, depth=depth, layer0=layer0)
    return pl.pallas_call(
        kern,
        grid=grid,
        in_specs=in_specs,
        out_specs=out_specs,
        out_shape=out_shape,
        scratch_shapes=scratch,
        compiler_params=pltpu.CompilerParams(
            dimension_semantics=("arbitrary", "arbitrary"),
            vmem_limit_bytes=VMEM_LIMIT),
        name="ctx_pass" if is_ctx else f"denoise_layer{layer0}",
    )(*args)


def kernel(x_prompt, x_sample, cache_gqa_k, cache_gqa_v, cache_diff_k, cache_diff_v, state_lru, c, c_ctx, w_mod, b_mod, g_pre, g_post, w_in, w_out, lru_conv_w, lru_conv_b, lru_wa, lru_ba, lru_wx, lru_bx, lru_lambda, gqa_gq, gqa_gk, diff_lam, diff_gsub):
    depth, d_model, _ = w_in.shape
    dec_batch = x_sample.shape[0]

    cond = jnp.concatenate(
        [c_ctx[None, :], c, jnp.zeros((MOD_ROWS - 1 - dec_batch, d_model), F32)], axis=0)
    n_mod = w_mod.shape[-1]
    mod_tile = n_mod // 3
    mod = pl.pallas_call(
        _mod_kernel,
        grid=(depth, n_mod // mod_tile),
        in_specs=[pl.BlockSpec((MOD_ROWS, d_model), lambda l, j: (0, 0)),
                  pl.BlockSpec((1, d_model, mod_tile), lambda l, j: (l, 0, j)),
                  pl.BlockSpec((1, 1, mod_tile), lambda l, j: (l, 0, j))],
        out_specs=pl.BlockSpec((1, MOD_ROWS, mod_tile), lambda l, j: (l, 0, j)),
        out_shape=jax.ShapeDtypeStruct((depth, MOD_ROWS, n_mod), F32),
        name="adaln_mod",
    )(cond, w_mod, b_mod[:, None, :])
    mod = mod[:, :, None, :]

    w_in_p = w_in.astype(BF16)
    w_out_p = w_out.astype(BF16)
    bd = jax.vmap(jax.vmap(_block_diag))
    wa_d, wx_d = bd(lru_wa), bd(lru_wx)
    n_ct = LRU_WIDTH // LANES
    wg = jnp.stack([
        jnp.concatenate([m[:, d, ct * LANES:(ct + 1) * LANES, ct * LANES:(ct + 1) * LANES]
                         for d in range(2) for m in (wa_d, wx_d)], axis=-1)
        for ct in range(n_ct)], axis=1).astype(BF16)
    bg = jnp.stack([
        jnp.concatenate([m[:, d, ct * LANES:(ct + 1) * LANES]
                         for d in range(2) for m in (lru_ba, lru_bx)], axis=-1)
        for ct in range(n_ct)], axis=1)[:, :, None, :]
    reps = LANES // HEAD_DIM
    weights = [g_pre[:, None, :], g_post[:, None, :], w_in_p, w_out_p, lru_conv_w,
               lru_conv_b[:, None, :], wg, bg, lru_lambda,
               jnp.broadcast_to(gqa_gq[:, :, None], (depth, HEAD_DIM, LANES)),
               jnp.tile(gqa_gk, (1, reps))[:, None, :],
               diff_lam, diff_gsub[:, None, :]]

    y_prompt, gk, gv, dk, dv, st = _layer_call(x_prompt, mod, weights, None, is_ctx=True, n_seq=2,
                                               layer0=0, n_layers=depth)
    b, t = x_prompt.shape[:2]
    new_gqa_k = gk.reshape(b, depth, GQA_KV_HEADS, HEAD_DIM, t).transpose(0, 1, 4, 2, 3)
    new_gqa_v = gv.reshape(b, depth, GQA_KV_HEADS, HEAD_DIM, t).transpose(0, 1, 4, 2, 3)
    new_diff_k = dk.reshape(b, depth, DIFF_HEADS, 2, HEAD_DIM, t).transpose(0, 1, 5, 2, 3, 4)
    new_diff_v = dv.reshape(b, depth, t, DIFF_HEADS, 2 * HEAD_DIM)

    db, _, past = cache_gqa_k.shape[:3]
    cos, sin = _rope_tables(x_sample.shape[1])

    def feature_major_chunks(tab):
        return tab.T.reshape(LANES, -1, ROW_CHUNK).transpose(1, 0, 2)

    extras = [cache_gqa_k.transpose(0, 1, 3, 4, 2).reshape(db, depth, GQA_KV_WIDTH, past),
              cache_gqa_v.transpose(0, 1, 3, 4, 2).reshape(db, depth, GQA_KV_WIDTH, past),
              cache_diff_k.transpose(0, 1, 3, 4, 5, 2).reshape(db, depth, DIFF_WIDTH, past),
              cache_diff_v.reshape(db, depth, past * DIFF_HEADS, 2 * HEAD_DIM),
              state_lru, cos, sin, feature_major_chunks(cos), feature_major_chunks(sin)]
    y_sample = x_sample
    for lyr in range(depth):
        (y_sample,) = _layer_call(y_sample, mod, weights, extras, is_ctx=False, n_seq=1,
                                  layer0=lyr, n_layers=1)

    return (y_prompt, y_sample, new_gqa_k, new_gqa_v, new_diff_k, new_diff_v, st)
```

```python
import functools
import math

import jax
import jax.numpy as jnp
import numpy as np
from jax import lax
from jax.experimental import pallas as pl
from jax.experimental.pallas import tpu as pltpu

F32 = jnp.float32
BF16 = jnp.bfloat16

GRID_W = 64
HEAD_DIM = 64
EPS = 1e-6
ROPE_BASE = 10000.0
LRU_WIDTH = 256
LRU_BLOCKS = 4
LRU_C = 8.0
CONV_W = 4
CONV_LEFT = 2
GQA_Q_HEADS = 4
GQA_KV_HEADS = 2
GQA_WIDTH = GQA_Q_HEADS * HEAD_DIM
GQA_KV_WIDTH = GQA_KV_HEADS * HEAD_DIM
DIFF_HEADS = 4
DIFF_WIDTH = DIFF_HEADS * 2 * HEAD_DIM
LOG2E = math.log2(math.e)
QK_SCALE = HEAD_DIM ** -0.5 * LOG2E

C_LRU_X = 0
C_LRU_G = C_LRU_X + LRU_WIDTH
C_GQ = C_LRU_G + LRU_WIDTH
C_GK = C_GQ + GQA_WIDTH
C_GV = C_GK + GQA_KV_WIDTH
C_GG = C_GV + GQA_KV_WIDTH
C_DQ = C_GG + GQA_WIDTH
C_DK = C_DQ + DIFF_WIDTH
C_DV = C_DK + DIFF_WIDTH
C_DG = C_DV + DIFF_WIDTH
N_IN = C_DG + DIFF_WIDTH
M_LRU = 0
M_GQA = LRU_WIDTH
M_DIFF = LRU_WIDTH + GQA_WIDTH
D_MIX = LRU_WIDTH + GQA_WIDTH + DIFF_WIDTH

LANES = 128
SUBLANES = 8
ROW_CHUNK = 256
KEY_CHUNK = 256
MOD_ROWS = 16
VMEM_LIMIT = 58 * 1024 * 1024


def _lambda_init(layer):
    return 0.8 - 0.6 * math.exp(-0.3 * layer)


def _dot(a, b):
    return jnp.dot(a, b, preferred_element_type=F32)


def _silu(x):
    return x * jax.nn.sigmoid(x)


def _rms(x, g):
    ms = jnp.mean(x * x, axis=-1, keepdims=True)
    return x * lax.rsqrt(ms + EPS) * g


def _rms_heads(x, g2):
    lo = lax.broadcasted_iota(jnp.int32, x.shape, 1) < HEAD_DIM
    t = x * x
    s_lo = jnp.sum(jnp.where(lo, t, 0.0), axis=-1, keepdims=True)
    s_hi = jnp.sum(jnp.where(lo, 0.0, t), axis=-1, keepdims=True)
    ms = jnp.where(lo, s_lo, s_hi) * (1.0 / HEAD_DIM)
    return x * lax.rsqrt(ms + EPS) * g2


def _rope(x, cos, sin_signed):
    first = (lax.broadcasted_iota(jnp.int32, x.shape, 1) % 32) < 16
    partner = jnp.where(first, pltpu.roll(x, LANES - 16, 1), pltpu.roll(x, 16, 1))
    return x * cos + partner * sin_signed


def _shift_rows(x, k, fill, n_rows):
    rows = lax.broadcasted_iota(jnp.int32, x.shape, 0)
    if k > 0:
        return jnp.where(rows >= k, pltpu.roll(x, k, 0), fill)
    return jnp.where(rows < n_rows + k, pltpu.roll(x, n_rows + k, 0), fill)


def _linear_scan(a, x, n_rows, reverse):
    nb = n_rows // SUBLANES
    lanes = a.shape[-1]
    a3 = a.reshape(nb, SUBLANES, lanes)
    x3 = x.reshape(nb, SUBLANES, lanes)
    sub = lax.broadcasted_iota(jnp.int32, (nb, SUBLANES, lanes), 1)
    k = 1
    while k < SUBLANES:
        keep = (sub < SUBLANES - k) if reverse else (sub >= k)
        shift = SUBLANES - k if reverse else k
        x3 = a3 * jnp.where(keep, pltpu.roll(x3, shift, 1), 0.0) + x3
        a3 = a3 * jnp.where(keep, pltpu.roll(a3, shift, 1), 1.0)
        k *= 2
    blocks = [None] * nb
    carry = None
    for j in (range(nb - 1, -1, -1) if reverse else range(nb)):
        xb = x3[j]
        if carry is not None:
            xb = xb + a3[j] * carry
        blocks[j] = xb
        carry = xb[0:1] if reverse else xb[SUBLANES - 1:SUBLANES]
    return jnp.concatenate(blocks, axis=0)


_PHASE_END = "phase-end"


def _mod_kernel(cond_ref, w_ref, b_ref, o_ref):
    o_ref[0] = _dot(_silu(cond_ref[...]).astype(BF16), w_ref[0].astype(BF16)) + b_ref[0]


def _layer_kernel(*refs, is_ctx, n_seq, seq_len, past_len, depth, layer0):
    T = seq_len
    Tk = past_len + T
    n_chunks = T // ROW_CHUNK
    it = iter(refs)
    x_ref, mod_ref, gpre_ref, gpost_ref, win_ref, wout_ref = (next(it) for _ in range(6))
    convw_ref, convb_ref, wg_ref, bg_ref, lam_ref = (next(it) for _ in range(5))
    gqc_ref, gk_ref, dlam_ref, gsub_ref = (next(it) for _ in range(4))
    if not is_ctx:
        (ck_ref, cv_ref, cdk_ref, cdv_ref, st_ref, cos_ref, sin_ref, cost_ref, sint_ref) = (
            next(it) for _ in range(9))
    y_ref = next(it)
    if is_ctx:
        ogk_ref, ogv_ref, odk_ref, odv_ref, ost_ref = (next(it) for _ in range(5))
    scratch_refs = [next(it) for _ in range(11)]

    @pl.when(pl.program_id(1) == 0)
    def _():
        y_ref[...] = x_ref[...]

    layer = pl.program_id(1) + layer0

    lam_i = jnp.float32(_lambda_init(depth - 1))
    for lyr in range(depth - 2, -1, -1):
        lam_i = jnp.where(layer == lyr, jnp.float32(_lambda_init(lyr)), lam_i)

    def layer_row(ref):
        return ref[pl.ds(layer, 1), :]

    conv_w, conv_b = convw_ref[layer], layer_row(convb_ref)
    gate_bias, lam_all = bg_ref[layer], lam_ref[layer]
    q_gain_cols, k_gain, sub_gain = gqc_ref[layer], layer_row(gk_ref), layer_row(gsub_ref)
    dl = dlam_ref[layer]

    d_model = gpre_ref.shape[-1]
    mod_row = mod_ref[0, 0:1, :] if is_ctx else mod_ref[0, pl.ds(pl.program_id(0) + 1, 1), :]
    shift = mod_row[:, 0:d_model]
    pre_gain = layer_row(gpre_ref) * (1.0 + mod_row[:, d_model:2 * d_model])
    post_gain = layer_row(gpost_ref) * mod_row[:, 2 * d_model:3 * d_model]
    lam = (jnp.exp(jnp.sum(dl[0:1] * dl[1:2], axis=-1, keepdims=True))
           - jnp.exp(jnp.sum(dl[2:3] * dl[3:4], axis=-1, keepdims=True)) + lam_i)

    def win(c0, n):
        return win_ref[0, :, c0:c0 + n]

    def rows_of(c):
        return pl.ds(c * ROW_CHUNK, ROW_CHUNK)

    def sequence_steps(s):
        (h_scr, mix_scr, zx_scr, qt_scr, k_scr, vt_scr, dqt_scr, dk_scr, dvt_scr, s_scr, e_scr) = (
            ref.at[s] for ref in scratch_refs)

        def prenorm_and_lru_in(c):
            r = rows_of(c)
            hn = _rms(y_ref[s, r, :], pre_gain) + shift
            hc = hn.astype(BF16)
            h_scr[r, :] = hc
            zxg = _dot(hc, win(C_LRU_X, 2 * LRU_WIDTH))
            zx_scr[r, 0:LRU_WIDTH] = zxg[:, 0:LRU_WIDTH]
            zx_scr[r, LRU_WIDTH:] = _silu(zxg[:, LRU_WIDTH:])

        row_id = lax.broadcasted_iota(jnp.int32, (T, LANES), 0)
        lru_state = {}

        def lru_piece(ct, d):
            cl = slice(ct * LANES, (ct + 1) * LANES)
            reverse = d == 1
            if d == 0:
                zx = zx_scr[:, cl]
                u = conv_b[:, cl] + zx * conv_w[CONV_LEFT:CONV_LEFT + 1, cl]
                for j in range(CONV_W):
                    if j != CONV_LEFT:
                        u = u + _shift_rows(zx, CONV_LEFT - j, 0.0, T) * conv_w[j:j + 1, cl]
                lru_state[ct] = (u, u.astype(BF16), None)
                yield
            u, u_bf, y_prev = lru_state[ct]
            gates = _dot(u_bf, wg_ref[0, ct, :, d * 2 * LANES:(d + 1) * 2 * LANES])
            gates = gates + gate_bias[ct:ct + 1, d * 2 * LANES:(d + 1) * 2 * LANES]
            r_g = jax.nn.sigmoid(gates[:, 0:LANES])
            i_g = jax.nn.sigmoid(gates[:, LANES:])
            yield
            lam_d = lam_all[d:d + 1, cl]
            sp = jnp.maximum(-lam_d, 0.0) + jnp.log1p(jnp.exp(-jnp.abs(lam_d)))
            a = jnp.exp2(r_g * (sp * (-LRU_C * LOG2E)))
            one_m_a2 = 1.0 - a * a
            root = jnp.where(one_m_a2 > 0.0, one_m_a2 * lax.rsqrt(one_m_a2), 0.0)
            inp = root * (i_g * u)
            if not is_ctx:
                h0 = st_ref[0, 0, d:d + 1, cl]
                first_row = (T - 1) if reverse else 0
                inp = inp + jnp.where(row_id == first_row, a * h0, 0.0)
            yield
            y_d = _linear_scan(a, inp, T, reverse)
            yield
            if is_ctx:
                last_row = 0 if reverse else T - 1
                ost_ref[s, 0, d:d + 1, cl] = y_d[last_row:last_row + 1, :]
            if d == 0:
                lru_state[ct] = (u, u_bf, y_d)
            else:
                lru_gate = zx_scr[:, LRU_WIDTH + ct * LANES:LRU_WIDTH + (ct + 1) * LANES]
                mix_scr[:, M_LRU + ct * LANES:M_LRU + (ct + 1) * LANES] = (
                    (y_prev + y_d) * lru_gate).astype(BF16)
            yield

        def cached_kv(c):
            r = rows_of(c)
            k_scr[r, :] = ck_ref[0, 0, :, r].T.astype(BF16)
            vt_scr[:, r] = cv_ref[0, 0, :, r].astype(BF16)
            for hd in range(DIFF_HEADS):
                cl = slice(hd * LANES, (hd + 1) * LANES)
                dk_scr[r, cl] = cdk_ref[0, 0, cl, r].T.astype(BF16)
                rows_h = pl.ds(c * ROW_CHUNK * DIFF_HEADS + hd, ROW_CHUNK, stride=DIFF_HEADS)
                dvt_scr[cl, r] = cdv_ref[0, 0, rows_h, :].T.astype(BF16)

        def new_kv(c):
            r = rows_of(c)
            rk = pl.ds(past_len + c * ROW_CHUNK, ROW_CHUNK)
            hc = h_scr[r, :]
            zkv = _dot(hc, win(C_GK, 2 * GQA_KV_WIDTH))
            k = _rms_heads(zkv[:, 0:GQA_KV_WIDTH], k_gain)
            v_t = zkv[:, GQA_KV_WIDTH:].T
            if is_ctx:
                ogk_ref[s, 0, :, r] = k.T
                ogv_ref[s, 0, :, r] = v_t
            else:
                k = _rope(k, cos_ref[r, :], sin_ref[r, :])
            k_scr[rk, :] = k.astype(BF16)
            vt_scr[:, rk] = v_t.astype(BF16)
            yield
            zdk = _dot(hc, win(C_DK, DIFF_WIDTH))
            yield
            zdv = _dot(hc, win(C_DV, DIFF_WIDTH))
            yield
            for hd in range(DIFF_HEADS):
                cl = slice(hd * LANES, (hd + 1) * LANES)
                dk = zdk[:, cl]
                if is_ctx:
                    odk_ref[s, 0, cl, r] = dk.T
                    rows_h = pl.ds(c * ROW_CHUNK * DIFF_HEADS + hd, ROW_CHUNK, stride=DIFF_HEADS)
                    odv_ref[s, 0, rows_h, :] = zdv[:, cl]
                else:
                    dk = _rope(dk, cos_ref[r, :], sin_ref[r, :])
                dk_scr[rk, cl] = dk.astype(BF16)
                dvt_scr[cl, rk] = zdv[:, cl].T.astype(BF16)
                yield

        def new_q_piece(cq, g):
            start = cq * ROW_CHUNK
            rq = pl.ds(start if isinstance(cq, int) else pl.multiple_of(start, ROW_CHUNK), ROW_CHUNK)
            c0 = C_GQ if g == 0 else C_DQ + (g - 1) * 2 * LANES
            z = _dot(h_scr[rq, :], win(c0, 2 * LANES))
            for j in range(2):
                t = z[:, j * LANES:(j + 1) * LANES].T
                if g == 0:
                    gain = jnp.concatenate([q_gain_cols] * (ROW_CHUNK // LANES), axis=1)
                    halves = []
                    for hh in range(LANES // HEAD_DIM):
                        th = t[hh * HEAD_DIM:(hh + 1) * HEAD_DIM]
                        ms = jnp.sum(th * th, axis=0, keepdims=True) * (1.0 / HEAD_DIM)
                        halves.append(th * lax.rsqrt(ms + EPS) * gain)
                    t = jnp.concatenate(halves, axis=0)
                if not is_ctx:
                    half = HEAD_DIM // 4
                    blocks = [t[i * half:(i + 1) * half] for i in range(LANES // half)]
                    partner = jnp.concatenate(
                        [blocks[i + 1 - 2 * (i % 2)] for i in range(len(blocks))], axis=0)
                    t = t * cost_ref[cq] + partner * sint_ref[cq]
                t = (t * QK_SCALE).astype(BF16)
                if g == 0:
                    qt_scr[cq, j * LANES:(j + 1) * LANES, :] = t
                else:
                    hd = (g - 1) * 2 + j
                    dqt_scr[cq, hd * LANES:(hd + 1) * LANES, :] = t

        n_q_pieces = 1 + DIFF_HEADS // 2

        for c in range(n_chunks):
            prenorm_and_lru_in(c)
            yield
        lru_pieces = [(ct, d) for ct in range(LRU_WIDTH // LANES) for d in range(2)]
        n_cached = past_len // ROW_CHUNK
        for i in range(max(len(lru_pieces), n_chunks, n_cached)):
            if i < n_chunks:
                yield from new_kv(i)
            if i < len(lru_pieces):
                yield from lru_piece(*lru_pieces[i])
            if i < n_cached:
                cached_kv(i)
                yield
        for g in range(n_q_pieces):
            new_q_piece(0, g)
            yield
        yield _PHASE_END

        zero_half = jnp.zeros((HEAD_DIM, ROW_CHUNK), BF16)

        group = GQA_Q_HEADS // GQA_KV_HEADS
        n_pairs = GQA_KV_HEADS + DIFF_HEADS
        n_kc = Tk // KEY_CHUNK

        def pair_qt(cq, p):
            if p < GQA_KV_HEADS:
                heads = [qt_scr[cq, (p * group + i) * HEAD_DIM:(p * group + i + 1) * HEAD_DIM, :]
                         for i in range(group)]
                cols = [jnp.concatenate([q, zero_half] if p == 0 else [zero_half, q], axis=0)
                        for q in heads]
            else:
                hd = p - GQA_KV_HEADS
                q1 = dqt_scr[cq, hd * LANES:hd * LANES + HEAD_DIM, :]
                q2 = dqt_scr[cq, hd * LANES + HEAD_DIM:(hd + 1) * LANES, :]
                cols = [jnp.concatenate([q1, zero_half], axis=0),
                        jnp.concatenate([zero_half, q2], axis=0)]
            return jnp.concatenate(cols, axis=1)

        def attn_steps(c, next_q):
            start = c * ROW_CHUNK
            r = pl.ds(start if isinstance(c, int) else pl.multiple_of(start, ROW_CHUNK), ROW_CHUNK)
            hc = h_scr[r, :]

            def pair_keys(p, i):
                rk = slice(i * KEY_CHUNK, (i + 1) * KEY_CHUNK)
                if p < GQA_KV_HEADS:
                    return k_scr[rk, :]
                hd = p - GQA_KV_HEADS
                return dk_scr[rk, hd * LANES:(hd + 1) * LANES]

            def pair_values_t(p):
                if p < GQA_KV_HEADS:
                    return vt_scr[...]
                hd = p - GQA_KV_HEADS
                return dvt_scr[hd * LANES:(hd + 1) * LANES, :]

            def scores(p, qt, i, m):
                sc = _dot(pair_keys(p, i), qt)
                s_scr[p % 2, i * KEY_CHUNK:(i + 1) * KEY_CHUNK, :] = sc
                mi = jnp.max(sc, axis=0, keepdims=True)
                return mi if m is None else jnp.maximum(m, mi)

            def finish(p, o_t, gates):
                if p < GQA_KV_HEADS:
                    o = jnp.concatenate(
                        [o_t[p * HEAD_DIM:(p + 1) * HEAD_DIM, i * ROW_CHUNK:(i + 1) * ROW_CHUNK]
                         for i in range(group)], axis=0).T
                    col = M_GQA + p * LANES
                    gt = gates[0][:, p * LANES:(p + 1) * LANES]
                else:
                    hd = p - GQA_KV_HEADS
                    o = _rms((o_t[:, 0:ROW_CHUNK] - lam * o_t[:, ROW_CHUNK:]).T, sub_gain) * (1.0 - lam_i)
                    col = M_DIFF + hd * LANES
                    gt = gates[1][:, hd * LANES:(hd + 1) * LANES]
                mix_scr[r, col:col + LANES] = (o * gt).astype(BF16)

            qt_next = pair_qt(c, 0)
            m_next = None
            for i in range(n_kc):
                m_next = scores(0, qt_next, i, m_next)
            gates = (_silu(_dot(hc, win(C_GG, GQA_WIDTH))), _silu(_dot(hc, win(C_DG, DIFF_WIDTH))))
            yield
            for p in range(n_pairs):
                m = m_next
                m_next = None
                if p + 1 < n_pairs:
                    qt_next = pair_qt(c, p + 1)
                l = None
                for i in range(n_kc):
                    if p + 1 < n_pairs:
                        m_next = scores(p + 1, qt_next, i, m_next)
                    rk = slice(i * KEY_CHUNK, (i + 1) * KEY_CHUNK)
                    e = jnp.exp2(s_scr[p % 2, rk, :] - m)
                    li = jnp.sum(e, axis=0, keepdims=True)
                    l = li if l is None else l + li
                    e_scr[p % 2, rk, :] = e.astype(BF16)
                    yield
                if next_q and p % 2 == 0 and p // 2 < n_q_pieces:
                    new_q_piece(jnp.where(c + 1 == n_chunks, 0, c + 1), p // 2)
                finish(p, _dot(pair_values_t(p), e_scr[p % 2]) / l, gates)
                yield

        if n_chunks == 1:
            yield from attn_steps(0, next_q=False)
        else:
            def attn_chunk(c, carry):
                for _ in attn_steps(c, next_q=True):
                    pass
                return carry
            lax.fori_loop(0, n_chunks, attn_chunk, 0)
            yield

        for c in range(n_chunks):
            r = rows_of(c)
            halves = []
            for j in range(2):
                half_n = wout_ref.shape[-1] // 2
                halves.append(_dot(mix_scr[r, :], wout_ref[0, :, j * half_n:(j + 1) * half_n]))
                yield
            out = jnp.concatenate(halves, axis=1)
            y_ref[s, r, :] = y_ref[s, r, :] + _rms(out, post_gain)
            yield

    def phase_done(gen):
        return next(gen, _PHASE_END) is _PHASE_END

    gens = [sequence_steps(s) for s in range(n_seq)]
    while not phase_done(gens[0]):
        pass
    for s in range(n_seq):
        cur_done = False
        nxt_done = s + 1 >= n_seq
        while not (cur_done and nxt_done):
            if not cur_done:
                cur_done = phase_done(gens[s])
            if not nxt_done:
                nxt_done = phase_done(gens[s + 1])


def _rope_tables(seq_len):
    quarter = HEAD_DIM // 4
    inv = np.power(np.float32(ROPE_BASE), -np.arange(quarter, dtype=np.float32) / quarter)
    t = np.arange(seq_len)
    row = (t // GRID_W).astype(np.float32)[:, None] * inv[None]
    col = (t % GRID_W).astype(np.float32)[:, None] * inv[None]
    cos = np.concatenate([np.cos(row), np.cos(row), np.cos(col), np.cos(col)], axis=-1)
    sin = np.concatenate([-np.sin(row), np.sin(row), -np.sin(col), np.sin(col)], axis=-1)
    reps = LANES // HEAD_DIM
    return (np.tile(cos, (1, reps)).astype(np.float32), np.tile(sin, (1, reps)).astype(np.float32))


def _block_diag(w):
    nb, bw, _ = w.shape
    eye = jnp.eye(nb, dtype=w.dtype)
    return (eye[:, None, :, None] * w[:, :, None, :]).reshape(nb * bw, nb * bw)


def _layer_call(x, mod, weights, extras, *, is_ctx, n_seq, layer0, n_layers):
    batch, seq_len, d_model = x.shape
    depth = weights[2].shape[0]
    past_len = 0 if is_ctx else extras[0].shape[3]
    tk = past_len + seq_len
    n_chunks = seq_len // ROW_CHUNK
    grid = (batch // n_seq, n_layers)
    single = dict(pipeline_mode=pl.Buffered(1))
    w_mode = single if n_layers == 1 else {}

    def per_layer(a):
        nd = a.ndim
        if a.dtype != BF16:
            return pl.BlockSpec(a.shape, lambda b, l: (0,) * nd, **single)
        return pl.BlockSpec((1,) + a.shape[1:], lambda b, l: (l + layer0,) + (0,) * (nd - 1), **w_mode)

    y_spec = pl.BlockSpec((n_seq, seq_len, d_model), lambda b, l: (b, 0, 0))
    x_spec = y_spec
    mod_spec = pl.BlockSpec((1,) + mod.shape[1:], lambda b, l: (l + layer0, 0, 0))
    in_specs = [x_spec, mod_spec] + [per_layer(w) for w in weights]
    args = [x, mod] + list(weights)
    if not is_ctx:
        for a in extras[:5]:
            in_specs.append(
                pl.BlockSpec((1, 1) + a.shape[2:], lambda b, l: (b, l + layer0, 0, 0)))
        for a in extras[5:]:
            in_specs.append(pl.BlockSpec(a.shape, lambda b, l, nd=a.ndim: (0,) * nd, **single))
        args += list(extras)

    out_shape = [jax.ShapeDtypeStruct(x.shape, F32)]
    out_specs = [y_spec]
    if is_ctx:
        for rows_, cols_ in ((GQA_KV_WIDTH, seq_len), (GQA_KV_WIDTH, seq_len), (DIFF_WIDTH, seq_len),
                             (seq_len * DIFF_HEADS, DIFF_WIDTH // DIFF_HEADS)):
            out_shape.append(jax.ShapeDtypeStruct((batch, depth, rows_, cols_), F32))
            out_specs.append(pl.BlockSpec((n_seq, 1, rows_, cols_), lambda b, l: (b, l, 0, 0)))
        out_shape.append(jax.ShapeDtypeStruct((batch, depth, 2, LRU_WIDTH), F32))
        out_specs.append(pl.BlockSpec((n_seq, 1, 2, LRU_WIDTH), lambda b, l: (b, l, 0, 0)))

    scratch = [pltpu.VMEM((n_seq,) + shape, dtype) for shape, dtype in (
        ((seq_len, d_model), BF16),
        ((seq_len, D_MIX), BF16),
        ((seq_len, 2 * LRU_WIDTH), F32),
        ((n_chunks, GQA_WIDTH, ROW_CHUNK), BF16),
        ((tk, GQA_KV_WIDTH), BF16),
        ((GQA_KV_WIDTH, tk), BF16),
        ((n_chunks, DIFF_WIDTH, ROW_CHUNK), BF16),
        ((tk, DIFF_WIDTH), BF16),
        ((DIFF_WIDTH, tk), BF16),
        ((2, tk, 2 * ROW_CHUNK), F32),
        ((2, tk, 2 * ROW_CHUNK), BF16),
    )]
    kern = functools.partial(_layer_kernel, is_ctx=is_ctx, n_seq=n_seq, seq_len=seq_len,
                             past_len=past_len, depth=depth, layer0=layer0)
    return pl.pallas_call(
        kern,
        grid=grid,
        in_specs=in_specs,
        out_specs=out_specs,
        out_shape=out_shape,
        scratch_shapes=scratch,
        compiler_params=pltpu.CompilerParams(
            dimension_semantics=("arbitrary", "arbitrary"),
            vmem_limit_bytes=VMEM_LIMIT),
        name="ctx_pass" if is_ctx else f"denoise_layer{layer0}",
    )(*args)


def kernel(x_prompt, x_sample, cache_gqa_k, cache_gqa_v, cache_diff_k, cache_diff_v, state_lru, c, c_ctx, w_mod, b_mod, g_pre, g_post, w_in, w_out, lru_conv_w, lru_conv_b, lru_wa, lru_ba, lru_wx, lru_bx, lru_lambda, gqa_gq, gqa_gk, diff_lam, diff_gsub):
    depth, d_model, _ = w_in.shape
    dec_batch = x_sample.shape[0]

    cond = jnp.concatenate(
        [c_ctx[None, :], c, jnp.zeros((MOD_ROWS - 1 - dec_batch, d_model), F32)], axis=0)
    n_mod = w_mod.shape[-1]
    mod_tile = n_mod // 3
    mod = pl.pallas_call(
        _mod_kernel,
        grid=(depth, n_mod // mod_tile),
        in_specs=[pl.BlockSpec((MOD_ROWS, d_model), lambda l, j: (0, 0)),
                  pl.BlockSpec((1, d_model, mod_tile), lambda l, j: (l, 0, j)),
                  pl.BlockSpec((1, 1, mod_tile), lambda l, j: (l, 0, j))],
        out_specs=pl.BlockSpec((1, MOD_ROWS, mod_tile), lambda l, j: (l, 0, j)),
        out_shape=jax.ShapeDtypeStruct((depth, MOD_ROWS, n_mod), F32),
        name="adaln_mod",
    )(cond, w_mod, b_mod[:, None, :])

    w_in_p = w_in.astype(BF16)
    w_out_p = w_out.astype(BF16)
    bd = jax.vmap(jax.vmap(_block_diag))
    wa_d, wx_d = bd(lru_wa), bd(lru_wx)
    n_ct = LRU_WIDTH // LANES
    wg = jnp.stack([
        jnp.concatenate([m[:, d, ct * LANES:(ct + 1) * LANES, ct * LANES:(ct + 1) * LANES]
                         for d in range(2) for m in (wa_d, wx_d)], axis=-1)
        for ct in range(n_ct)], axis=1).astype(BF16)
    bg = jnp.stack([
        jnp.concatenate([m[:, d, ct * LANES:(ct + 1) * LANES]
                         for d in range(2) for m in (lru_ba, lru_bx)], axis=-1)
        for ct in range(n_ct)], axis=1)
    reps = LANES // HEAD_DIM
    weights = [g_pre, g_post, w_in_p, w_out_p, lru_conv_w, lru_conv_b, wg, bg, lru_lambda,
               jnp.broadcast_to(gqa_gq[:, :, None], (depth, HEAD_DIM, LANES)),
               jnp.tile(gqa_gk, (1, reps)), diff_lam, diff_gsub]

    y_prompt, gk, gv, dk, dv, st = _layer_call(x_prompt, mod, weights, None, is_ctx=True, n_seq=2,
                                               layer0=0, n_layers=depth)
    b, t = x_prompt.shape[:2]
    new_gqa_k = gk.reshape(b, depth, GQA_KV_HEADS, HEAD_DIM, t).transpose(0, 1, 4, 2, 3)
    new_gqa_v = gv.reshape(b, depth, GQA_KV_HEADS, HEAD_DIM, t).transpose(0, 1, 4, 2, 3)
    new_diff_k = dk.reshape(b, depth, DIFF_HEADS, 2, HEAD_DIM, t).transpose(0, 1, 5, 2, 3, 4)
    new_diff_v = dv.reshape(b, depth, t, DIFF_HEADS, 2 * HEAD_DIM)

    db, _, past = cache_gqa_k.shape[:3]
    cos, sin = _rope_tables(x_sample.shape[1])

    def feature_major_chunks(tab):
        return tab.T.reshape(LANES, -1, ROW_CHUNK).transpose(1, 0, 2)

    extras = [cache_gqa_k.transpose(0, 1, 3, 4, 2).reshape(db, depth, GQA_KV_WIDTH, past),
              cache_gqa_v.transpose(0, 1, 3, 4, 2).reshape(db, depth, GQA_KV_WIDTH, past),
              cache_diff_k.transpose(0, 1, 3, 4, 5, 2).reshape(db, depth, DIFF_WIDTH, past),
              cache_diff_v.reshape(db, depth, past * DIFF_HEADS, 2 * HEAD_DIM),
              state_lru, cos, sin, feature_major_chunks(cos), feature_major_chunks(sin)]
    y_sample = x_sample
    for lyr in range(depth):
        (y_sample,) = _layer_call(y_sample, mod, weights, extras, is_ctx=False, n_seq=1,
                                  layer0=lyr, n_layers=1)

    return (y_prompt, y_sample, new_gqa_k, new_gqa_v, new_diff_k, new_diff_v, st)
```

```python
import functools
import math

import jax
import jax.numpy as jnp
import numpy as np
from jax import lax
from jax.experimental import pallas as pl
from jax.experimental.pallas import tpu as pltpu

F32 = jnp.float32
BF16 = jnp.bfloat16

GRID_W = 64
HEAD_DIM = 64
EPS = 1e-6
ROPE_BASE = 10000.0
LRU_WIDTH = 256
LRU_BLOCKS = 4
LRU_C = 8.0
CONV_W = 4
CONV_LEFT = 2
GQA_Q_HEADS = 4
GQA_KV_HEADS = 2
GQA_WIDTH = GQA_Q_HEADS * HEAD_DIM
GQA_KV_WIDTH = GQA_KV_HEADS * HEAD_DIM
DIFF_HEADS = 4
DIFF_WIDTH = DIFF_HEADS * 2 * HEAD_DIM
LOG2E = math.log2(math.e)
QK_SCALE = HEAD_DIM ** -0.5 * LOG2E

C_LRU_X = 0
C_LRU_G = C_LRU_X + LRU_WIDTH
C_GQ = C_LRU_G + LRU_WIDTH
C_GK = C_GQ + GQA_WIDTH
C_GV = C_GK + GQA_KV_WIDTH
C_GG = C_GV + GQA_KV_WIDTH
C_DQ = C_GG + GQA_WIDTH
C_DK = C_DQ + DIFF_WIDTH
C_DV = C_DK + DIFF_WIDTH
C_DG = C_DV + DIFF_WIDTH
N_IN = C_DG + DIFF_WIDTH
M_LRU = 0
M_GQA = LRU_WIDTH
M_DIFF = LRU_WIDTH + GQA_WIDTH
D_MIX = LRU_WIDTH + GQA_WIDTH + DIFF_WIDTH

LANES = 128
SUBLANES = 8
ROW_CHUNK = 256
KEY_CHUNK = 256
MOD_ROWS = 16
VMEM_LIMIT = 58 * 1024 * 1024


def _lambda_init(layer):
    return 0.8 - 0.6 * math.exp(-0.3 * layer)


def _dot(a, b):
    return jnp.dot(a, b, preferred_element_type=F32)


def _silu(x):
    return x * jax.nn.sigmoid(x)


def _rms(x, g):
    ms = jnp.mean(x * x, axis=-1, keepdims=True)
    return x * lax.rsqrt(ms + EPS) * g


def _rms_heads(x, g2):
    lo = lax.broadcasted_iota(jnp.int32, x.shape, 1) < HEAD_DIM
    t = x * x
    s_lo = jnp.sum(jnp.where(lo, t, 0.0), axis=-1, keepdims=True)
    s_hi = jnp.sum(jnp.where(lo, 0.0, t), axis=-1, keepdims=True)
    ms = jnp.where(lo, s_lo, s_hi) * (1.0 / HEAD_DIM)
    return x * lax.rsqrt(ms + EPS) * g2


def _rope(x, cos, sin_signed):
    first = (lax.broadcasted_iota(jnp.int32, x.shape, 1) % 32) < 16
    partner = jnp.where(first, pltpu.roll(x, LANES - 16, 1), pltpu.roll(x, 16, 1))
    return x * cos + partner * sin_signed


def _shift_rows(x, k, fill, n_rows):
    rows = lax.broadcasted_iota(jnp.int32, x.shape, 0)
    if k > 0:
        return jnp.where(rows >= k, pltpu.roll(x, k, 0), fill)
    return jnp.where(rows < n_rows + k, pltpu.roll(x, n_rows + k, 0), fill)


def _linear_scan(a, x, n_rows, reverse):
    nb = n_rows // SUBLANES
    lanes = a.shape[-1]
    a3 = a.reshape(nb, SUBLANES, lanes)
    x3 = x.reshape(nb, SUBLANES, lanes)
    sub = lax.broadcasted_iota(jnp.int32, (nb, SUBLANES, lanes), 1)
    k = 1
    while k < SUBLANES:
        keep = (sub < SUBLANES - k) if reverse else (sub >= k)
        shift = SUBLANES - k if reverse else k
        x3 = a3 * jnp.where(keep, pltpu.roll(x3, shift, 1), 0.0) + x3
        a3 = a3 * jnp.where(keep, pltpu.roll(a3, shift, 1), 1.0)
        k *= 2
    blocks = [None] * nb
    carry = None
    for j in (range(nb - 1, -1, -1) if reverse else range(nb)):
        xb = x3[j]
        if carry is not None:
            xb = xb + a3[j] * carry
        blocks[j] = xb
        carry = xb[0:1] if reverse else xb[SUBLANES - 1:SUBLANES]
    return jnp.concatenate(blocks, axis=0)


_PHASE_END = "phase-end"


def _mod_kernel(cond_ref, w_ref, b_ref, o_ref):
    o_ref[0] = _dot(_silu(cond_ref[...]).astype(BF16), w_ref[0].astype(BF16)) + b_ref[0]


def _layer_kernel(*refs, is_ctx, n_seq, seq_len, past_len, depth, layer0):
    T = seq_len
    Tk = past_len + T
    n_chunks = T // ROW_CHUNK
    it = iter(refs)
    x_ref, mod_ref, gpre_ref, gpost_ref, win_ref, wout_ref = (next(it) for _ in range(6))
    convw_ref, convb_ref, wg_ref, bg_ref, lam_ref = (next(it) for _ in range(5))
    gqc_ref, gk_ref, dlam_ref, gsub_ref = (next(it) for _ in range(4))
    if not is_ctx:
        (ck_ref, cv_ref, cdk_ref, cdv_ref, st_ref, cos_ref, sin_ref, cost_ref, sint_ref) = (
            next(it) for _ in range(9))
    y_ref = next(it)
    if is_ctx:
        ogk_ref, ogv_ref, odk_ref, odv_ref, ost_ref = (next(it) for _ in range(5))
    scratch_refs = [next(it) for _ in range(11)]

    @pl.when(pl.program_id(1) == 0)
    def _():
        y_ref[...] = x_ref[...]

    layer = pl.program_id(1) + layer0

    lam_i = jnp.float32(_lambda_init(depth - 1))
    for lyr in range(depth - 2, -1, -1):
        lam_i = jnp.where(layer == lyr, jnp.float32(_lambda_init(lyr)), lam_i)

    def layer_row(ref):
        return ref[pl.ds(layer, 1), :]

    conv_w, conv_b = convw_ref[layer], layer_row(convb_ref)
    gate_bias, lam_all = bg_ref[layer], lam_ref[layer]
    q_gain_cols, k_gain, sub_gain = gqc_ref[layer], layer_row(gk_ref), layer_row(gsub_ref)
    dl = dlam_ref[layer]

    d_model = gpre_ref.shape[-1]
    mod_row = mod_ref[0, 0:1, :] if is_ctx else mod_ref[0, pl.ds(pl.program_id(0) + 1, 1), :]
    shift = mod_row[:, 0:d_model]
    pre_gain = layer_row(gpre_ref) * (1.0 + mod_row[:, d_model:2 * d_model])
    post_gain = layer_row(gpost_ref) * mod_row[:, 2 * d_model:3 * d_model]
    lam = (jnp.exp(jnp.sum(dl[0:1] * dl[1:2], axis=-1, keepdims=True))
           - jnp.exp(jnp.sum(dl[2:3] * dl[3:4], axis=-1, keepdims=True)) + lam_i)

    def win(c0, n):
        return win_ref[0, :, c0:c0 + n]

    def rows_of(c):
        return pl.ds(c * ROW_CHUNK, ROW_CHUNK)

    def sequence_steps(s):
        (h_scr, mix_scr, zx_scr, qt_scr, k_scr, vt_scr, dqt_scr, dk_scr, dvt_scr, s_scr, e_scr) = (
            ref.at[s] for ref in scratch_refs)

        def prenorm_and_lru_in(c):
            r = rows_of(c)
            hn = _rms(y_ref[s, r, :], pre_gain) + shift
            hc = hn.astype(BF16)
            h_scr[r, :] = hc
            zxg = _dot(hc, win(C_LRU_X, 2 * LRU_WIDTH))
            zx_scr[r, 0:LRU_WIDTH] = zxg[:, 0:LRU_WIDTH]
            zx_scr[r, LRU_WIDTH:] = _silu(zxg[:, LRU_WIDTH:])

        row_id = lax.broadcasted_iota(jnp.int32, (T, LANES), 0)
        lru_state = {}

        def lru_piece(ct, d):
            cl = slice(ct * LANES, (ct + 1) * LANES)
            reverse = d == 1
            if d == 0:
                zx = zx_scr[:, cl]
                u = conv_b[:, cl] + zx * conv_w[CONV_LEFT:CONV_LEFT + 1, cl]
                for j in range(CONV_W):
                    if j != CONV_LEFT:
                        u = u + _shift_rows(zx, CONV_LEFT - j, 0.0, T) * conv_w[j:j + 1, cl]
                lru_state[ct] = (u, u.astype(BF16), None)
                yield
            u, u_bf, y_prev = lru_state[ct]
            gates = _dot(u_bf, wg_ref[0, ct, :, d * 2 * LANES:(d + 1) * 2 * LANES])
            gates = gates + gate_bias[ct:ct + 1, d * 2 * LANES:(d + 1) * 2 * LANES]
            r_g = jax.nn.sigmoid(gates[:, 0:LANES])
            i_g = jax.nn.sigmoid(gates[:, LANES:])
            yield
            lam_d = lam_all[d:d + 1, cl]
            sp = jnp.maximum(-lam_d, 0.0) + jnp.log1p(jnp.exp(-jnp.abs(lam_d)))
            a = jnp.exp2(r_g * (sp * (-LRU_C * LOG2E)))
            one_m_a2 = 1.0 - a * a
            root = jnp.where(one_m_a2 > 0.0, one_m_a2 * lax.rsqrt(one_m_a2), 0.0)
            inp = root * (i_g * u)
            if not is_ctx:
                h0 = st_ref[0, 0, d:d + 1, cl]
                first_row = (T - 1) if reverse else 0
                inp = inp + jnp.where(row_id == first_row, a * h0, 0.0)
            yield
            y_d = _linear_scan(a, inp, T, reverse)
            yield
            if is_ctx:
                last_row = 0 if reverse else T - 1
                ost_ref[s, 0, d:d + 1, cl] = y_d[last_row:last_row + 1, :]
            if d == 0:
                lru_state[ct] = (u, u_bf, y_d)
            else:
                lru_gate = zx_scr[:, LRU_WIDTH + ct * LANES:LRU_WIDTH + (ct + 1) * LANES]
                mix_scr[:, M_LRU + ct * LANES:M_LRU + (ct + 1) * LANES] = (
                    (y_prev + y_d) * lru_gate).astype(BF16)
            yield

        def cached_kv(c):
            r = rows_of(c)
            k_scr[r, :] = ck_ref[0, 0, :, r].T.astype(BF16)
            vt_scr[:, r] = cv_ref[0, 0, :, r].astype(BF16)
            for hd in range(DIFF_HEADS):
                cl = slice(hd * LANES, (hd + 1) * LANES)
                dk_scr[r, cl] = cdk_ref[0, 0, cl, r].T.astype(BF16)
                rows_h = pl.ds(c * ROW_CHUNK * DIFF_HEADS + hd, ROW_CHUNK, stride=DIFF_HEADS)
                dvt_scr[cl, r] = cdv_ref[0, 0, rows_h, :].T.astype(BF16)

        def new_kv(c):
            r = rows_of(c)
            rk = pl.ds(past_len + c * ROW_CHUNK, ROW_CHUNK)
            hc = h_scr[r, :]
            zkv = _dot(hc, win(C_GK, 2 * GQA_KV_WIDTH))
            k = _rms_heads(zkv[:, 0:GQA_KV_WIDTH], k_gain)
            v_t = zkv[:, GQA_KV_WIDTH:].T
            if is_ctx:
                ogk_ref[s, 0, :, r] = k.T
                ogv_ref[s, 0, :, r] = v_t
            else:
                k = _rope(k, cos_ref[r, :], sin_ref[r, :])
            k_scr[rk, :] = k.astype(BF16)
            vt_scr[:, rk] = v_t.astype(BF16)
            yield
            zdk = _dot(hc, win(C_DK, DIFF_WIDTH))
            for hd in range(DIFF_HEADS):
                cl = slice(hd * LANES, (hd + 1) * LANES)
                dk = zdk[:, cl]
                if is_ctx:
                    odk_ref[s, 0, cl, r] = dk.T
                else:
                    dk = _rope(dk, cos_ref[r, :], sin_ref[r, :])
                dk_scr[rk, cl] = dk.astype(BF16)
            yield
            zdv = _dot(hc, win(C_DV, DIFF_WIDTH))
            for hd in range(DIFF_HEADS):
                cl = slice(hd * LANES, (hd + 1) * LANES)
                if is_ctx:
                    rows_h = pl.ds(c * ROW_CHUNK * DIFF_HEADS + hd, ROW_CHUNK, stride=DIFF_HEADS)
                    odv_ref[s, 0, rows_h, :] = zdv[:, cl]
                dvt_scr[cl, rk] = zdv[:, cl].T.astype(BF16)
            yield

        def new_q_piece(cq, g):
            start = cq * ROW_CHUNK
            rq = pl.ds(start if isinstance(cq, int) else pl.multiple_of(start, ROW_CHUNK), ROW_CHUNK)
            c0 = C_GQ if g == 0 else C_DQ + (g - 1) * 2 * LANES
            z = _dot(h_scr[rq, :], win(c0, 2 * LANES))
            for j in range(2):
                t = z[:, j * LANES:(j + 1) * LANES].T
                if g == 0:
                    gain = jnp.concatenate([q_gain_cols] * (ROW_CHUNK // LANES), axis=1)
                    halves = []
                    for hh in range(LANES // HEAD_DIM):
                        th = t[hh * HEAD_DIM:(hh + 1) * HEAD_DIM]
                        ms = jnp.sum(th * th, axis=0, keepdims=True) * (1.0 / HEAD_DIM)
                        halves.append(th * lax.rsqrt(ms + EPS) * gain)
                    t = jnp.concatenate(halves, axis=0)
                if not is_ctx:
                    half = HEAD_DIM // 4
                    blocks = [t[i * half:(i + 1) * half] for i in range(LANES // half)]
                    partner = jnp.concatenate(
                        [blocks[i + 1 - 2 * (i % 2)] for i in range(len(blocks))], axis=0)
                    t = t * cost_ref[cq] + partner * sint_ref[cq]
                t = (t * QK_SCALE).astype(BF16)
                if g == 0:
                    qt_scr[cq, j * LANES:(j + 1) * LANES, :] = t
                else:
                    hd = (g - 1) * 2 + j
                    dqt_scr[cq, hd * LANES:(hd + 1) * LANES, :] = t

        n_q_pieces = 1 + DIFF_HEADS // 2

        for c in range(n_chunks):
            prenorm_and_lru_in(c)
            yield
        lru_pieces = [(ct, d) for ct in range(LRU_WIDTH // LANES) for d in range(2)]
        n_cached = past_len // ROW_CHUNK
        for i in range(max(len(lru_pieces), n_chunks, n_cached)):
            if i < n_chunks:
                yield from new_kv(i)
            if i < len(lru_pieces):
                yield from lru_piece(*lru_pieces[i])
            if i < n_cached:
                cached_kv(i)
                yield
        for g in range(n_q_pieces):
            new_q_piece(0, g)
            yield
        yield _PHASE_END

        zero_half = jnp.zeros((HEAD_DIM, ROW_CHUNK), BF16)

        group = GQA_Q_HEADS // GQA_KV_HEADS
        n_pairs = GQA_KV_HEADS + DIFF_HEADS
        n_kc = Tk // KEY_CHUNK

        def pair_qt(cq, p):
            if p < GQA_KV_HEADS:
                heads = [qt_scr[cq, (p * group + i) * HEAD_DIM:(p * group + i + 1) * HEAD_DIM, :]
                         for i in range(group)]
                cols = [jnp.concatenate([q, zero_half] if p == 0 else [zero_half, q], axis=0)
                        for q in heads]
            else:
                hd = p - GQA_KV_HEADS
                q1 = dqt_scr[cq, hd * LANES:hd * LANES + HEAD_DIM, :]
                q2 = dqt_scr[cq, hd * LANES + HEAD_DIM:(hd + 1) * LANES, :]
                cols = [jnp.concatenate([q1, zero_half], axis=0),
                        jnp.concatenate([zero_half, q2], axis=0)]
            return jnp.concatenate(cols, axis=1)

        def attn_steps(c, next_q):
            start = c * ROW_CHUNK
            r = pl.ds(start if isinstance(c, int) else pl.multiple_of(start, ROW_CHUNK), ROW_CHUNK)
            hc = h_scr[r, :]

            def pair_keys(p, i):
                rk = slice(i * KEY_CHUNK, (i + 1) * KEY_CHUNK)
                if p < GQA_KV_HEADS:
                    return k_scr[rk, :]
                hd = p - GQA_KV_HEADS
                return dk_scr[rk, hd * LANES:(hd + 1) * LANES]

            def pair_values_t(p):
                if p < GQA_KV_HEADS:
                    return vt_scr[...]
                hd = p - GQA_KV_HEADS
                return dvt_scr[hd * LANES:(hd + 1) * LANES, :]

            def scores(p, qt, i, m):
                sc = _dot(pair_keys(p, i), qt)
                s_scr[p % 2, i * KEY_CHUNK:(i + 1) * KEY_CHUNK, :] = sc
                mi = jnp.max(sc, axis=0, keepdims=True)
                return mi if m is None else jnp.maximum(m, mi)

            def finish(p, o_t, gate_tile):
                if p < GQA_KV_HEADS:
                    o = jnp.concatenate(
                        [o_t[p * HEAD_DIM:(p + 1) * HEAD_DIM, i * ROW_CHUNK:(i + 1) * ROW_CHUNK]
                         for i in range(group)], axis=0).T
                    col = M_GQA + p * LANES
                else:
                    hd = p - GQA_KV_HEADS
                    o = _rms((o_t[:, 0:ROW_CHUNK] - lam * o_t[:, ROW_CHUNK:]).T, sub_gain) * (1.0 - lam_i)
                    col = M_DIFF + hd * LANES
                mix_scr[r, col:col + LANES] = (o * gate_tile).astype(BF16)

            qt_next = pair_qt(c, 0)
            m_next = None
            for i in range(n_kc):
                m_next = scores(0, qt_next, i, m_next)
            def gate_tile_pair(j):
                c0 = C_GG if j == 0 else C_DG + (j - 1) * 2 * LANES
                return _silu(_dot(hc, win(c0, 2 * LANES)))

            gates_early = n_chunks > 1
            gates = [gate_tile_pair(j) for j in range(n_pairs // 2)] if gates_early else None
            yield
            for p in range(n_pairs):
                m = m_next
                m_next = None
                if p + 1 < n_pairs:
                    qt_next = pair_qt(c, p + 1)
                l = None
                for i in range(n_kc):
                    if p + 1 < n_pairs:
                        m_next = scores(p + 1, qt_next, i, m_next)
                    rk = slice(i * KEY_CHUNK, (i + 1) * KEY_CHUNK)
                    e = jnp.exp2(s_scr[p % 2, rk, :] - m)
                    li = jnp.sum(e, axis=0, keepdims=True)
                    l = li if l is None else l + li
                    e_scr[p % 2, rk, :] = e.astype(BF16)
                    yield
                if next_q and p % 2 == 0 and p // 2 < n_q_pieces:
                    new_q_piece(jnp.where(c + 1 == n_chunks, 0, c + 1), p // 2)
                if p % 2 == 0:
                    gate_pair = gates[p // 2] if gates_early else gate_tile_pair(p // 2)
                finish(p, _dot(pair_values_t(p), e_scr[p % 2]) / l,
                       gate_pair[:, (p % 2) * LANES:(p % 2 + 1) * LANES])
                yield

        if n_chunks == 1:
            yield from attn_steps(0, next_q=False)
        else:
            def attn_chunk(c, carry):
                for _ in attn_steps(c, next_q=True):
                    pass
                return carry
            lax.fori_loop(0, n_chunks, attn_chunk, 0)
            yield

        for c in range(n_chunks):
            r = rows_of(c)
            halves = []
            for j in range(2):
                half_n = wout_ref.shape[-1] // 2
                halves.append(_dot(mix_scr[r, :], wout_ref[0, :, j * half_n:(j + 1) * half_n]))
                yield
            out = jnp.concatenate(halves, axis=1)
            y_ref[s, r, :] = y_ref[s, r, :] + _rms(out, post_gain)
            yield

    def phase_done(gen):
        return next(gen, _PHASE_END) is _PHASE_END

    gens = [sequence_steps(s) for s in range(n_seq)]
    while not phase_done(gens[0]):
        pass
    for s in range(n_seq):
        cur_done = False
        nxt_done = s + 1 >= n_seq
        while not (cur_done and nxt_done):
            if not cur_done:
                cur_done = phase_done(gens[s])
            if not nxt_done:
                nxt_done = phase_done(gens[s + 1])


def _rope_tables(seq_len):
    quarter = HEAD_DIM // 4
    inv = np.power(np.float32(ROPE_BASE), -np.arange(quarter, dtype=np.float32) / quarter)
    t = np.arange(seq_len)
    row = (t // GRID_W).astype(np.float32)[:, None] * inv[None]
    col = (t % GRID_W).astype(np.float32)[:, None] * inv[None]
    cos = np.concatenate([np.cos(row), np.cos(row), np.cos(col), np.cos(col)], axis=-1)
    sin = np.concatenate([-np.sin(row), np.sin(row), -np.sin(col), np.sin(col)], axis=-1)
    reps = LANES // HEAD_DIM
    return (np.tile(cos, (1, reps)).astype(np.float32), np.tile(sin, (1, reps)).astype(np.float32))


def _block_diag(w):
    nb, bw, _ = w.shape
    eye = jnp.eye(nb, dtype=w.dtype)
    return (eye[:, None, :, None] * w[:, :, None, :]).reshape(nb * bw, nb * bw)


def _layer_call(x, mod, weights, extras, *, is_ctx, n_seq, layer0, n_layers):
    batch, seq_len, d_model = x.shape
    depth = weights[2].shape[0]
    past_len = 0 if is_ctx else extras[0].shape[3]
    tk = past_len + seq_len
    n_chunks = seq_len // ROW_CHUNK
    grid = (batch // n_seq, n_layers)
    single = dict(pipeline_mode=pl.Buffered(1))
    w_mode = single if n_layers == 1 else {}

    def per_layer(a):
        nd = a.ndim
        if a.dtype != BF16:
            return pl.BlockSpec(a.shape, lambda b, l: (0,) * nd, **single)
        return pl.BlockSpec((1,) + a.shape[1:], lambda b, l: (l + layer0,) + (0,) * (nd - 1), **w_mode)

    y_spec = pl.BlockSpec((n_seq, seq_len, d_model), lambda b, l: (b, 0, 0))
    x_spec = y_spec
    mod_spec = pl.BlockSpec((1,) + mod.shape[1:], lambda b, l: (l + layer0, 0, 0))
    in_specs = [x_spec, mod_spec] + [per_layer(w) for w in weights]
    args = [x, mod] + list(weights)
    if not is_ctx:
        for a in extras[:5]:
            in_specs.append(
                pl.BlockSpec((1, 1) + a.shape[2:], lambda b, l: (b, l + layer0, 0, 0)))
        for a in extras[5:]:
            in_specs.append(pl.BlockSpec(a.shape, lambda b, l, nd=a.ndim: (0,) * nd, **single))
        args += list(extras)

    out_shape = [jax.ShapeDtypeStruct(x.shape, F32)]
    out_specs = [y_spec]
    if is_ctx:
        for rows_, cols_ in ((GQA_KV_WIDTH, seq_len), (GQA_KV_WIDTH, seq_len), (DIFF_WIDTH, seq_len),
                             (seq_len * DIFF_HEADS, DIFF_WIDTH // DIFF_HEADS)):
            out_shape.append(jax.ShapeDtypeStruct((batch, depth, rows_, cols_), F32))
            out_specs.append(pl.BlockSpec((n_seq, 1, rows_, cols_), lambda b, l: (b, l, 0, 0)))
        out_shape.append(jax.ShapeDtypeStruct((batch, depth, 2, LRU_WIDTH), F32))
        out_specs.append(pl.BlockSpec((n_seq, 1, 2, LRU_WIDTH), lambda b, l: (b, l, 0, 0)))

    scratch = [pltpu.VMEM((n_seq,) + shape, dtype) for shape, dtype in (
        ((seq_len, d_model), BF16),
        ((seq_len, D_MIX), BF16),
        ((seq_len, 2 * LRU_WIDTH), F32),
        ((n_chunks, GQA_WIDTH, ROW_CHUNK), BF16),
        ((tk, GQA_KV_WIDTH), BF16),
        ((GQA_KV_WIDTH, tk), BF16),
        ((n_chunks, DIFF_WIDTH, ROW_CHUNK), BF16),
        ((tk, DIFF_WIDTH), BF16),
        ((DIFF_WIDTH, tk), BF16),
        ((2, tk, 2 * ROW_CHUNK), F32),
        ((2, tk, 2 * ROW_CHUNK), BF16),
    )]
    kern = functools.partial(_layer_kernel, is_ctx=is_ctx, n_seq=n_seq, seq_len=seq_len,
                             past_len=past_len, depth=depth, layer0=layer0)
    return pl.pallas_call(
        kern,
        grid=grid,
        in_specs=in_specs,
        out_specs=out_specs,
        out_shape=out_shape,
        scratch_shapes=scratch,
        compiler_params=pltpu.CompilerParams(
            dimension_semantics=("arbitrary", "arbitrary"),
            vmem_limit_bytes=VMEM_LIMIT),
        name="ctx_pass" if is_ctx else f"denoise_layer{layer0}",
    )(*args)


def kernel(x_prompt, x_sample, cache_gqa_k, cache_gqa_v, cache_diff_k, cache_diff_v, state_lru, c, c_ctx, w_mod, b_mod, g_pre, g_post, w_in, w_out, lru_conv_w, lru_conv_b, lru_wa, lru_ba, lru_wx, lru_bx, lru_lambda, gqa_gq, gqa_gk, diff_lam, diff_gsub):
    depth, d_model, _ = w_in.shape
    dec_batch = x_sample.shape[0]

    cond = jnp.concatenate(
        [c_ctx[None, :], c, jnp.zeros((MOD_ROWS - 1 - dec_batch, d_model), F32)], axis=0)
    n_mod = w_mod.shape[-1]
    mod_tile = n_mod // 3
    mod = pl.pallas_call(
        _mod_kernel,
        grid=(depth, n_mod // mod_tile),
        in_specs=[pl.BlockSpec((MOD_ROWS, d_model), lambda l, j: (0, 0)),
                  pl.BlockSpec((1, d_model, mod_tile), lambda l, j: (l, 0, j)),
                  pl.BlockSpec((1, 1, mod_tile), lambda l, j: (l, 0, j))],
        out_specs=pl.BlockSpec((1, MOD_ROWS, mod_tile), lambda l, j: (l, 0, j)),
        out_shape=jax.ShapeDtypeStruct((depth, MOD_ROWS, n_mod), F32),
        name="adaln_mod",
    )(cond, w_mod, b_mod[:, None, :])

    w_in_p = w_in.astype(BF16)
    w_out_p = w_out.astype(BF16)
    bd = jax.vmap(jax.vmap(_block_diag))
    wa_d, wx_d = bd(lru_wa), bd(lru_wx)
    n_ct = LRU_WIDTH // LANES
    wg = jnp.stack([
        jnp.concatenate([m[:, d, ct * LANES:(ct + 1) * LANES, ct * LANES:(ct + 1) * LANES]
                         for d in range(2) for m in (wa_d, wx_d)], axis=-1)
        for ct in range(n_ct)], axis=1).astype(BF16)
    bg = jnp.stack([
        jnp.concatenate([m[:, d, ct * LANES:(ct + 1) * LANES]
                         for d in range(2) for m in (lru_ba, lru_bx)], axis=-1)
        for ct in range(n_ct)], axis=1)
    reps = LANES // HEAD_DIM
    weights = [g_pre, g_post, w_in_p, w_out_p, lru_conv_w, lru_conv_b, wg, bg, lru_lambda,
               jnp.broadcast_to(gqa_gq[:, :, None], (depth, HEAD_DIM, LANES)),
               jnp.tile(gqa_gk, (1, reps)), diff_lam, diff_gsub]

    y_prompt, gk, gv, dk, dv, st = _layer_call(x_prompt, mod, weights, None, is_ctx=True, n_seq=2,
                                               layer0=0, n_layers=depth)
    b, t = x_prompt.shape[:2]
    new_gqa_k = gk.reshape(b, depth, GQA_KV_HEADS, HEAD_DIM, t).transpose(0, 1, 4, 2, 3)
    new_gqa_v = gv.reshape(b, depth, GQA_KV_HEADS, HEAD_DIM, t).transpose(0, 1, 4, 2, 3)
    new_diff_k = dk.reshape(b, depth, DIFF_HEADS, 2, HEAD_DIM, t).transpose(0, 1, 5, 2, 3, 4)
    new_diff_v = dv.reshape(b, depth, t, DIFF_HEADS, 2 * HEAD_DIM)

    db, _, past = cache_gqa_k.shape[:3]
    cos, sin = _rope_tables(x_sample.shape[1])

    def feature_major_chunks(tab):
        return tab.T.reshape(LANES, -1, ROW_CHUNK).transpose(1, 0, 2)

    extras = [cache_gqa_k.transpose(0, 1, 3, 4, 2).reshape(db, depth, GQA_KV_WIDTH, past),
              cache_gqa_v.transpose(0, 1, 3, 4, 2).reshape(db, depth, GQA_KV_WIDTH, past),
              cache_diff_k.transpose(0, 1, 3, 4, 5, 2).reshape(db, depth, DIFF_WIDTH, past),
              cache_diff_v.reshape(db, depth, past * DIFF_HEADS, 2 * HEAD_DIM),
              state_lru, cos, sin, feature_major_chunks(cos), feature_major_chunks(sin)]
    y_sample = x_sample
    for lyr in range(depth):
        (y_sample,) = _layer_call(y_sample, mod, weights, extras, is_ctx=False, n_seq=1,
                                  layer0=lyr, n_layers=1)

    return (y_prompt, y_sample, new_gqa_k, new_gqa_v, new_diff_k, new_diff_v, st)
```

```python
import functools
import math

import jax
import jax.numpy as jnp
import numpy as np
from jax import lax
from jax.experimental import pallas as pl
from jax.experimental.pallas import tpu as pltpu

F32 = jnp.float32
BF16 = jnp.bfloat16

GRID_W = 64
HEAD_DIM = 64
EPS = 1e-6
ROPE_BASE = 10000.0
LRU_WIDTH = 256
LRU_BLOCKS = 4
LRU_C = 8.0
CONV_W = 4
CONV_LEFT = 2
GQA_Q_HEADS = 4
GQA_KV_HEADS = 2
GQA_WIDTH = GQA_Q_HEADS * HEAD_DIM
GQA_KV_WIDTH = GQA_KV_HEADS * HEAD_DIM
DIFF_HEADS = 4
DIFF_WIDTH = DIFF_HEADS * 2 * HEAD_DIM
LOG2E = math.log2(math.e)
QK_SCALE = HEAD_DIM ** -0.5 * LOG2E

C_LRU_X = 0
C_LRU_G = C_LRU_X + LRU_WIDTH
C_GQ = C_LRU_G + LRU_WIDTH
C_GK = C_GQ + GQA_WIDTH
C_GV = C_GK + GQA_KV_WIDTH
C_GG = C_GV + GQA_KV_WIDTH
C_DQ = C_GG + GQA_WIDTH
C_DK = C_DQ + DIFF_WIDTH
C_DV = C_DK + DIFF_WIDTH
C_DG = C_DV + DIFF_WIDTH
N_IN = C_DG + DIFF_WIDTH
M_LRU = 0
M_GQA = LRU_WIDTH
M_DIFF = LRU_WIDTH + GQA_WIDTH
D_MIX = LRU_WIDTH + GQA_WIDTH + DIFF_WIDTH

LANES = 128
SUBLANES = 8
ROW_CHUNK = 256
KEY_CHUNK = 256
MOD_ROWS = 16
VMEM_LIMIT = 58 * 1024 * 1024


def _lambda_init(layer):
    return 0.8 - 0.6 * math.exp(-0.3 * layer)


def _dot(a, b):
    return jnp.dot(a, b, preferred_element_type=F32)


def _silu(x):
    return x * jax.nn.sigmoid(x)


def _rms(x, g):
    ms = jnp.mean(x * x, axis=-1, keepdims=True)
    return x * lax.rsqrt(ms + EPS) * g


def _rms_heads(x, g2):
    lo = lax.broadcasted_iota(jnp.int32, x.shape, 1) < HEAD_DIM
    t = x * x
    s_lo = jnp.sum(jnp.where(lo, t, 0.0), axis=-1, keepdims=True)
    s_hi = jnp.sum(jnp.where(lo, 0.0, t), axis=-1, keepdims=True)
    ms = jnp.where(lo, s_lo, s_hi) * (1.0 / HEAD_DIM)
    return x * lax.rsqrt(ms + EPS) * g2


def _rope(x, cos, sin_signed):
    first = (lax.broadcasted_iota(jnp.int32, x.shape, 1) % 32) < 16
    partner = jnp.where(first, pltpu.roll(x, LANES - 16, 1), pltpu.roll(x, 16, 1))
    return x * cos + partner * sin_signed


def _shift_rows(x, k, fill, n_rows):
    rows = lax.broadcasted_iota(jnp.int32, x.shape, 0)
    if k > 0:
        return jnp.where(rows >= k, pltpu.roll(x, k, 0), fill)
    return jnp.where(rows < n_rows + k, pltpu.roll(x, n_rows + k, 0), fill)


def _linear_scan(a, x, n_rows, reverse):
    nb = n_rows // SUBLANES
    lanes = a.shape[-1]
    a3 = a.reshape(nb, SUBLANES, lanes)
    x3 = x.reshape(nb, SUBLANES, lanes)
    sub = lax.broadcasted_iota(jnp.int32, (nb, SUBLANES, lanes), 1)
    k = 1
    while k < SUBLANES:
        keep = (sub < SUBLANES - k) if reverse else (sub >= k)
        shift = SUBLANES - k if reverse else k
        x3 = a3 * jnp.where(keep, pltpu.roll(x3, shift, 1), 0.0) + x3
        a3 = a3 * jnp.where(keep, pltpu.roll(a3, shift, 1), 1.0)
        k *= 2
    blocks = [None] * nb
    carry = None
    for j in (range(nb - 1, -1, -1) if reverse else range(nb)):
        xb = x3[j]
        if carry is not None:
            xb = xb + a3[j] * carry
        blocks[j] = xb
        carry = xb[0:1] if reverse else xb[SUBLANES - 1:SUBLANES]
    return jnp.concatenate(blocks, axis=0)


_PHASE_END = "phase-end"


def _mod_kernel(cond_ref, w_ref, b_ref, o_ref):
    o_ref[0] = _dot(_silu(cond_ref[...]).astype(BF16), w_ref[0].astype(BF16)) + b_ref[0]


def _layer_kernel(*refs, is_ctx, n_seq, seq_len, past_len, depth, layer0):
    T = seq_len
    Tk = past_len + T
    n_chunks = T // ROW_CHUNK
    it = iter(refs)
    x_ref, mod_ref, gpre_ref, gpost_ref, win_ref, wout_ref = (next(it) for _ in range(6))
    convw_ref, convb_ref, wg_ref, bg_ref, lam_ref = (next(it) for _ in range(5))
    gqc_ref, gk_ref, dlam_ref, gsub_ref = (next(it) for _ in range(4))
    if not is_ctx:
        (ck_ref, cv_ref, cdk_ref, cdv_ref, st_ref, cos_ref, sin_ref, cost_ref, sint_ref) = (
            next(it) for _ in range(9))
    y_ref = next(it)
    if is_ctx:
        ogk_ref, ogv_ref, odk_ref, odv_ref, ost_ref = (next(it) for _ in range(5))
    scratch_refs = [next(it) for _ in range(11)]

    @pl.when(pl.program_id(1) == 0)
    def _():
        y_ref[...] = x_ref[...]

    layer = pl.program_id(1) + layer0

    lam_i = jnp.float32(_lambda_init(depth - 1))
    for lyr in range(depth - 2, -1, -1):
        lam_i = jnp.where(layer == lyr, jnp.float32(_lambda_init(lyr)), lam_i)

    def layer_row(ref):
        return ref[pl.ds(layer, 1), :]

    conv_w, conv_b = convw_ref[layer], layer_row(convb_ref)
    gate_bias, lam_all = bg_ref[layer], lam_ref[layer]
    q_gain_cols, k_gain, sub_gain = gqc_ref[layer], layer_row(gk_ref), layer_row(gsub_ref)
    dl = dlam_ref[layer]

    d_model = gpre_ref.shape[-1]
    mod_row = mod_ref[0, 0:1, :] if is_ctx else mod_ref[0, pl.ds(pl.program_id(0) + 1, 1), :]
    shift = mod_row[:, 0:d_model]
    pre_gain = layer_row(gpre_ref) * (1.0 + mod_row[:, d_model:2 * d_model])
    post_gain = layer_row(gpost_ref) * mod_row[:, 2 * d_model:3 * d_model]
    lam = (jnp.exp(jnp.sum(dl[0:1] * dl[1:2], axis=-1, keepdims=True))
           - jnp.exp(jnp.sum(dl[2:3] * dl[3:4], axis=-1, keepdims=True)) + lam_i)

    def win(c0, n):
        return win_ref[0, :, c0:c0 + n]

    def rows_of(c):
        return pl.ds(c * ROW_CHUNK, ROW_CHUNK)

    def sequence_steps(s):
        (h_scr, mix_scr, zx_scr, qt_scr, k_scr, vt_scr, dqt_scr, dk_scr, dvt_scr, s_scr, e_scr) = (
            ref.at[s] for ref in scratch_refs)

        def prenorm_and_lru_in(c):
            r = rows_of(c)
            hn = _rms(y_ref[s, r, :], pre_gain) + shift
            hc = hn.astype(BF16)
            h_scr[r, :] = hc
            zxg = _dot(hc, win(C_LRU_X, 2 * LRU_WIDTH))
            zx_scr[r, 0:LRU_WIDTH] = zxg[:, 0:LRU_WIDTH]
            zx_scr[r, LRU_WIDTH:] = _silu(zxg[:, LRU_WIDTH:])

        row_id = lax.broadcasted_iota(jnp.int32, (T, LANES), 0)
        lru_state = {}

        def lru_piece(ct, d):
            cl = slice(ct * LANES, (ct + 1) * LANES)
            reverse = d == 1
            if d == 0:
                zx = zx_scr[:, cl]
                u = conv_b[:, cl] + zx * conv_w[CONV_LEFT:CONV_LEFT + 1, cl]
                for j in range(CONV_W):
                    if j != CONV_LEFT:
                        u = u + _shift_rows(zx, CONV_LEFT - j, 0.0, T) * conv_w[j:j + 1, cl]
                lru_state[ct] = (u, u.astype(BF16), None)
                yield
            u, u_bf, y_prev = lru_state[ct]
            gates = _dot(u_bf, wg_ref[0, ct, :, d * 2 * LANES:(d + 1) * 2 * LANES])
            gates = gates + gate_bias[ct:ct + 1, d * 2 * LANES:(d + 1) * 2 * LANES]
            r_g = jax.nn.sigmoid(gates[:, 0:LANES])
            i_g = jax.nn.sigmoid(gates[:, LANES:])
            yield
            lam_d = lam_all[d:d + 1, cl]
            sp = jnp.maximum(-lam_d, 0.0) + jnp.log1p(jnp.exp(-jnp.abs(lam_d)))
            a = jnp.exp2(r_g * (sp * (-LRU_C * LOG2E)))
            one_m_a2 = 1.0 - a * a
            root = jnp.where(one_m_a2 > 0.0, one_m_a2 * lax.rsqrt(one_m_a2), 0.0)
            inp = root * (i_g * u)
            if not is_ctx:
                h0 = st_ref[0, 0, d:d + 1, cl]
                first_row = (T - 1) if reverse else 0
                inp = inp + jnp.where(row_id == first_row, a * h0, 0.0)
            yield
            y_d = _linear_scan(a, inp, T, reverse)
            yield
            if is_ctx:
                last_row = 0 if reverse else T - 1
                ost_ref[s, 0, d:d + 1, cl] = y_d[last_row:last_row + 1, :]
            if d == 0:
                lru_state[ct] = (u, u_bf, y_d)
            else:
                lru_gate = zx_scr[:, LRU_WIDTH + ct * LANES:LRU_WIDTH + (ct + 1) * LANES]
                mix_scr[:, M_LRU + ct * LANES:M_LRU + (ct + 1) * LANES] = (
                    (y_prev + y_d) * lru_gate).astype(BF16)
            yield

        def cached_kv(c):
            r = rows_of(c)
            k_scr[r, :] = ck_ref[0, 0, :, r].T.astype(BF16)
            vt_scr[:, r] = cv_ref[0, 0, :, r].astype(BF16)
            for hd in range(DIFF_HEADS):
                cl = slice(hd * LANES, (hd + 1) * LANES)
                dk_scr[r, cl] = cdk_ref[0, 0, cl, r].T.astype(BF16)
                rows_h = pl.ds(c * ROW_CHUNK * DIFF_HEADS + hd, ROW_CHUNK, stride=DIFF_HEADS)
                dvt_scr[cl, r] = cdv_ref[0, 0, rows_h, :].T.astype(BF16)

        def new_kv(c):
            r = rows_of(c)
            rk = pl.ds(past_len + c * ROW_CHUNK, ROW_CHUNK)
            hc = h_scr[r, :]
            zkv = _dot(hc, win(C_GK, 2 * GQA_KV_WIDTH))
            k = _rms_heads(zkv[:, 0:GQA_KV_WIDTH], k_gain)
            v_t = zkv[:, GQA_KV_WIDTH:].T
            if is_ctx:
                ogk_ref[s, 0, :, r] = k.T
                ogv_ref[s, 0, :, r] = v_t
            else:
                k = _rope(k, cos_ref[r, :], sin_ref[r, :])
            k_scr[rk, :] = k.astype(BF16)
            vt_scr[:, rk] = v_t.astype(BF16)
            yield
            zdk = _dot(hc, win(C_DK, DIFF_WIDTH))
            for hd in range(DIFF_HEADS):
                cl = slice(hd * LANES, (hd + 1) * LANES)
                dk = zdk[:, cl]
                if is_ctx:
                    odk_ref[s, 0, cl, r] = dk.T
                else:
                    dk = _rope(dk, cos_ref[r, :], sin_ref[r, :])
                dk_scr[rk, cl] = dk.astype(BF16)
            yield
            zdv = _dot(hc, win(C_DV, DIFF_WIDTH))
            for hd in range(DIFF_HEADS):
                cl = slice(hd * LANES, (hd + 1) * LANES)
                if is_ctx:
                    rows_h = pl.ds(c * ROW_CHUNK * DIFF_HEADS + hd, ROW_CHUNK, stride=DIFF_HEADS)
                    odv_ref[s, 0, rows_h, :] = zdv[:, cl]
                dvt_scr[cl, rk] = zdv[:, cl].T.astype(BF16)
            yield

        def new_q_piece(cq, g):
            start = cq * ROW_CHUNK
            rq = pl.ds(start if isinstance(cq, int) else pl.multiple_of(start, ROW_CHUNK), ROW_CHUNK)
            c0 = C_GQ if g == 0 else C_DQ + (g - 1) * 2 * LANES
            z = _dot(h_scr[rq, :], win(c0, 2 * LANES))
            for j in range(2):
                t = z[:, j * LANES:(j + 1) * LANES].T
                if g == 0:
                    gain = jnp.concatenate([q_gain_cols] * (ROW_CHUNK // LANES), axis=1)
                    halves = []
                    for hh in range(LANES // HEAD_DIM):
                        th = t[hh * HEAD_DIM:(hh + 1) * HEAD_DIM]
                        ms = jnp.sum(th * th, axis=0, keepdims=True) * (1.0 / HEAD_DIM)
                        halves.append(th * lax.rsqrt(ms + EPS) * gain)
                    t = jnp.concatenate(halves, axis=0)
                if not is_ctx:
                    half = HEAD_DIM // 4
                    blocks = [t[i * half:(i + 1) * half] for i in range(LANES // half)]
                    partner = jnp.concatenate(
                        [blocks[i + 1 - 2 * (i % 2)] for i in range(len(blocks))], axis=0)
                    t = t * cost_ref[cq] + partner * sint_ref[cq]
                t = (t * QK_SCALE).astype(BF16)
                if g == 0:
                    qt_scr[cq, j * LANES:(j + 1) * LANES, :] = t
                else:
                    hd = (g - 1) * 2 + j
                    dqt_scr[cq, hd * LANES:(hd + 1) * LANES, :] = t

        n_q_pieces = 1 + DIFF_HEADS // 2

        for c in range(n_chunks):
            prenorm_and_lru_in(c)
            yield
        lru_pieces = [(ct, d) for ct in range(LRU_WIDTH // LANES) for d in range(2)]
        n_cached = past_len // ROW_CHUNK
        for i in range(max(len(lru_pieces), n_chunks, n_cached)):
            if i < n_chunks:
                yield from new_kv(i)
            if i < len(lru_pieces):
                yield from lru_piece(*lru_pieces[i])
            if i < n_cached:
                cached_kv(i)
                yield
        for g in range(n_q_pieces):
            new_q_piece(0, g)
            yield
        yield _PHASE_END

        zero_half = jnp.zeros((HEAD_DIM, ROW_CHUNK), BF16)

        group = GQA_Q_HEADS // GQA_KV_HEADS
        n_pairs = GQA_KV_HEADS + DIFF_HEADS
        n_kc = Tk // KEY_CHUNK

        def pair_qt(cq, p):
            if p < GQA_KV_HEADS:
                heads = [qt_scr[cq, (p * group + i) * HEAD_DIM:(p * group + i + 1) * HEAD_DIM, :]
                         for i in range(group)]
                cols = [jnp.concatenate([q, zero_half] if p == 0 else [zero_half, q], axis=0)
                        for q in heads]
            else:
                hd = p - GQA_KV_HEADS
                q1 = dqt_scr[cq, hd * LANES:hd * LANES + HEAD_DIM, :]
                q2 = dqt_scr[cq, hd * LANES + HEAD_DIM:(hd + 1) * LANES, :]
                cols = [jnp.concatenate([q1, zero_half], axis=0),
                        jnp.concatenate([zero_half, q2], axis=0)]
            return jnp.concatenate(cols, axis=1)

        def attn_steps(c, next_q):
            start = c * ROW_CHUNK
            r = pl.ds(start if isinstance(c, int) else pl.multiple_of(start, ROW_CHUNK), ROW_CHUNK)
            hc = h_scr[r, :]

            def pair_keys(p, i):
                rk = slice(i * KEY_CHUNK, (i + 1) * KEY_CHUNK)
                if p < GQA_KV_HEADS:
                    return k_scr[rk, :]
                hd = p - GQA_KV_HEADS
                return dk_scr[rk, hd * LANES:(hd + 1) * LANES]

            def pair_values_t(p):
                if p < GQA_KV_HEADS:
                    return vt_scr[...]
                hd = p - GQA_KV_HEADS
                return dvt_scr[hd * LANES:(hd + 1) * LANES, :]

            def scores(p, qt, i, m):
                sc = _dot(pair_keys(p, i), qt)
                s_scr[p % 2, i * KEY_CHUNK:(i + 1) * KEY_CHUNK, :] = sc
                mi = jnp.max(sc, axis=0, keepdims=True)
                return mi if m is None else jnp.maximum(m, mi)

            def finish(p, o_t, gate_tile):
                if p < GQA_KV_HEADS:
                    o = jnp.concatenate(
                        [o_t[p * HEAD_DIM:(p + 1) * HEAD_DIM, i * ROW_CHUNK:(i + 1) * ROW_CHUNK]
                         for i in range(group)], axis=0).T
                    col = M_GQA + p * LANES
                else:
                    hd = p - GQA_KV_HEADS
                    o = _rms((o_t[:, 0:ROW_CHUNK] - lam * o_t[:, ROW_CHUNK:]).T, sub_gain) * (1.0 - lam_i)
                    col = M_DIFF + hd * LANES
                mix_scr[r, col:col + LANES] = (o * gate_tile).astype(BF16)

            qt_next = pair_qt(c, 0)
            m_next = None
            for i in range(n_kc):
                m_next = scores(0, qt_next, i, m_next)
            def gate_tile_pair(j):
                c0 = C_GG if j == 0 else C_DG + (j - 1) * 2 * LANES
                return _silu(_dot(hc, win(c0, 2 * LANES)))

            gates_early = False
            gates = [gate_tile_pair(j) for j in range(n_pairs // 2)] if gates_early else None
            yield
            for p in range(n_pairs):
                m = m_next
                m_next = None
                if p + 1 < n_pairs:
                    qt_next = pair_qt(c, p + 1)
                l = None
                for i in range(n_kc):
                    if p + 1 < n_pairs:
                        m_next = scores(p + 1, qt_next, i, m_next)
                    rk = slice(i * KEY_CHUNK, (i + 1) * KEY_CHUNK)
                    e = jnp.exp2(s_scr[p % 2, rk, :] - m)
                    li = jnp.sum(e, axis=0, keepdims=True)
                    l = li if l is None else l + li
                    e_scr[p % 2, rk, :] = e.astype(BF16)
                    yield
                if next_q and p % 2 == 0 and p // 2 < n_q_pieces:
                    new_q_piece(jnp.where(c + 1 == n_chunks, 0, c + 1), p // 2)
                if p % 2 == 0:
                    gate_pair = gates[p // 2] if gates_early else gate_tile_pair(p // 2)
                finish(p, _dot(pair_values_t(p), e_scr[p % 2]) / l,
                       gate_pair[:, (p % 2) * LANES:(p % 2 + 1) * LANES])
                yield

        if n_chunks == 1:
            yield from attn_steps(0, next_q=False)
        else:
            def attn_chunk(c, carry):
                for _ in attn_steps(c, next_q=True):
                    pass
                return carry
            lax.fori_loop(0, n_chunks, attn_chunk, 0)
            yield

        for c in range(n_chunks):
            r = rows_of(c)
            halves = []
            for j in range(2):
                half_n = wout_ref.shape[-1] // 2
                halves.append(_dot(mix_scr[r, :], wout_ref[0, :, j * half_n:(j + 1) * half_n]))
                yield
            out = jnp.concatenate(halves, axis=1)
            y_ref[s, r, :] = y_ref[s, r, :] + _rms(out, post_gain)
            yield

    def phase_done(gen):
        return next(gen, _PHASE_END) is _PHASE_END

    gens = [sequence_steps(s) for s in range(n_seq)]
    while not phase_done(gens[0]):
        pass
    for s in range(n_seq):
        cur_done = False
        nxt_done = s + 1 >= n_seq
        while not (cur_done and nxt_done):
            if not cur_done:
                cur_done = phase_done(gens[s])
            if not nxt_done:
                nxt_done = phase_done(gens[s + 1])


def _rope_tables(seq_len):
    quarter = HEAD_DIM // 4
    inv = np.power(np.float32(ROPE_BASE), -np.arange(quarter, dtype=np.float32) / quarter)
    t = np.arange(seq_len)
    row = (t // GRID_W).astype(np.float32)[:, None] * inv[None]
    col = (t % GRID_W).astype(np.float32)[:, None] * inv[None]
    cos = np.concatenate([np.cos(row), np.cos(row), np.cos(col), np.cos(col)], axis=-1)
    sin = np.concatenate([-np.sin(row), np.sin(row), -np.sin(col), np.sin(col)], axis=-1)
    reps = LANES // HEAD_DIM
    return (np.tile(cos, (1, reps)).astype(np.float32), np.tile(sin, (1, reps)).astype(np.float32))


def _block_diag(w):
    nb, bw, _ = w.shape
    eye = jnp.eye(nb, dtype=w.dtype)
    return (eye[:, None, :, None] * w[:, :, None, :]).reshape(nb * bw, nb * bw)


def _layer_call(x, mod, weights, extras, *, is_ctx, n_seq, layer0, n_layers):
    batch, seq_len, d_model = x.shape
    depth = weights[2].shape[0]
    past_len = 0 if is_ctx else extras[0].shape[3]
    tk = past_len + seq_len
    n_chunks = seq_len // ROW_CHUNK
    grid = (batch // n_seq, n_layers)
    single = dict(pipeline_mode=pl.Buffered(1))
    w_mode = single if n_layers == 1 else {}

    def per_layer(a):
        nd = a.ndim
        if a.dtype != BF16:
            return pl.BlockSpec(a.shape, lambda b, l: (0,) * nd, **single)
        return pl.BlockSpec((1,) + a.shape[1:], lambda b, l: (l + layer0,) + (0,) * (nd - 1), **w_mode)

    y_spec = pl.BlockSpec((n_seq, seq_len, d_model), lambda b, l: (b, 0, 0))
    x_spec = y_spec
    mod_spec = pl.BlockSpec((1,) + mod.shape[1:], lambda b, l: (l + layer0, 0, 0))
    in_specs = [x_spec, mod_spec] + [per_layer(w) for w in weights]
    args = [x, mod] + list(weights)
    if not is_ctx:
        for a in extras[:5]:
            in_specs.append(
                pl.BlockSpec((1, 1) + a.shape[2:], lambda b, l: (b, l + layer0, 0, 0)))
        for a in extras[5:]:
            in_specs.append(pl.BlockSpec(a.shape, lambda b, l, nd=a.ndim: (0,) * nd, **single))
        args += list(extras)

    out_shape = [jax.ShapeDtypeStruct(x.shape, F32)]
    out_specs = [y_spec]
    if is_ctx:
        for rows_, cols_ in ((GQA_KV_WIDTH, seq_len), (GQA_KV_WIDTH, seq_len), (DIFF_WIDTH, seq_len),
                             (seq_len * DIFF_HEADS, DIFF_WIDTH // DIFF_HEADS)):
            out_shape.append(jax.ShapeDtypeStruct((batch, depth, rows_, cols_), F32))
            out_specs.append(pl.BlockSpec((n_seq, 1, rows_, cols_), lambda b, l: (b, l, 0, 0)))
        out_shape.append(jax.ShapeDtypeStruct((batch, depth, 2, LRU_WIDTH), F32))
        out_specs.append(pl.BlockSpec((n_seq, 1, 2, LRU_WIDTH), lambda b, l: (b, l, 0, 0)))

    scratch = [pltpu.VMEM((n_seq,) + shape, dtype) for shape, dtype in (
        ((seq_len, d_model), BF16),
        ((seq_len, D_MIX), BF16),
        ((seq_len, 2 * LRU_WIDTH), F32),
        ((n_chunks, GQA_WIDTH, ROW_CHUNK), BF16),
        ((tk, GQA_KV_WIDTH), BF16),
        ((GQA_KV_WIDTH, tk), BF16),
        ((n_chunks, DIFF_WIDTH, ROW_CHUNK), BF16),
        ((tk, DIFF_WIDTH), BF16),
        ((DIFF_WIDTH, tk), BF16),
        ((2, tk, 2 * ROW_CHUNK), F32),
        ((2, tk, 2 * ROW_CHUNK), BF16),
    )]
    kern = functools.partial(_layer_kernel, is_ctx=is_ctx, n_seq=n_seq, seq_len=seq_len,
                             past_len=past_len, depth=depth, layer0=layer0)
    return pl.pallas_call(
        kern,
        grid=grid,
        in_specs=in_specs,
        out_specs=out_specs,
        out_shape=out_shape,
        scratch_shapes=scratch,
        compiler_params=pltpu.CompilerParams(
            dimension_semantics=("arbitrary", "arbitrary"),
            vmem_limit_bytes=VMEM_LIMIT),
        name="ctx_pass" if is_ctx else f"denoise_layer{layer0}",
    )(*args)


def kernel(x_prompt, x_sample, cache_gqa_k, cache_gqa_v, cache_diff_k, cache_diff_v, state_lru, c, c_ctx, w_mod, b_mod, g_pre, g_post, w_in, w_out, lru_conv_w, lru_conv_b, lru_wa, lru_ba, lru_wx, lru_bx, lru_lambda, gqa_gq, gqa_gk, diff_lam, diff_gsub):
    depth, d_model, _ = w_in.shape
    dec_batch = x_sample.shape[0]

    cond = jnp.concatenate(
        [c_ctx[None, :], c, jnp.zeros((MOD_ROWS - 1 - dec_batch, d_model), F32)], axis=0)
    n_mod = w_mod.shape[-1]
    mod_tile = n_mod // 3
    mod = pl.pallas_call(
        _mod_kernel,
        grid=(depth, n_mod // mod_tile),
        in_specs=[pl.BlockSpec((MOD_ROWS, d_model), lambda l, j: (0, 0)),
                  pl.BlockSpec((1, d_model, mod_tile), lambda l, j: (l, 0, j)),
                  pl.BlockSpec((1, 1, mod_tile), lambda l, j: (l, 0, j))],
        out_specs=pl.BlockSpec((1, MOD_ROWS, mod_tile), lambda l, j: (l, 0, j)),
        out_shape=jax.ShapeDtypeStruct((depth, MOD_ROWS, n_mod), F32),
        name="adaln_mod",
    )(cond, w_mod, b_mod[:, None, :])

    w_in_p = w_in.astype(BF16)
    w_out_p = w_out.astype(BF16)
    bd = jax.vmap(jax.vmap(_block_diag))
    wa_d, wx_d = bd(lru_wa), bd(lru_wx)
    n_ct = LRU_WIDTH // LANES
    wg = jnp.stack([
        jnp.concatenate([m[:, d, ct * LANES:(ct + 1) * LANES, ct * LANES:(ct + 1) * LANES]
                         for d in range(2) for m in (wa_d, wx_d)], axis=-1)
        for ct in range(n_ct)], axis=1).astype(BF16)
    bg = jnp.stack([
        jnp.concatenate([m[:, d, ct * LANES:(ct + 1) * LANES]
                         for d in range(2) for m in (lru_ba, lru_bx)], axis=-1)
        for ct in range(n_ct)], axis=1)
    reps = LANES // HEAD_DIM
    weights = [g_pre, g_post, w_in_p, w_out_p, lru_conv_w, lru_conv_b, wg, bg, lru_lambda,
               jnp.broadcast_to(gqa_gq[:, :, None], (depth, HEAD_DIM, LANES)),
               jnp.tile(gqa_gk, (1, reps)), diff_lam, diff_gsub]

    y_prompt, gk, gv, dk, dv, st = _layer_call(x_prompt, mod, weights, None, is_ctx=True, n_seq=2,
                                               layer0=0, n_layers=depth)
    b, t = x_prompt.shape[:2]
    new_gqa_k = gk.reshape(b, depth, GQA_KV_HEADS, HEAD_DIM, t).transpose(0, 1, 4, 2, 3)
    new_gqa_v = gv.reshape(b, depth, GQA_KV_HEADS, HEAD_DIM, t).transpose(0, 1, 4, 2, 3)
    new_diff_k = dk.reshape(b, depth, DIFF_HEADS, 2, HEAD_DIM, t).transpose(0, 1, 5, 2, 3, 4)
    new_diff_v = dv.reshape(b, depth, t, DIFF_HEADS, 2 * HEAD_DIM)

    db, _, past = cache_gqa_k.shape[:3]
    cos, sin = _rope_tables(x_sample.shape[1])

    def feature_major_chunks(tab):
        return tab.T.reshape(LANES, -1, ROW_CHUNK).transpose(1, 0, 2)

    extras = [cache_gqa_k.transpose(0, 1, 3, 4, 2).reshape(db, depth, GQA_KV_WIDTH, past),
              cache_gqa_v.transpose(0, 1, 3, 4, 2).reshape(db, depth, GQA_KV_WIDTH, past),
              cache_diff_k.transpose(0, 1, 3, 4, 5, 2).reshape(db, depth, DIFF_WIDTH, past),
              cache_diff_v.reshape(db, depth, past * DIFF_HEADS, 2 * HEAD_DIM),
              state_lru, cos, sin, feature_major_chunks(cos), feature_major_chunks(sin)]
    y_sample = x_sample
    for lyr in range(depth):
        (y_sample,) = _layer_call(y_sample, mod, weights, extras, is_ctx=False, n_seq=1,
                                  layer0=lyr, n_layers=1)

    return (y_prompt, y_sample, new_gqa_k, new_gqa_v, new_diff_k, new_diff_v, st)
```

```python
import functools
import math

import jax
import jax.numpy as jnp
import numpy as np
from jax import lax
from jax.experimental import pallas as pl
from jax.experimental.pallas import tpu as pltpu

F32 = jnp.float32
BF16 = jnp.bfloat16

GRID_W = 64
HEAD_DIM = 64
EPS = 1e-6
ROPE_BASE = 10000.0
LRU_WIDTH = 256
LRU_BLOCKS = 4
LRU_C = 8.0
CONV_W = 4
CONV_LEFT = 2
GQA_Q_HEADS = 4
GQA_KV_HEADS = 2
GQA_WIDTH = GQA_Q_HEADS * HEAD_DIM
GQA_KV_WIDTH = GQA_KV_HEADS * HEAD_DIM
DIFF_HEADS = 4
DIFF_WIDTH = DIFF_HEADS * 2 * HEAD_DIM
LOG2E = math.log2(math.e)
QK_SCALE = HEAD_DIM ** -0.5 * LOG2E

C_LRU_X = 0
C_LRU_G = C_LRU_X + LRU_WIDTH
C_GQ = C_LRU_G + LRU_WIDTH
C_GK = C_GQ + GQA_WIDTH
C_GV = C_GK + GQA_KV_WIDTH
C_GG = C_GV + GQA_KV_WIDTH
C_DQ = C_GG + GQA_WIDTH
C_DK = C_DQ + DIFF_WIDTH
C_DV = C_DK + DIFF_WIDTH
C_DG = C_DV + DIFF_WIDTH
N_IN = C_DG + DIFF_WIDTH
M_LRU = 0
M_GQA = LRU_WIDTH
M_DIFF = LRU_WIDTH + GQA_WIDTH
D_MIX = LRU_WIDTH + GQA_WIDTH + DIFF_WIDTH

LANES = 128
SUBLANES = 8
ROW_CHUNK = 256
KEY_CHUNK = 256
MOD_ROWS = 16
VMEM_LIMIT = 58 * 1024 * 1024


def _lambda_init(layer):
    return 0.8 - 0.6 * math.exp(-0.3 * layer)


def _dot(a, b):
    return jnp.dot(a, b, preferred_element_type=F32)


def _silu(x):
    return x * jax.nn.sigmoid(x)


def _rms(x, g):
    ms = jnp.mean(x * x, axis=-1, keepdims=True)
    return x * lax.rsqrt(ms + EPS) * g


def _rms_heads_t(t, gain_cols):
    gain = jnp.concatenate([gain_cols] * (t.shape[1] // LANES), axis=1)
    heads = []
    for hh in range(t.shape[0] // HEAD_DIM):
        th = t[hh * HEAD_DIM:(hh + 1) * HEAD_DIM]
        ms = jnp.sum(th * th, axis=0, keepdims=True) * (1.0 / HEAD_DIM)
        heads.append(th * lax.rsqrt(ms + EPS) * gain)
    return jnp.concatenate(heads, axis=0)


def _rms_heads(x, g2):
    lo = lax.broadcasted_iota(jnp.int32, x.shape, 1) < HEAD_DIM
    t = x * x
    s_lo = jnp.sum(jnp.where(lo, t, 0.0), axis=-1, keepdims=True)
    s_hi = jnp.sum(jnp.where(lo, 0.0, t), axis=-1, keepdims=True)
    ms = jnp.where(lo, s_lo, s_hi) * (1.0 / HEAD_DIM)
    return x * lax.rsqrt(ms + EPS) * g2


def _rope(x, cos, sin_signed):
    first = (lax.broadcasted_iota(jnp.int32, x.shape, 1) % 32) < 16
    partner = jnp.where(first, pltpu.roll(x, LANES - 16, 1), pltpu.roll(x, 16, 1))
    return x * cos + partner * sin_signed


def _shift_rows(x, k, n_rows):
    nb = n_rows // SUBLANES
    x3 = x.reshape(nb, SUBLANES, x.shape[-1])
    sub = lax.broadcasted_iota(jnp.int32, x3.shape, 1)
    rolled = pltpu.roll(x3, k % SUBLANES, 1)
    edge = jnp.zeros_like(rolled[:1])
    if k > 0:
        y3 = jnp.where(sub >= k, rolled, jnp.concatenate([edge, rolled[:-1]], axis=0))
    else:
        y3 = jnp.where(sub < SUBLANES + k, rolled, jnp.concatenate([rolled[1:], edge], axis=0))
    return y3.reshape(n_rows, x.shape[-1])


def _linear_scan(a, x, n_rows, reverse, h0=None):
    nb = n_rows // SUBLANES
    lanes = a.shape[-1]
    a3 = a.reshape(nb, SUBLANES, lanes)
    x3 = x.reshape(nb, SUBLANES, lanes)
    sub = lax.broadcasted_iota(jnp.int32, (nb, SUBLANES, lanes), 1)
    k = 1
    while k < SUBLANES:
        keep = (sub < SUBLANES - k) if reverse else (sub >= k)
        shift = SUBLANES - k if reverse else k
        x3 = a3 * jnp.where(keep, pltpu.roll(x3, shift, 1), 0.0) + x3
        a3 = a3 * jnp.where(keep, pltpu.roll(a3, shift, 1), 1.0)
        k *= 2
    blocks = [None] * nb
    carry = h0
    for j in (range(nb - 1, -1, -1) if reverse else range(nb)):
        xb = x3[j]
        if carry is not None:
            xb = xb + a3[j] * carry
        blocks[j] = xb
        carry = xb[0:1] if reverse else xb[SUBLANES - 1:SUBLANES]
    return jnp.concatenate(blocks, axis=0)


_PHASE_END = "phase-end"


def _mod_kernel(cond_ref, w_ref, b_ref, o_ref):
    o_ref[0] = _dot(_silu(cond_ref[...]).astype(BF16), w_ref[0].astype(BF16)) + b_ref[0]


def _layer_kernel(*refs, is_ctx, n_seq, seq_len, past_len, depth, layer0):
    T = seq_len
    Tk = past_len + T
    n_chunks = T // ROW_CHUNK
    it = iter(refs)
    x_ref, mod_ref, gpre_ref, gpost_ref, win_ref, wout_ref = (next(it) for _ in range(6))
    convw_ref, convb_ref, wg_ref, bg_ref, lam_ref = (next(it) for _ in range(5))
    gqc_ref, gk_ref, dlam_ref, gsub_ref = (next(it) for _ in range(4))
    if not is_ctx:
        (ck_ref, cv_ref, cdk_ref, cdv_ref, st_ref, cos_ref, sin_ref, cost_ref, sint_ref) = (
            next(it) for _ in range(9))
    y_ref = next(it)
    if is_ctx:
        ogk_ref, ogv_ref, odk_ref, odv_ref, ost_ref = (next(it) for _ in range(5))
    scratch_refs = [next(it) for _ in range(11)]

    @pl.when(pl.program_id(1) == 0)
    def _():
        y_ref[...] = x_ref[...]

    layer = pl.program_id(1) + layer0

    lam_i = jnp.float32(_lambda_init(depth - 1))
    for lyr in range(depth - 2, -1, -1):
        lam_i = jnp.where(layer == lyr, jnp.float32(_lambda_init(lyr)), lam_i)

    def layer_row(ref):
        return ref[pl.ds(layer, 1), :]

    conv_w, conv_b = convw_ref[layer], layer_row(convb_ref)
    gate_bias, lam_all = bg_ref[layer], lam_ref[layer]
    q_gain_cols, k_gain, sub_gain = gqc_ref[layer], layer_row(gk_ref), layer_row(gsub_ref)
    dl = dlam_ref[layer]

    d_model = gpre_ref.shape[-1]
    mod_row = mod_ref[0, 0:1, :] if is_ctx else mod_ref[0, pl.ds(pl.program_id(0) + 1, 1), :]
    shift = mod_row[:, 0:d_model]
    pre_gain = layer_row(gpre_ref) * (1.0 + mod_row[:, d_model:2 * d_model])
    post_gain = layer_row(gpost_ref) * mod_row[:, 2 * d_model:3 * d_model]
    lam = (jnp.exp(jnp.sum(dl[0:1] * dl[1:2], axis=-1, keepdims=True))
           - jnp.exp(jnp.sum(dl[2:3] * dl[3:4], axis=-1, keepdims=True)) + lam_i)

    def win(c0, n):
        return win_ref[0, :, c0:c0 + n]

    def rows_of(c):
        return pl.ds(c * ROW_CHUNK, ROW_CHUNK)

    def sequence_steps(s):
        (h_scr, mix_scr, zx_scr, qt_scr, k_scr, vt_scr, dqt_scr, dk_scr, dvt_scr, s_scr, e_scr) = (
            ref.at[s] for ref in scratch_refs)

        def prenorm_and_lru_in(c):
            r = rows_of(c)
            hn = _rms(y_ref[s, r, :], pre_gain) + shift
            hc = hn.astype(BF16)
            h_scr[r, :] = hc
            zxg = _dot(hc, win(C_LRU_X, 2 * LRU_WIDTH))
            zx_scr[r, 0:LRU_WIDTH] = zxg[:, 0:LRU_WIDTH]
            zx_scr[r, LRU_WIDTH:] = _silu(zxg[:, LRU_WIDTH:])

        lru_state = {}

        def lru_piece(ct, d):
            cl = slice(ct * LANES, (ct + 1) * LANES)
            reverse = d == 1
            if d == 0:
                zx = zx_scr[:, cl]
                u = conv_b[:, cl] + zx * conv_w[CONV_LEFT:CONV_LEFT + 1, cl]
                for j in range(CONV_W):
                    if j != CONV_LEFT:
                        u = u + _shift_rows(zx, CONV_LEFT - j, T) * conv_w[j:j + 1, cl]
                lru_state[ct] = (u, u.astype(BF16), None)
                yield
            u, u_bf, y_prev = lru_state[ct]
            gates = _dot(u_bf, wg_ref[0, ct, :, d * 2 * LANES:(d + 1) * 2 * LANES])
            gates = gates + gate_bias[ct:ct + 1, d * 2 * LANES:(d + 1) * 2 * LANES]
            r_g = jax.nn.sigmoid(gates[:, 0:LANES])
            i_g = jax.nn.sigmoid(gates[:, LANES:])
            yield
            lam_d = lam_all[d:d + 1, cl]
            sp = jnp.maximum(-lam_d, 0.0) + jnp.log1p(jnp.exp(-jnp.abs(lam_d)))
            a = jnp.exp2(r_g * (sp * (-LRU_C * LOG2E)))
            one_m_a2 = 1.0 - a * a
            root = jnp.where(one_m_a2 > 0.0, one_m_a2 * lax.rsqrt(one_m_a2), 0.0)
            inp = root * (i_g * u)
            yield
            y_d = _linear_scan(a, inp, T, reverse, None if is_ctx else st_ref[0, 0, d:d + 1, cl])
            yield
            if is_ctx:
                last_row = 0 if reverse else T - 1
                ost_ref[s, 0, d:d + 1, cl] = y_d[last_row:last_row + 1, :]
            if d == 0:
                lru_state[ct] = (u, u_bf, y_d)
            else:
                lru_gate = zx_scr[:, LRU_WIDTH + ct * LANES:LRU_WIDTH + (ct + 1) * LANES]
                mix_scr[:, M_LRU + ct * LANES:M_LRU + (ct + 1) * LANES] = (
                    (y_prev + y_d) * lru_gate).astype(BF16)
            yield

        def cached_kv(c):
            r = rows_of(c)
            k_scr[r, :] = ck_ref[0, 0, :, r].T.astype(BF16)
            vt_scr[:, r] = cv_ref[0, 0, :, r].astype(BF16)
            for hd in range(DIFF_HEADS):
                cl = slice(hd * LANES, (hd + 1) * LANES)
                dk_scr[r, cl] = cdk_ref[0, 0, cl, r].T.astype(BF16)
                rows_h = pl.ds(c * ROW_CHUNK * DIFF_HEADS + hd, ROW_CHUNK, stride=DIFF_HEADS)
                dvt_scr[cl, r] = cdv_ref[0, 0, rows_h, :].T.astype(BF16)

        def new_kv(c):
            r = rows_of(c)
            rk = pl.ds(past_len + c * ROW_CHUNK, ROW_CHUNK)
            hc = h_scr[r, :]
            zkv = _dot(hc, win(C_GK, 2 * GQA_KV_WIDTH))
            k = _rms_heads(zkv[:, 0:GQA_KV_WIDTH], k_gain)
            v_t = zkv[:, GQA_KV_WIDTH:].T
            if is_ctx:
                ogk_ref[s, 0, :, r] = k.T
                ogv_ref[s, 0, :, r] = v_t
            else:
                k = _rope(k, cos_ref[r, :], sin_ref[r, :])
            k_scr[rk, :] = k.astype(BF16)
            vt_scr[:, rk] = v_t.astype(BF16)
            yield
            zdk = _dot(hc, win(C_DK, DIFF_WIDTH))
            for hd in range(DIFF_HEADS):
                cl = slice(hd * LANES, (hd + 1) * LANES)
                dk = zdk[:, cl]
                if is_ctx:
                    odk_ref[s, 0, cl, r] = dk.T
                else:
                    dk = _rope(dk, cos_ref[r, :], sin_ref[r, :])
                dk_scr[rk, cl] = dk.astype(BF16)
            yield
            zdv = _dot(hc, win(C_DV, DIFF_WIDTH))
            for hd in range(DIFF_HEADS):
                cl = slice(hd * LANES, (hd + 1) * LANES)
                if is_ctx:
                    rows_h = pl.ds(c * ROW_CHUNK * DIFF_HEADS + hd, ROW_CHUNK, stride=DIFF_HEADS)
                    odv_ref[s, 0, rows_h, :] = zdv[:, cl]
                dvt_scr[cl, rk] = zdv[:, cl].T.astype(BF16)
            yield

        def new_q_piece(cq, g):
            start = cq * ROW_CHUNK
            rq = pl.ds(start if isinstance(cq, int) else pl.multiple_of(start, ROW_CHUNK), ROW_CHUNK)
            c0 = C_GQ if g == 0 else C_DQ + (g - 1) * 2 * LANES
            z = _dot(h_scr[rq, :], win(c0, 2 * LANES))
            for j in range(2):
                t = z[:, j * LANES:(j + 1) * LANES].T
                if g == 0:
                    t = _rms_heads_t(t, q_gain_cols)
                if not is_ctx:
                    half = HEAD_DIM // 4
                    blocks = [t[i * half:(i + 1) * half] for i in range(LANES // half)]
                    partner = jnp.concatenate(
                        [blocks[i + 1 - 2 * (i % 2)] for i in range(len(blocks))], axis=0)
                    t = t * cost_ref[cq] + partner * sint_ref[cq]
                t = (t * QK_SCALE).astype(BF16)
                if g == 0:
                    qt_scr[cq, j * LANES:(j + 1) * LANES, :] = t
                else:
                    hd = (g - 1) * 2 + j
                    dqt_scr[cq, hd * LANES:(hd + 1) * LANES, :] = t

        n_q_pieces = 1 + DIFF_HEADS // 2

        for c in range(n_chunks):
            prenorm_and_lru_in(c)
            yield
        lru_pieces = [(ct, d) for ct in range(LRU_WIDTH // LANES) for d in range(2)]
        n_cached = past_len // ROW_CHUNK
        for i in range(max(len(lru_pieces), n_chunks, n_cached)):
            if i < n_chunks:
                yield from new_kv(i)
            if i < len(lru_pieces):
                yield from lru_piece(*lru_pieces[i])
            if i < n_cached:
                cached_kv(i)
                yield
        for g in range(n_q_pieces):
            new_q_piece(0, g)
            yield
        yield _PHASE_END

        zero_half = jnp.zeros((HEAD_DIM, ROW_CHUNK), BF16)

        group = GQA_Q_HEADS // GQA_KV_HEADS
        n_pairs = GQA_KV_HEADS + DIFF_HEADS
        n_kc = Tk // KEY_CHUNK

        def pair_qt(cq, p):
            if p < GQA_KV_HEADS:
                heads = [qt_scr[cq, (p * group + i) * HEAD_DIM:(p * group + i + 1) * HEAD_DIM, :]
                         for i in range(group)]
                cols = [jnp.concatenate([q, zero_half] if p == 0 else [zero_half, q], axis=0)
                        for q in heads]
            else:
                hd = p - GQA_KV_HEADS
                q1 = dqt_scr[cq, hd * LANES:hd * LANES + HEAD_DIM, :]
                q2 = dqt_scr[cq, hd * LANES + HEAD_DIM:(hd + 1) * LANES, :]
                cols = [jnp.concatenate([q1, zero_half], axis=0),
                        jnp.concatenate([zero_half, q2], axis=0)]
            return jnp.concatenate(cols, axis=1)

        def attn_steps(c, next_q):
            start = c * ROW_CHUNK
            r = pl.ds(start if isinstance(c, int) else pl.multiple_of(start, ROW_CHUNK), ROW_CHUNK)
            hc = h_scr[r, :]

            def pair_keys(p, i):
                rk = slice(i * KEY_CHUNK, (i + 1) * KEY_CHUNK)
                if p < GQA_KV_HEADS:
                    return k_scr[rk, :]
                hd = p - GQA_KV_HEADS
                return dk_scr[rk, hd * LANES:(hd + 1) * LANES]

            def pair_values_t(p):
                if p < GQA_KV_HEADS:
                    return vt_scr[...]
                hd = p - GQA_KV_HEADS
                return dvt_scr[hd * LANES:(hd + 1) * LANES, :]

            def scores(p, qt, i, m):
                sc = _dot(pair_keys(p, i), qt)
                s_scr[p % 2, i * KEY_CHUNK:(i + 1) * KEY_CHUNK, :] = sc
                mi = jnp.max(sc, axis=0, keepdims=True)
                return mi if m is None else jnp.maximum(m, mi)

            def finish(p, o_t, gate_tile):
                if p < GQA_KV_HEADS:
                    o = jnp.concatenate(
                        [o_t[p * HEAD_DIM:(p + 1) * HEAD_DIM, i * ROW_CHUNK:(i + 1) * ROW_CHUNK]
                         for i in range(group)], axis=0).T
                    col = M_GQA + p * LANES
                else:
                    hd = p - GQA_KV_HEADS
                    o = _rms((o_t[:, 0:ROW_CHUNK] - lam * o_t[:, ROW_CHUNK:]).T, sub_gain) * (1.0 - lam_i)
                    col = M_DIFF + hd * LANES
                mix_scr[r, col:col + LANES] = (o * gate_tile).astype(BF16)

            qt_next = pair_qt(c, 0)
            m_next = None
            for i in range(n_kc):
                m_next = scores(0, qt_next, i, m_next)
            def gate_tile_pair(j):
                c0 = C_GG if j == 0 else C_DG + (j - 1) * 2 * LANES
                return _silu(_dot(hc, win(c0, 2 * LANES)))

            gates_early = n_chunks > 1
            gates = [gate_tile_pair(j) for j in range(n_pairs // 2)] if gates_early else None
            yield
            for p in range(n_pairs):
                m = m_next
                m_next = None
                if p + 1 < n_pairs:
                    qt_next = pair_qt(c, p + 1)
                l = None
                for i in range(n_kc):
                    if p + 1 < n_pairs:
                        m_next = scores(p + 1, qt_next, i, m_next)
                    rk = slice(i * KEY_CHUNK, (i + 1) * KEY_CHUNK)
                    e = jnp.exp2(s_scr[p % 2, rk, :] - m)
                    li = jnp.sum(e, axis=0, keepdims=True)
                    l = li if l is None else l + li
                    e_scr[p % 2, rk, :] = e.astype(BF16)
                    yield
                if next_q and p % 2 == 0 and p // 2 < n_q_pieces:
                    new_q_piece(jnp.where(c + 1 == n_chunks, 0, c + 1), p // 2)
                if p % 2 == 0:
                    gate_pair = gates[p // 2] if gates_early else gate_tile_pair(p // 2)
                finish(p, _dot(pair_values_t(p), e_scr[p % 2]) / l,
                       gate_pair[:, (p % 2) * LANES:(p % 2 + 1) * LANES])
                yield

        if n_chunks == 1:
            yield from attn_steps(0, next_q=False)
        else:
            def attn_chunk(c, carry):
                for _ in attn_steps(c, next_q=True):
                    pass
                return carry
            lax.fori_loop(0, n_chunks, attn_chunk, 0)
            yield

        for c in range(n_chunks):
            r = rows_of(c)
            halves = []
            for j in range(2):
                half_n = wout_ref.shape[-1] // 2
                halves.append(_dot(mix_scr[r, :], wout_ref[0, :, j * half_n:(j + 1) * half_n]))
                yield
            out = jnp.concatenate(halves, axis=1)
            y_ref[s, r, :] = y_ref[s, r, :] + _rms(out, post_gain)
            yield

    def phase_done(gen):
        return next(gen, _PHASE_END) is _PHASE_END

    gens = [sequence_steps(s) for s in range(n_seq)]
    while not phase_done(gens[0]):
        pass
    for s in range(n_seq):
        cur_done = False
        nxt_done = s + 1 >= n_seq
        while not (cur_done and nxt_done):
            if not cur_done:
                cur_done = phase_done(gens[s])
            if not nxt_done:
                nxt_done = phase_done(gens[s + 1])


def _rope_tables(seq_len):
    quarter = HEAD_DIM // 4
    inv = np.power(np.float32(ROPE_BASE), -np.arange(quarter, dtype=np.float32) / quarter)
    t = np.arange(seq_len)
    row = (t // GRID_W).astype(np.float32)[:, None] * inv[None]
    col = (t % GRID_W).astype(np.float32)[:, None] * inv[None]
    cos = np.concatenate([np.cos(row), np.cos(row), np.cos(col), np.cos(col)], axis=-1)
    sin = np.concatenate([-np.sin(row), np.sin(row), -np.sin(col), np.sin(col)], axis=-1)
    reps = LANES // HEAD_DIM
    return (np.tile(cos, (1, reps)).astype(np.float32), np.tile(sin, (1, reps)).astype(np.float32))


def _block_diag(w):
    nb, bw, _ = w.shape
    eye = jnp.eye(nb, dtype=w.dtype)
    return (eye[:, None, :, None] * w[:, :, None, :]).reshape(nb * bw, nb * bw)


def _layer_call(x, mod, weights, extras, *, is_ctx, n_seq, layer0, n_layers):
    batch, seq_len, d_model = x.shape
    depth = weights[2].shape[0]
    past_len = 0 if is_ctx else extras[0].shape[3]
    tk = past_len + seq_len
    n_chunks = seq_len // ROW_CHUNK
    grid = (batch // n_seq, n_layers)
    single = dict(pipeline_mode=pl.Buffered(1))
    w_mode = single if n_layers == 1 else {}

    def per_layer(a):
        nd = a.ndim
        if a.dtype != BF16:
            return pl.BlockSpec(a.shape, lambda b, l: (0,) * nd, **single)
        return pl.BlockSpec((1,) + a.shape[1:], lambda b, l: (l + layer0,) + (0,) * (nd - 1), **w_mode)

    y_spec = pl.BlockSpec((n_seq, seq_len, d_model), lambda b, l: (b, 0, 0))
    x_spec = y_spec
    mod_spec = pl.BlockSpec((1,) + mod.shape[1:], lambda b, l: (l + layer0, 0, 0))
    in_specs = [x_spec, mod_spec] + [per_layer(w) for w in weights]
    args = [x, mod] + list(weights)
    if not is_ctx:
        for a in extras[:5]:
            in_specs.append(
                pl.BlockSpec((1, 1) + a.shape[2:], lambda b, l: (b, l + layer0, 0, 0)))
        for a in extras[5:]:
            in_specs.append(pl.BlockSpec(a.shape, lambda b, l, nd=a.ndim: (0,) * nd, **single))
        args += list(extras)

    out_shape = [jax.ShapeDtypeStruct(x.shape, F32)]
    out_specs = [y_spec]
    if is_ctx:
        for rows_, cols_ in ((GQA_KV_WIDTH, seq_len), (GQA_KV_WIDTH, seq_len), (DIFF_WIDTH, seq_len),
                             (seq_len * DIFF_HEADS, DIFF_WIDTH // DIFF_HEADS)):
            out_shape.append(jax.ShapeDtypeStruct((batch, depth, rows_, cols_), F32))
            out_specs.append(pl.BlockSpec((n_seq, 1, rows_, cols_), lambda b, l: (b, l, 0, 0)))
        out_shape.append(jax.ShapeDtypeStruct((batch, depth, 2, LRU_WIDTH), F32))
        out_specs.append(pl.BlockSpec((n_seq, 1, 2, LRU_WIDTH), lambda b, l: (b, l, 0, 0)))

    scratch = [pltpu.VMEM((n_seq,) + shape, dtype) for shape, dtype in (
        ((seq_len, d_model), BF16),
        ((seq_len, D_MIX), BF16),
        ((seq_len, 2 * LRU_WIDTH), F32),
        ((n_chunks, GQA_WIDTH, ROW_CHUNK), BF16),
        ((tk, GQA_KV_WIDTH), BF16),
        ((GQA_KV_WIDTH, tk), BF16),
        ((n_chunks, DIFF_WIDTH, ROW_CHUNK), BF16),
        ((tk, DIFF_WIDTH), BF16),
        ((DIFF_WIDTH, tk), BF16),
        ((2, tk, 2 * ROW_CHUNK), F32),
        ((2, tk, 2 * ROW_CHUNK), BF16),
    )]
    kern = functools.partial(_layer_kernel, is_ctx=is_ctx, n_seq=n_seq, seq_len=seq_len,
                             past_len=past_len, depth=depth, layer0=layer0)
    return pl.pallas_call(
        kern,
        grid=grid,
        in_specs=in_specs,
        out_specs=out_specs,
        out_shape=out_shape,
        scratch_shapes=scratch,
        compiler_params=pltpu.CompilerParams(
            dimension_semantics=("arbitrary", "arbitrary"),
            vmem_limit_bytes=VMEM_LIMIT),
        name="ctx_pass" if is_ctx else f"denoise_layer{layer0}",
    )(*args)


def kernel(x_prompt, x_sample, cache_gqa_k, cache_gqa_v, cache_diff_k, cache_diff_v, state_lru, c, c_ctx, w_mod, b_mod, g_pre, g_post, w_in, w_out, lru_conv_w, lru_conv_b, lru_wa, lru_ba, lru_wx, lru_bx, lru_lambda, gqa_gq, gqa_gk, diff_lam, diff_gsub):
    depth, d_model, _ = w_in.shape
    dec_batch = x_sample.shape[0]

    cond = jnp.concatenate(
        [c_ctx[None, :], c, jnp.zeros((MOD_ROWS - 1 - dec_batch, d_model), F32)], axis=0)
    n_mod = w_mod.shape[-1]
    mod_tile = n_mod // 3
    mod = pl.pallas_call(
        _mod_kernel,
        grid=(depth, n_mod // mod_tile),
        in_specs=[pl.BlockSpec((MOD_ROWS, d_model), lambda l, j: (0, 0)),
                  pl.BlockSpec((1, d_model, mod_tile), lambda l, j: (l, 0, j)),
                  pl.BlockSpec((1, 1, mod_tile), lambda l, j: (l, 0, j))],
        out_specs=pl.BlockSpec((1, MOD_ROWS, mod_tile), lambda l, j: (l, 0, j)),
        out_shape=jax.ShapeDtypeStruct((depth, MOD_ROWS, n_mod), F32),
        name="adaln_mod",
    )(cond, w_mod, b_mod[:, None, :])

    w_in_p = w_in.astype(BF16)
    w_out_p = w_out.astype(BF16)
    bd = jax.vmap(jax.vmap(_block_diag))
    wa_d, wx_d = bd(lru_wa), bd(lru_wx)
    n_ct = LRU_WIDTH // LANES
    wg = jnp.stack([
        jnp.concatenate([m[:, d, ct * LANES:(ct + 1) * LANES, ct * LANES:(ct + 1) * LANES]
                         for d in range(2) for m in (wa_d, wx_d)], axis=-1)
        for ct in range(n_ct)], axis=1).astype(BF16)
    bg = jnp.stack([
        jnp.concatenate([m[:, d, ct * LANES:(ct + 1) * LANES]
                         for d in range(2) for m in (lru_ba, lru_bx)], axis=-1)
        for ct in range(n_ct)], axis=1)
    reps = LANES // HEAD_DIM
    weights = [g_pre, g_post, w_in_p, w_out_p, lru_conv_w, lru_conv_b, wg, bg, lru_lambda,
               jnp.broadcast_to(gqa_gq[:, :, None], (depth, HEAD_DIM, LANES)),
               jnp.tile(gqa_gk, (1, reps)), diff_lam, diff_gsub]

    y_prompt, gk, gv, dk, dv, st = _layer_call(x_prompt, mod, weights, None, is_ctx=True, n_seq=2,
                                               layer0=0, n_layers=depth)
    b, t = x_prompt.shape[:2]
    new_gqa_k = gk.reshape(b, depth, GQA_KV_HEADS, HEAD_DIM, t).transpose(0, 1, 4, 2, 3)
    new_gqa_v = gv.reshape(b, depth, GQA_KV_HEADS, HEAD_DIM, t).transpose(0, 1, 4, 2, 3)
    new_diff_k = dk.reshape(b, depth, DIFF_HEADS, 2, HEAD_DIM, t).transpose(0, 1, 5, 2, 3, 4)
    new_diff_v = dv.reshape(b, depth, t, DIFF_HEADS, 2 * HEAD_DIM)

    db, _, past = cache_gqa_k.shape[:3]
    cos, sin = _rope_tables(x_sample.shape[1])

    def feature_major_chunks(tab):
        return tab.T.reshape(LANES, -1, ROW_CHUNK).transpose(1, 0, 2)

    extras = [cache_gqa_k.transpose(0, 1, 3, 4, 2).reshape(db, depth, GQA_KV_WIDTH, past),
              cache_gqa_v.transpose(0, 1, 3, 4, 2).reshape(db, depth, GQA_KV_WIDTH, past),
              cache_diff_k.transpose(0, 1, 3, 4, 5, 2).reshape(db, depth, DIFF_WIDTH, past),
              cache_diff_v.reshape(db, depth, past * DIFF_HEADS, 2 * HEAD_DIM),
              state_lru, cos, sin, feature_major_chunks(cos), feature_major_chunks(sin)]
    y_sample = x_sample
    for lyr in range(depth):
        (y_sample,) = _layer_call(y_sample, mod, weights, extras, is_ctx=False, n_seq=1,
                                  layer0=lyr, n_layers=1)

    return (y_prompt, y_sample, new_gqa_k, new_gqa_v, new_diff_k, new_diff_v, st)
```

```python
import functools
import math

import jax
import jax.numpy as jnp
import numpy as np
from jax import lax
from jax.experimental import pallas as pl
from jax.experimental.pallas import tpu as pltpu

F32 = jnp.float32
BF16 = jnp.bfloat16

GRID_W = 64
HEAD_DIM = 64
EPS = 1e-6
ROPE_BASE = 10000.0
LRU_WIDTH = 256
LRU_BLOCKS = 4
LRU_C = 8.0
CONV_W = 4
CONV_LEFT = 2
GQA_Q_HEADS = 4
GQA_KV_HEADS = 2
GQA_WIDTH = GQA_Q_HEADS * HEAD_DIM
GQA_KV_WIDTH = GQA_KV_HEADS * HEAD_DIM
DIFF_HEADS = 4
DIFF_WIDTH = DIFF_HEADS * 2 * HEAD_DIM
LOG2E = math.log2(math.e)
QK_SCALE = HEAD_DIM ** -0.5 * LOG2E

C_LRU_X = 0
C_LRU_G = C_LRU_X + LRU_WIDTH
C_GQ = C_LRU_G + LRU_WIDTH
C_GK = C_GQ + GQA_WIDTH
C_GV = C_GK + GQA_KV_WIDTH
C_GG = C_GV + GQA_KV_WIDTH
C_DQ = C_GG + GQA_WIDTH
C_DK = C_DQ + DIFF_WIDTH
C_DV = C_DK + DIFF_WIDTH
C_DG = C_DV + DIFF_WIDTH
N_IN = C_DG + DIFF_WIDTH
M_LRU = 0
M_GQA = LRU_WIDTH
M_DIFF = LRU_WIDTH + GQA_WIDTH
D_MIX = LRU_WIDTH + GQA_WIDTH + DIFF_WIDTH

LANES = 128
SUBLANES = 8
ROW_CHUNK = 256
KEY_CHUNK = 256
LRU_SPREAD = 7
MOD_ROWS = 16
VMEM_LIMIT = 60 * 1024 * 1024


def _lambda_init(layer):
    return 0.8 - 0.6 * math.exp(-0.3 * layer)


def _dot(a, b):
    return jnp.dot(a, b, preferred_element_type=F32)


def _silu(x):
    return x * jax.nn.sigmoid(x)


def _rms(x, g):
    ms = jnp.mean(x * x, axis=-1, keepdims=True)
    return x * lax.rsqrt(ms + EPS) * g


def _rms_heads_t(t, gain_cols):
    gain = jnp.concatenate([gain_cols] * (t.shape[1] // LANES), axis=1)
    heads = []
    for hh in range(t.shape[0] // HEAD_DIM):
        th = t[hh * HEAD_DIM:(hh + 1) * HEAD_DIM]
        ms = jnp.sum(th * th, axis=0, keepdims=True) * (1.0 / HEAD_DIM)
        heads.append(th * lax.rsqrt(ms + EPS) * gain)
    return jnp.concatenate(heads, axis=0)


def _rms_heads(x, g2):
    lo = lax.broadcasted_iota(jnp.int32, x.shape, 1) < HEAD_DIM
    t = x * x
    s_lo = jnp.sum(jnp.where(lo, t, 0.0), axis=-1, keepdims=True)
    s_hi = jnp.sum(jnp.where(lo, 0.0, t), axis=-1, keepdims=True)
    ms = jnp.where(lo, s_lo, s_hi) * (1.0 / HEAD_DIM)
    return x * lax.rsqrt(ms + EPS) * g2


def _rope(x, cos, sin_signed):
    first = (lax.broadcasted_iota(jnp.int32, x.shape, 1) % 32) < 16
    partner = jnp.where(first, pltpu.roll(x, LANES - 16, 1), pltpu.roll(x, 16, 1))
    return x * cos + partner * sin_signed


def _shift_rows(x, k, n_rows):
    nb = n_rows // SUBLANES
    x3 = x.reshape(nb, SUBLANES, x.shape[-1])
    sub = lax.broadcasted_iota(jnp.int32, x3.shape, 1)
    rolled = pltpu.roll(x3, k % SUBLANES, 1)
    edge = jnp.zeros_like(rolled[:1])
    if k > 0:
        y3 = jnp.where(sub >= k, rolled, jnp.concatenate([edge, rolled[:-1]], axis=0))
    else:
        y3 = jnp.where(sub < SUBLANES + k, rolled, jnp.concatenate([rolled[1:], edge], axis=0))
    return y3.reshape(n_rows, x.shape[-1])


def _linear_scan(a, x, n_rows, reverse, h0=None):
    nb = n_rows // SUBLANES
    lanes = a.shape[-1]
    a3 = a.reshape(nb, SUBLANES, lanes)
    x3 = x.reshape(nb, SUBLANES, lanes)
    sub = lax.broadcasted_iota(jnp.int32, (nb, SUBLANES, lanes), 1)
    k = 1
    while k < SUBLANES:
        keep = (sub < SUBLANES - k) if reverse else (sub >= k)
        shift = SUBLANES - k if reverse else k
        x3 = a3 * jnp.where(keep, pltpu.roll(x3, shift, 1), 0.0) + x3
        a3 = a3 * jnp.where(keep, pltpu.roll(a3, shift, 1), 1.0)
        k *= 2
    blocks = [None] * nb
    carry = h0
    for j in (range(nb - 1, -1, -1) if reverse else range(nb)):
        xb = x3[j]
        if carry is not None:
            xb = xb + a3[j] * carry
        blocks[j] = xb
        carry = xb[0:1] if reverse else xb[SUBLANES - 1:SUBLANES]
    return jnp.concatenate(blocks, axis=0)


_PHASE_END = "phase-end"


def _mod_kernel(cond_ref, w_ref, b_ref, o_ref):
    o_ref[0] = _dot(_silu(cond_ref[...]).astype(BF16), w_ref[0].astype(BF16)) + b_ref[0]


def _layer_kernel(*refs, is_ctx, n_seq, seq_len, past_len, depth, layer0):
    T = seq_len
    Tk = past_len + T
    n_chunks = T // ROW_CHUNK
    it = iter(refs)
    x_ref, mod_ref, gpre_ref, gpost_ref, win_ref, wout_ref = (next(it) for _ in range(6))
    convw_ref, convb_ref, wg_ref, bg_ref, lam_ref = (next(it) for _ in range(5))
    gqc_ref, gk_ref, dlam_ref, gsub_ref = (next(it) for _ in range(4))
    if not is_ctx:
        (ck_ref, cv_ref, cdk_ref, cdv_ref, st_ref, cos_ref, sin_ref, cost_ref, sint_ref) = (
            next(it) for _ in range(9))
    y_ref = next(it)
    if is_ctx:
        ogk_ref, ogv_ref, odk_ref, odv_ref, ost_ref = (next(it) for _ in range(5))
    scratch_refs = [next(it) for _ in range(11)]

    @pl.when(pl.program_id(1) == 0)
    def _():
        y_ref[...] = x_ref[...]

    layer = pl.program_id(1) + layer0

    lam_i = jnp.float32(_lambda_init(depth - 1))
    for lyr in range(depth - 2, -1, -1):
        lam_i = jnp.where(layer == lyr, jnp.float32(_lambda_init(lyr)), lam_i)

    def layer_row(ref):
        return ref[pl.ds(layer, 1), :]

    conv_w, conv_b = convw_ref[layer], layer_row(convb_ref)
    gate_bias, lam_all = bg_ref[layer], lam_ref[layer]
    q_gain_cols, k_gain, sub_gain = gqc_ref[layer], layer_row(gk_ref), layer_row(gsub_ref)
    dl = dlam_ref[layer]

    d_model = gpre_ref.shape[-1]
    mod_row = mod_ref[0, 0:1, :] if is_ctx else mod_ref[0, pl.ds(pl.program_id(0) + 1, 1), :]
    shift = mod_row[:, 0:d_model]
    pre_gain = layer_row(gpre_ref) * (1.0 + mod_row[:, d_model:2 * d_model])
    post_gain = layer_row(gpost_ref) * mod_row[:, 2 * d_model:3 * d_model]
    lam = (jnp.exp(jnp.sum(dl[0:1] * dl[1:2], axis=-1, keepdims=True))
           - jnp.exp(jnp.sum(dl[2:3] * dl[3:4], axis=-1, keepdims=True)) + lam_i)

    def win(c0, n):
        return win_ref[0, :, c0:c0 + n]

    def rows_of(c):
        return pl.ds(c * ROW_CHUNK, ROW_CHUNK)

    def sequence_steps(s):
        (h_scr, mix_scr, zx_scr, qt_scr, k_scr, vt_scr, dqt_scr, dk_scr, dvt_scr, s_scr, e_scr) = (
            ref.at[s] for ref in scratch_refs)

        def prenorm_and_lru_in(c):
            r = rows_of(c)
            hn = _rms(y_ref[s, r, :], pre_gain) + shift
            hc = hn.astype(BF16)
            h_scr[r, :] = hc
            zxg = _dot(hc, win(C_LRU_X, 2 * LRU_WIDTH))
            zx_scr[r, 0:LRU_WIDTH] = zxg[:, 0:LRU_WIDTH]
            zx_scr[r, LRU_WIDTH:] = _silu(zxg[:, LRU_WIDTH:])

        lru_state = {}

        def lru_piece(ct, d):
            cl = slice(ct * LANES, (ct + 1) * LANES)
            reverse = d == 1
            if d == 0:
                zx = zx_scr[:, cl]
                u = conv_b[:, cl] + zx * conv_w[CONV_LEFT:CONV_LEFT + 1, cl]
                for j in range(CONV_W):
                    if j != CONV_LEFT:
                        u = u + _shift_rows(zx, CONV_LEFT - j, T) * conv_w[j:j + 1, cl]
                lru_state[ct] = (u, u.astype(BF16), None)
                yield
            u, u_bf, y_prev = lru_state[ct]
            gates = _dot(u_bf, wg_ref[0, ct, :, d * 2 * LANES:(d + 1) * 2 * LANES])
            gates = gates + gate_bias[ct:ct + 1, d * 2 * LANES:(d + 1) * 2 * LANES]
            r_g = jax.nn.sigmoid(gates[:, 0:LANES])
            i_g = jax.nn.sigmoid(gates[:, LANES:])
            yield
            lam_d = lam_all[d:d + 1, cl]
            sp = jnp.maximum(-lam_d, 0.0) + jnp.log1p(jnp.exp(-jnp.abs(lam_d)))
            a = jnp.exp2(r_g * (sp * (-LRU_C * LOG2E)))
            one_m_a2 = 1.0 - a * a
            root = jnp.where(one_m_a2 > 0.0, one_m_a2 * lax.rsqrt(one_m_a2), 0.0)
            inp = root * (i_g * u)
            yield
            y_d = _linear_scan(a, inp, T, reverse, None if is_ctx else st_ref[0, 0, d:d + 1, cl])
            yield
            if is_ctx:
                last_row = 0 if reverse else T - 1
                ost_ref[s, 0, d:d + 1, cl] = y_d[last_row:last_row + 1, :]
            if d == 0:
                lru_state[ct] = (u, u_bf, y_d)
            else:
                lru_gate = zx_scr[:, LRU_WIDTH + ct * LANES:LRU_WIDTH + (ct + 1) * LANES]
                mix_scr[:, M_LRU + ct * LANES:M_LRU + (ct + 1) * LANES] = (
                    (y_prev + y_d) * lru_gate).astype(BF16)
            yield

        def cached_kv(c):
            r = rows_of(c)
            k_scr[r, :] = ck_ref[0, 0, :, r].T.astype(BF16)
            vt_scr[:, r] = cv_ref[0, 0, :, r].astype(BF16)
            for hd in range(DIFF_HEADS):
                cl = slice(hd * LANES, (hd + 1) * LANES)
                dk_scr[r, cl] = cdk_ref[0, 0, cl, r].T.astype(BF16)
                rows_h = pl.ds(c * ROW_CHUNK * DIFF_HEADS + hd, ROW_CHUNK, stride=DIFF_HEADS)
                dvt_scr[cl, r] = cdv_ref[0, 0, rows_h, :].T.astype(BF16)

        def new_kv(c):
            r = rows_of(c)
            rk = pl.ds(past_len + c * ROW_CHUNK, ROW_CHUNK)
            hc = h_scr[r, :]
            zkv = _dot(hc, win(C_GK, 2 * GQA_KV_WIDTH))
            k = _rms_heads(zkv[:, 0:GQA_KV_WIDTH], k_gain)
            v_t = zkv[:, GQA_KV_WIDTH:].T
            if is_ctx:
                ogk_ref[s, 0, :, r] = k.T
                ogv_ref[s, 0, :, r] = v_t
            else:
                k = _rope(k, cos_ref[r, :], sin_ref[r, :])
            k_scr[rk, :] = k.astype(BF16)
            vt_scr[:, rk] = v_t.astype(BF16)
            yield
            zdk = _dot(hc, win(C_DK, DIFF_WIDTH))
            for hd in range(DIFF_HEADS):
                cl = slice(hd * LANES, (hd + 1) * LANES)
                dk = zdk[:, cl]
                if is_ctx:
                    odk_ref[s, 0, cl, r] = dk.T
                else:
                    dk = _rope(dk, cos_ref[r, :], sin_ref[r, :])
                dk_scr[rk, cl] = dk.astype(BF16)
            yield
            zdv = _dot(hc, win(C_DV, DIFF_WIDTH))
            for hd in range(DIFF_HEADS):
                cl = slice(hd * LANES, (hd + 1) * LANES)
                if is_ctx:
                    rows_h = pl.ds(c * ROW_CHUNK * DIFF_HEADS + hd, ROW_CHUNK, stride=DIFF_HEADS)
                    odv_ref[s, 0, rows_h, :] = zdv[:, cl]
                dvt_scr[cl, rk] = zdv[:, cl].T.astype(BF16)
            yield

        def new_q_piece(cq, g):
            start = cq * ROW_CHUNK
            rq = pl.ds(start if isinstance(cq, int) else pl.multiple_of(start, ROW_CHUNK), ROW_CHUNK)
            c0 = C_GQ if g == 0 else C_DQ + (g - 1) * 2 * LANES
            z = _dot(h_scr[rq, :], win(c0, 2 * LANES))
            for j in range(2):
                t = z[:, j * LANES:(j + 1) * LANES].T
                if g == 0:
                    t = _rms_heads_t(t, q_gain_cols)
                if not is_ctx:
                    half = HEAD_DIM // 4
                    blocks = [t[i * half:(i + 1) * half] for i in range(LANES // half)]
                    partner = jnp.concatenate(
                        [blocks[i + 1 - 2 * (i % 2)] for i in range(len(blocks))], axis=0)
                    t = t * cost_ref[cq] + partner * sint_ref[cq]
                t = (t * QK_SCALE).astype(BF16)
                if g == 0:
                    qt_scr[cq, j * LANES:(j + 1) * LANES, :] = t
                else:
                    hd = (g - 1) * 2 + j
                    dqt_scr[cq, hd * LANES:(hd + 1) * LANES, :] = t

        n_q_pieces = 1 + DIFF_HEADS // 2

        for c in range(n_chunks):
            prenorm_and_lru_in(c)
            yield
        lru_pieces = [(ct, d) for ct in range(LRU_WIDTH // LANES) for d in range(2)]
        lru_with_attention = n_chunks > 1
        n_cached = past_len // ROW_CHUNK
        for i in range(max(len(lru_pieces), n_chunks, n_cached)):
            if i < n_chunks:
                yield from new_kv(i)
            if i < len(lru_pieces) and not lru_with_attention:
                yield from lru_piece(*lru_pieces[i])
            if i < n_cached:
                cached_kv(i)
                yield
        for g in range(n_q_pieces):
            new_q_piece(0, g)
            yield
        yield _PHASE_END

        zero_half = jnp.zeros((HEAD_DIM, ROW_CHUNK), BF16)

        group = GQA_Q_HEADS // GQA_KV_HEADS
        n_pairs = GQA_KV_HEADS + DIFF_HEADS
        n_kc = Tk // KEY_CHUNK

        def pair_qt(cq, p):
            if p < GQA_KV_HEADS:
                heads = [qt_scr[cq, (p * group + i) * HEAD_DIM:(p * group + i + 1) * HEAD_DIM, :]
                         for i in range(group)]
                cols = [jnp.concatenate([q, zero_half] if p == 0 else [zero_half, q], axis=0)
                        for q in heads]
            else:
                hd = p - GQA_KV_HEADS
                q1 = dqt_scr[cq, hd * LANES:hd * LANES + HEAD_DIM, :]
                q2 = dqt_scr[cq, hd * LANES + HEAD_DIM:(hd + 1) * LANES, :]
                cols = [jnp.concatenate([q1, zero_half], axis=0),
                        jnp.concatenate([zero_half, q2], axis=0)]
            return jnp.concatenate(cols, axis=1)

        def attn_steps(c, next_q):
            start = c * ROW_CHUNK
            r = pl.ds(start if isinstance(c, int) else pl.multiple_of(start, ROW_CHUNK), ROW_CHUNK)
            hc = h_scr[r, :]

            def pair_keys(p, i):
                rk = slice(i * KEY_CHUNK, (i + 1) * KEY_CHUNK)
                if p < GQA_KV_HEADS:
                    return k_scr[rk, :]
                hd = p - GQA_KV_HEADS
                return dk_scr[rk, hd * LANES:(hd + 1) * LANES]

            def pair_values_t(p):
                if p < GQA_KV_HEADS:
                    return vt_scr[...]
                hd = p - GQA_KV_HEADS
                return dvt_scr[hd * LANES:(hd + 1) * LANES, :]

            def scores(p, qt, i, m):
                sc = _dot(pair_keys(p, i), qt)
                s_scr[p % 2, i * KEY_CHUNK:(i + 1) * KEY_CHUNK, :] = sc
                mi = jnp.max(sc, axis=0, keepdims=True)
                return mi if m is None else jnp.maximum(m, mi)

            def finish(p, o_t, gate_tile):
                if p < GQA_KV_HEADS:
                    o = jnp.concatenate(
                        [o_t[p * HEAD_DIM:(p + 1) * HEAD_DIM, i * ROW_CHUNK:(i + 1) * ROW_CHUNK]
                         for i in range(group)], axis=0).T
                    col = M_GQA + p * LANES
                else:
                    hd = p - GQA_KV_HEADS
                    o = _rms((o_t[:, 0:ROW_CHUNK] - lam * o_t[:, ROW_CHUNK:]).T, sub_gain) * (1.0 - lam_i)
                    col = M_DIFF + hd * LANES
                mix_scr[r, col:col + LANES] = (o * gate_tile).astype(BF16)

            qt_next = pair_qt(c, 0)
            m_next = None
            for i in range(n_kc):
                m_next = scores(0, qt_next, i, m_next)
            def gate_tile_pair(j):
                c0 = C_GG if j == 0 else C_DG + (j - 1) * 2 * LANES
                return _silu(_dot(hc, win(c0, 2 * LANES)))

            gates_early = n_chunks > 1
            gates = [gate_tile_pair(j) for j in range(n_pairs // 2)] if gates_early else None
            yield
            for p in range(n_pairs):
                m = m_next
                m_next = None
                if p + 1 < n_pairs:
                    qt_next = pair_qt(c, p + 1)
                l = None
                for i in range(n_kc):
                    if p + 1 < n_pairs:
                        m_next = scores(p + 1, qt_next, i, m_next)
                    rk = slice(i * KEY_CHUNK, (i + 1) * KEY_CHUNK)
                    e = jnp.exp2(s_scr[p % 2, rk, :] - m)
                    li = jnp.sum(e, axis=0, keepdims=True)
                    l = li if l is None else l + li
                    e_scr[p % 2, rk, :] = e.astype(BF16)
                    yield
                if next_q and p % 2 == 0 and p // 2 < n_q_pieces:
                    new_q_piece(c + 1, p // 2)
                if p % 2 == 0:
                    gate_pair = gates[p // 2] if gates_early else gate_tile_pair(p // 2)
                finish(p, _dot(pair_values_t(p), e_scr[p % 2]) / l,
                       gate_pair[:, (p % 2) * LANES:(p % 2 + 1) * LANES])
                yield

        if n_chunks == 1:
            yield from attn_steps(0, next_q=False)
        else:
            for c in range(n_chunks):
                side = lru_piece(*lru_pieces[c]) if c < len(lru_pieces) else iter(())
                for n, _ in enumerate(attn_steps(c, next_q=c + 1 < n_chunks)):
                    if n % LRU_SPREAD == 0:
                        next(side, None)
                    yield
                for _ in side:
                    yield
            for extra in lru_pieces[n_chunks:]:
                yield from lru_piece(*extra)

        for c in range(n_chunks):
            r = rows_of(c)
            halves = []
            for j in range(2):
                half_n = wout_ref.shape[-1] // 2
                halves.append(_dot(mix_scr[r, :], wout_ref[0, :, j * half_n:(j + 1) * half_n]))
                yield
            out = jnp.concatenate(halves, axis=1)
            y_ref[s, r, :] = y_ref[s, r, :] + _rms(out, post_gain)
            yield

    def phase_done(gen):
        return next(gen, _PHASE_END) is _PHASE_END

    gens = [sequence_steps(s) for s in range(n_seq)]
    while not phase_done(gens[0]):
        pass
    for s in range(n_seq):
        cur_done = False
        nxt_done = s + 1 >= n_seq
        while not (cur_done and nxt_done):
            if not cur_done:
                cur_done = phase_done(gens[s])
            if not nxt_done:
                nxt_done = phase_done(gens[s + 1])


def _rope_tables(seq_len):
    quarter = HEAD_DIM // 4
    inv = np.power(np.float32(ROPE_BASE), -np.arange(quarter, dtype=np.float32) / quarter)
    t = np.arange(seq_len)
    row = (t // GRID_W).astype(np.float32)[:, None] * inv[None]
    col = (t % GRID_W).astype(np.float32)[:, None] * inv[None]
    cos = np.concatenate([np.cos(row), np.cos(row), np.cos(col), np.cos(col)], axis=-1)
    sin = np.concatenate([-np.sin(row), np.sin(row), -np.sin(col), np.sin(col)], axis=-1)
    reps = LANES // HEAD_DIM
    return (np.tile(cos, (1, reps)).astype(np.float32), np.tile(sin, (1, reps)).astype(np.float32))


def _block_diag(w):
    nb, bw, _ = w.shape
    eye = jnp.eye(nb, dtype=w.dtype)
    return (eye[:, None, :, None] * w[:, :, None, :]).reshape(nb * bw, nb * bw)


def _layer_call(x, mod, weights, extras, *, is_ctx, n_seq, layer0, n_layers):
    batch, seq_len, d_model = x.shape
    depth = weights[2].shape[0]
    past_len = 0 if is_ctx else extras[0].shape[3]
    tk = past_len + seq_len
    n_chunks = seq_len // ROW_CHUNK
    grid = (batch // n_seq, n_layers)
    single = dict(pipeline_mode=pl.Buffered(1))
    w_mode = single if n_layers == 1 else {}

    def per_layer(a):
        nd = a.ndim
        if a.dtype != BF16:
            return pl.BlockSpec(a.shape, lambda b, l: (0,) * nd, **single)
        return pl.BlockSpec((1,) + a.shape[1:], lambda b, l: (l + layer0,) + (0,) * (nd - 1), **w_mode)

    y_spec = pl.BlockSpec((n_seq, seq_len, d_model), lambda b, l: (b, 0, 0))
    x_spec = y_spec
    mod_spec = pl.BlockSpec((1,) + mod.shape[1:], lambda b, l: (l + layer0, 0, 0))
    in_specs = [x_spec, mod_spec] + [per_layer(w) for w in weights]
    args = [x, mod] + list(weights)
    if not is_ctx:
        for a in extras[:5]:
            in_specs.append(
                pl.BlockSpec((1, 1) + a.shape[2:], lambda b, l: (b, l + layer0, 0, 0)))
        for a in extras[5:]:
            in_specs.append(pl.BlockSpec(a.shape, lambda b, l, nd=a.ndim: (0,) * nd, **single))
        args += list(extras)

    out_shape = [jax.ShapeDtypeStruct(x.shape, F32)]
    out_specs = [y_spec]
    if is_ctx:
        for rows_, cols_ in ((GQA_KV_WIDTH, seq_len), (GQA_KV_WIDTH, seq_len), (DIFF_WIDTH, seq_len),
                             (seq_len * DIFF_HEADS, DIFF_WIDTH // DIFF_HEADS)):
            out_shape.append(jax.ShapeDtypeStruct((batch, depth, rows_, cols_), F32))
            out_specs.append(pl.BlockSpec((n_seq, 1, rows_, cols_), lambda b, l: (b, l, 0, 0)))
        out_shape.append(jax.ShapeDtypeStruct((batch, depth, 2, LRU_WIDTH), F32))
        out_specs.append(pl.BlockSpec((n_seq, 1, 2, LRU_WIDTH), lambda b, l: (b, l, 0, 0)))

    scratch = [pltpu.VMEM((n_seq,) + shape, dtype) for shape, dtype in (
        ((seq_len, d_model), BF16),
        ((seq_len, D_MIX), BF16),
        ((seq_len, 2 * LRU_WIDTH), F32),
        ((n_chunks, GQA_WIDTH, ROW_CHUNK), BF16),
        ((tk, GQA_KV_WIDTH), BF16),
        ((GQA_KV_WIDTH, tk), BF16),
        ((n_chunks, DIFF_WIDTH, ROW_CHUNK), BF16),
        ((tk, DIFF_WIDTH), BF16),
        ((DIFF_WIDTH, tk), BF16),
        ((2, tk, 2 * ROW_CHUNK), F32),
        ((2, tk, 2 * ROW_CHUNK), BF16),
    )]
    kern = functools.partial(_layer_kernel, is_ctx=is_ctx, n_seq=n_seq, seq_len=seq_len,
                             past_len=past_len, depth=depth, layer0=layer0)
    return pl.pallas_call(
        kern,
        grid=grid,
        in_specs=in_specs,
        out_specs=out_specs,
        out_shape=out_shape,
        scratch_shapes=scratch,
        compiler_params=pltpu.CompilerParams(
            dimension_semantics=("arbitrary", "arbitrary"),
            vmem_limit_bytes=VMEM_LIMIT),
        name="ctx_pass" if is_ctx else f"denoise_layer{layer0}",
    )(*args)


def kernel(x_prompt, x_sample, cache_gqa_k, cache_gqa_v, cache_diff_k, cache_diff_v, state_lru, c, c_ctx, w_mod, b_mod, g_pre, g_post, w_in, w_out, lru_conv_w, lru_conv_b, lru_wa, lru_ba, lru_wx, lru_bx, lru_lambda, gqa_gq, gqa_gk, diff_lam, diff_gsub):
    depth, d_model, _ = w_in.shape
    dec_batch = x_sample.shape[0]

    cond = jnp.concatenate(
        [c_ctx[None, :], c, jnp.zeros((MOD_ROWS - 1 - dec_batch, d_model), F32)], axis=0)
    n_mod = w_mod.shape[-1]
    mod_tile = n_mod // 3
    mod = pl.pallas_call(
        _mod_kernel,
        grid=(depth, n_mod // mod_tile),
        in_specs=[pl.BlockSpec((MOD_ROWS, d_model), lambda l, j: (0, 0)),
                  pl.BlockSpec((1, d_model, mod_tile), lambda l, j: (l, 0, j)),
                  pl.BlockSpec((1, 1, mod_tile), lambda l, j: (l, 0, j))],
        out_specs=pl.BlockSpec((1, MOD_ROWS, mod_tile), lambda l, j: (l, 0, j)),
        out_shape=jax.ShapeDtypeStruct((depth, MOD_ROWS, n_mod), F32),
        name="adaln_mod",
    )(cond, w_mod, b_mod[:, None, :])

    w_in_p = w_in.astype(BF16)
    w_out_p = w_out.astype(BF16)
    bd = jax.vmap(jax.vmap(_block_diag))
    wa_d, wx_d = bd(lru_wa), bd(lru_wx)
    n_ct = LRU_WIDTH // LANES
    wg = jnp.stack([
        jnp.concatenate([m[:, d, ct * LANES:(ct + 1) * LANES, ct * LANES:(ct + 1) * LANES]
                         for d in range(2) for m in (wa_d, wx_d)], axis=-1)
        for ct in range(n_ct)], axis=1).astype(BF16)
    bg = jnp.stack([
        jnp.concatenate([m[:, d, ct * LANES:(ct + 1) * LANES]
                         for d in range(2) for m in (lru_ba, lru_bx)], axis=-1)
        for ct in range(n_ct)], axis=1)
    reps = LANES // HEAD_DIM
    weights = [g_pre, g_post, w_in_p, w_out_p, lru_conv_w, lru_conv_b, wg, bg, lru_lambda,
               jnp.broadcast_to(gqa_gq[:, :, None], (depth, HEAD_DIM, LANES)),
               jnp.tile(gqa_gk, (1, reps)), diff_lam, diff_gsub]

    y_prompt, gk, gv, dk, dv, st = _layer_call(x_prompt, mod, weights, None, is_ctx=True, n_seq=2,
                                               layer0=0, n_layers=depth)
    b, t = x_prompt.shape[:2]
    new_gqa_k = gk.reshape(b, depth, GQA_KV_HEADS, HEAD_DIM, t).transpose(0, 1, 4, 2, 3)
    new_gqa_v = gv.reshape(b, depth, GQA_KV_HEADS, HEAD_DIM, t).transpose(0, 1, 4, 2, 3)
    new_diff_k = dk.reshape(b, depth, DIFF_HEADS, 2, HEAD_DIM, t).transpose(0, 1, 5, 2, 3, 4)
    new_diff_v = dv.reshape(b, depth, t, DIFF_HEADS, 2 * HEAD_DIM)

    db, _, past = cache_gqa_k.shape[:3]
    cos, sin = _rope_tables(x_sample.shape[1])

    def feature_major_chunks(tab):
        return tab.T.reshape(LANES, -1, ROW_CHUNK).transpose(1, 0, 2)

    extras = [cache_gqa_k.transpose(0, 1, 3, 4, 2).reshape(db, depth, GQA_KV_WIDTH, past),
              cache_gqa_v.transpose(0, 1, 3, 4, 2).reshape(db, depth, GQA_KV_WIDTH, past),
              cache_diff_k.transpose(0, 1, 3, 4, 5, 2).reshape(db, depth, DIFF_WIDTH, past),
              cache_diff_v.reshape(db, depth, past * DIFF_HEADS, 2 * HEAD_DIM),
              state_lru, cos, sin, feature_major_chunks(cos), feature_major_chunks(sin)]
    y_sample = x_sample
    for lyr in range(depth):
        (y_sample,) = _layer_call(y_sample, mod, weights, extras, is_ctx=False, n_seq=1,
                                  layer0=lyr, n_layers=1)

    return (y_prompt, y_sample, new_gqa_k, new_gqa_v, new_diff_k, new_diff_v, st)
```

```python
import functools
import math

import jax
import jax.numpy as jnp
import numpy as np
from jax import lax
from jax.experimental import pallas as pl
from jax.experimental.pallas import tpu as pltpu

F32 = jnp.float32
BF16 = jnp.bfloat16

GRID_W = 64
HEAD_DIM = 64
EPS = 1e-6
ROPE_BASE = 10000.0
LRU_WIDTH = 256
LRU_BLOCKS = 4
LRU_C = 8.0
CONV_W = 4
CONV_LEFT = 2
GQA_Q_HEADS = 4
GQA_KV_HEADS = 2
GQA_WIDTH = GQA_Q_HEADS * HEAD_DIM
GQA_KV_WIDTH = GQA_KV_HEADS * HEAD_DIM
DIFF_HEADS = 4
DIFF_WIDTH = DIFF_HEADS * 2 * HEAD_DIM
LOG2E = math.log2(math.e)
QK_SCALE = HEAD_DIM ** -0.5 * LOG2E

C_LRU_X = 0
C_LRU_G = C_LRU_X + LRU_WIDTH
C_GQ = C_LRU_G + LRU_WIDTH
C_GK = C_GQ + GQA_WIDTH
C_GV = C_GK + GQA_KV_WIDTH
C_GG = C_GV + GQA_KV_WIDTH
C_DQ = C_GG + GQA_WIDTH
C_DK = C_DQ + DIFF_WIDTH
C_DV = C_DK + DIFF_WIDTH
C_DG = C_DV + DIFF_WIDTH
N_IN = C_DG + DIFF_WIDTH
M_LRU = 0
M_GQA = LRU_WIDTH
M_DIFF = LRU_WIDTH + GQA_WIDTH
D_MIX = LRU_WIDTH + GQA_WIDTH + DIFF_WIDTH

LANES = 128
SUBLANES = 8
ROW_CHUNK = 256
KEY_CHUNK = 256
LRU_SPREAD = 4
MOD_ROWS = 16
VMEM_LIMIT = 60 * 1024 * 1024


def _lambda_init(layer):
    return 0.8 - 0.6 * math.exp(-0.3 * layer)


def _dot(a, b):
    return jnp.dot(a, b, preferred_element_type=F32)


def _silu(x):
    return x * jax.nn.sigmoid(x)


def _rms(x, g):
    ms = jnp.mean(x * x, axis=-1, keepdims=True)
    return x * lax.rsqrt(ms + EPS) * g


def _rms_heads_t(t, gain_cols):
    gain = jnp.concatenate([gain_cols] * (t.shape[1] // LANES), axis=1)
    heads = []
    for hh in range(t.shape[0] // HEAD_DIM):
        th = t[hh * HEAD_DIM:(hh + 1) * HEAD_DIM]
        ms = jnp.sum(th * th, axis=0, keepdims=True) * (1.0 / HEAD_DIM)
        heads.append(th * lax.rsqrt(ms + EPS) * gain)
    return jnp.concatenate(heads, axis=0)


def _rms_heads(x, g2):
    lo = lax.broadcasted_iota(jnp.int32, x.shape, 1) < HEAD_DIM
    t = x * x
    s_lo = jnp.sum(jnp.where(lo, t, 0.0), axis=-1, keepdims=True)
    s_hi = jnp.sum(jnp.where(lo, 0.0, t), axis=-1, keepdims=True)
    ms = jnp.where(lo, s_lo, s_hi) * (1.0 / HEAD_DIM)
    return x * lax.rsqrt(ms + EPS) * g2


def _rope(x, cos, sin_signed):
    first = (lax.broadcasted_iota(jnp.int32, x.shape, 1) % 32) < 16
    partner = jnp.where(first, pltpu.roll(x, LANES - 16, 1), pltpu.roll(x, 16, 1))
    return x * cos + partner * sin_signed


def _shift_rows(x, k, n_rows):
    nb = n_rows // SUBLANES
    x3 = x.reshape(nb, SUBLANES, x.shape[-1])
    sub = lax.broadcasted_iota(jnp.int32, x3.shape, 1)
    rolled = pltpu.roll(x3, k % SUBLANES, 1)
    edge = jnp.zeros_like(rolled[:1])
    if k > 0:
        y3 = jnp.where(sub >= k, rolled, jnp.concatenate([edge, rolled[:-1]], axis=0))
    else:
        y3 = jnp.where(sub < SUBLANES + k, rolled, jnp.concatenate([rolled[1:], edge], axis=0))
    return y3.reshape(n_rows, x.shape[-1])


def _linear_scan(a, x, n_rows, reverse, h0=None):
    nb = n_rows // SUBLANES
    lanes = a.shape[-1]
    a3 = a.reshape(nb, SUBLANES, lanes)
    x3 = x.reshape(nb, SUBLANES, lanes)
    sub = lax.broadcasted_iota(jnp.int32, (nb, SUBLANES, lanes), 1)
    k = 1
    while k < SUBLANES:
        keep = (sub < SUBLANES - k) if reverse else (sub >= k)
        shift = SUBLANES - k if reverse else k
        x3 = a3 * jnp.where(keep, pltpu.roll(x3, shift, 1), 0.0) + x3
        a3 = a3 * jnp.where(keep, pltpu.roll(a3, shift, 1), 1.0)
        k *= 2
    blocks = [None] * nb
    carry = h0
    for j in (range(nb - 1, -1, -1) if reverse else range(nb)):
        xb = x3[j]
        if carry is not None:
            xb = xb + a3[j] * carry
        blocks[j] = xb
        carry = xb[0:1] if reverse else xb[SUBLANES - 1:SUBLANES]
    return jnp.concatenate(blocks, axis=0)


_PHASE_END = "phase-end"


def _mod_kernel(cond_ref, w_ref, b_ref, o_ref):
    o_ref[0] = _dot(_silu(cond_ref[...]).astype(BF16), w_ref[0].astype(BF16)) + b_ref[0]


def _layer_kernel(*refs, is_ctx, n_seq, seq_len, past_len, depth, layer0):
    T = seq_len
    Tk = past_len + T
    n_chunks = T // ROW_CHUNK
    it = iter(refs)
    x_ref, mod_ref, gpre_ref, gpost_ref, win_ref, wout_ref = (next(it) for _ in range(6))
    convw_ref, convb_ref, wg_ref, bg_ref, lam_ref = (next(it) for _ in range(5))
    gqc_ref, gk_ref, dlam_ref, gsub_ref = (next(it) for _ in range(4))
    if not is_ctx:
        (ck_ref, cv_ref, cdk_ref, cdv_ref, st_ref, cos_ref, sin_ref, cost_ref, sint_ref) = (
            next(it) for _ in range(9))
    y_ref = next(it)
    if is_ctx:
        ogk_ref, ogv_ref, odk_ref, odv_ref, ost_ref = (next(it) for _ in range(5))
    scratch_refs = [next(it) for _ in range(11)]

    @pl.when(pl.program_id(1) == 0)
    def _():
        y_ref[...] = x_ref[...]

    layer = pl.program_id(1) + layer0

    lam_i = jnp.float32(_lambda_init(depth - 1))
    for lyr in range(depth - 2, -1, -1):
        lam_i = jnp.where(layer == lyr, jnp.float32(_lambda_init(lyr)), lam_i)

    def layer_row(ref):
        return ref[pl.ds(layer, 1), :]

    conv_w, conv_b = convw_ref[layer], layer_row(convb_ref)
    gate_bias, lam_all = bg_ref[layer], lam_ref[layer]
    q_gain_cols, k_gain, sub_gain = gqc_ref[layer], layer_row(gk_ref), layer_row(gsub_ref)
    dl = dlam_ref[layer]

    d_model = gpre_ref.shape[-1]
    mod_row = mod_ref[0, 0:1, :] if is_ctx else mod_ref[0, pl.ds(pl.program_id(0) + 1, 1), :]
    shift = mod_row[:, 0:d_model]
    pre_gain = layer_row(gpre_ref) * (1.0 + mod_row[:, d_model:2 * d_model])
    post_gain = layer_row(gpost_ref) * mod_row[:, 2 * d_model:3 * d_model]
    lam = (jnp.exp(jnp.sum(dl[0:1] * dl[1:2], axis=-1, keepdims=True))
           - jnp.exp(jnp.sum(dl[2:3] * dl[3:4], axis=-1, keepdims=True)) + lam_i)

    def win(c0, n):
        return win_ref[0, :, c0:c0 + n]

    def rows_of(c):
        return pl.ds(c * ROW_CHUNK, ROW_CHUNK)

    def sequence_steps(s):
        (h_scr, mix_scr, zx_scr, qt_scr, k_scr, vt_scr, dqt_scr, dk_scr, dvt_scr, s_scr, e_scr) = (
            ref.at[s] for ref in scratch_refs)

        def prenorm_and_lru_in(c):
            r = rows_of(c)
            hn = _rms(y_ref[s, r, :], pre_gain) + shift
            hc = hn.astype(BF16)
            h_scr[r, :] = hc
            zxg = _dot(hc, win(C_LRU_X, 2 * LRU_WIDTH))
            zx_scr[r, 0:LRU_WIDTH] = zxg[:, 0:LRU_WIDTH]
            zx_scr[r, LRU_WIDTH:] = _silu(zxg[:, LRU_WIDTH:])

        lru_state = {}

        def lru_piece(ct, d):
            cl = slice(ct * LANES, (ct + 1) * LANES)
            reverse = d == 1
            if d == 0:
                zx = zx_scr[:, cl]
                u = conv_b[:, cl] + zx * conv_w[CONV_LEFT:CONV_LEFT + 1, cl]
                for j in range(CONV_W):
                    if j != CONV_LEFT:
                        u = u + _shift_rows(zx, CONV_LEFT - j, T) * conv_w[j:j + 1, cl]
                lru_state[ct] = (u, u.astype(BF16), None)
                yield
            u, u_bf, y_prev = lru_state[ct]
            gates = _dot(u_bf, wg_ref[0, ct, :, d * 2 * LANES:(d + 1) * 2 * LANES])
            gates = gates + gate_bias[ct:ct + 1, d * 2 * LANES:(d + 1) * 2 * LANES]
            r_g = jax.nn.sigmoid(gates[:, 0:LANES])
            i_g = jax.nn.sigmoid(gates[:, LANES:])
            yield
            lam_d = lam_all[d:d + 1, cl]
            sp = jnp.maximum(-lam_d, 0.0) + jnp.log1p(jnp.exp(-jnp.abs(lam_d)))
            a = jnp.exp2(r_g * (sp * (-LRU_C * LOG2E)))
            one_m_a2 = 1.0 - a * a
            root = jnp.where(one_m_a2 > 0.0, one_m_a2 * lax.rsqrt(one_m_a2), 0.0)
            inp = root * (i_g * u)
            yield
            y_d = _linear_scan(a, inp, T, reverse, None if is_ctx else st_ref[0, 0, d:d + 1, cl])
            yield
            if is_ctx:
                last_row = 0 if reverse else T - 1
                ost_ref[s, 0, d:d + 1, cl] = y_d[last_row:last_row + 1, :]
            if d == 0:
                lru_state[ct] = (u, u_bf, y_d)
            else:
                lru_gate = zx_scr[:, LRU_WIDTH + ct * LANES:LRU_WIDTH + (ct + 1) * LANES]
                mix_scr[:, M_LRU + ct * LANES:M_LRU + (ct + 1) * LANES] = (
                    (y_prev + y_d) * lru_gate).astype(BF16)
            yield

        def cached_kv(c):
            r = rows_of(c)
            k_scr[r, :] = ck_ref[0, 0, :, r].T.astype(BF16)
            vt_scr[:, r] = cv_ref[0, 0, :, r].astype(BF16)
            for hd in range(DIFF_HEADS):
                cl = slice(hd * LANES, (hd + 1) * LANES)
                dk_scr[r, cl] = cdk_ref[0, 0, cl, r].T.astype(BF16)
                rows_h = pl.ds(c * ROW_CHUNK * DIFF_HEADS + hd, ROW_CHUNK, stride=DIFF_HEADS)
                dvt_scr[cl, r] = cdv_ref[0, 0, rows_h, :].T.astype(BF16)

        def new_kv(c):
            r = rows_of(c)
            rk = pl.ds(past_len + c * ROW_CHUNK, ROW_CHUNK)
            hc = h_scr[r, :]
            zkv = _dot(hc, win(C_GK, 2 * GQA_KV_WIDTH))
            k = _rms_heads(zkv[:, 0:GQA_KV_WIDTH], k_gain)
            v_t = zkv[:, GQA_KV_WIDTH:].T
            if is_ctx:
                ogk_ref[s, 0, :, r] = k.T
                ogv_ref[s, 0, :, r] = v_t
            else:
                k = _rope(k, cos_ref[r, :], sin_ref[r, :])
            k_scr[rk, :] = k.astype(BF16)
            vt_scr[:, rk] = v_t.astype(BF16)
            yield
            zdk = _dot(hc, win(C_DK, DIFF_WIDTH))
            for hd in range(DIFF_HEADS):
                cl = slice(hd * LANES, (hd + 1) * LANES)
                dk = zdk[:, cl]
                if is_ctx:
                    odk_ref[s, 0, cl, r] = dk.T
                else:
                    dk = _rope(dk, cos_ref[r, :], sin_ref[r, :])
                dk_scr[rk, cl] = dk.astype(BF16)
            yield
            zdv = _dot(hc, win(C_DV, DIFF_WIDTH))
            for hd in range(DIFF_HEADS):
                cl = slice(hd * LANES, (hd + 1) * LANES)
                if is_ctx:
                    rows_h = pl.ds(c * ROW_CHUNK * DIFF_HEADS + hd, ROW_CHUNK, stride=DIFF_HEADS)
                    odv_ref[s, 0, rows_h, :] = zdv[:, cl]
                dvt_scr[cl, rk] = zdv[:, cl].T.astype(BF16)
            yield

        def new_q_piece(cq, g):
            start = cq * ROW_CHUNK
            rq = pl.ds(start if isinstance(cq, int) else pl.multiple_of(start, ROW_CHUNK), ROW_CHUNK)
            c0 = C_GQ if g == 0 else C_DQ + (g - 1) * 2 * LANES
            z = _dot(h_scr[rq, :], win(c0, 2 * LANES))
            for j in range(2):
                t = z[:, j * LANES:(j + 1) * LANES].T
                if g == 0:
                    t = _rms_heads_t(t, q_gain_cols)
                if not is_ctx:
                    half = HEAD_DIM // 4
                    blocks = [t[i * half:(i + 1) * half] for i in range(LANES // half)]
                    partner = jnp.concatenate(
                        [blocks[i + 1 - 2 * (i % 2)] for i in range(len(blocks))], axis=0)
                    t = t * cost_ref[cq] + partner * sint_ref[cq]
                t = (t * QK_SCALE).astype(BF16)
                if g == 0:
                    qt_scr[cq, j * LANES:(j + 1) * LANES, :] = t
                else:
                    hd = (g - 1) * 2 + j
                    dqt_scr[cq, hd * LANES:(hd + 1) * LANES, :] = t

        n_q_pieces = 1 + DIFF_HEADS // 2

        for c in range(n_chunks):
            prenorm_and_lru_in(c)
            yield
        lru_pieces = [(ct, d) for ct in range(LRU_WIDTH // LANES) for d in range(2)]
        lru_with_attention = n_chunks > 1
        n_cached = past_len // ROW_CHUNK
        for i in range(max(len(lru_pieces), n_chunks, n_cached)):
            if i < n_chunks:
                yield from new_kv(i)
            if i < len(lru_pieces) and not lru_with_attention:
                yield from lru_piece(*lru_pieces[i])
            if i < n_cached:
                cached_kv(i)
                yield
        for g in range(n_q_pieces):
            new_q_piece(0, g)
            yield
        yield _PHASE_END

        zero_half = jnp.zeros((HEAD_DIM, ROW_CHUNK), BF16)

        group = GQA_Q_HEADS // GQA_KV_HEADS
        n_pairs = GQA_KV_HEADS + DIFF_HEADS
        n_kc = Tk // KEY_CHUNK

        def pair_qt(cq, p):
            if p < GQA_KV_HEADS:
                heads = [qt_scr[cq, (p * group + i) * HEAD_DIM:(p * group + i + 1) * HEAD_DIM, :]
                         for i in range(group)]
                cols = [jnp.concatenate([q, zero_half] if p == 0 else [zero_half, q], axis=0)
                        for q in heads]
            else:
                hd = p - GQA_KV_HEADS
                q1 = dqt_scr[cq, hd * LANES:hd * LANES + HEAD_DIM, :]
                q2 = dqt_scr[cq, hd * LANES + HEAD_DIM:(hd + 1) * LANES, :]
                cols = [jnp.concatenate([q1, zero_half], axis=0),
                        jnp.concatenate([zero_half, q2], axis=0)]
            return jnp.concatenate(cols, axis=1)

        def attn_steps(c, next_q):
            start = c * ROW_CHUNK
            r = pl.ds(start if isinstance(c, int) else pl.multiple_of(start, ROW_CHUNK), ROW_CHUNK)
            hc = h_scr[r, :]

            def pair_keys(p, i):
                rk = slice(i * KEY_CHUNK, (i + 1) * KEY_CHUNK)
                if p < GQA_KV_HEADS:
                    return k_scr[rk, :]
                hd = p - GQA_KV_HEADS
                return dk_scr[rk, hd * LANES:(hd + 1) * LANES]

            def pair_values_t(p):
                if p < GQA_KV_HEADS:
                    return vt_scr[...]
                hd = p - GQA_KV_HEADS
                return dvt_scr[hd * LANES:(hd + 1) * LANES, :]

            def scores(p, qt, i, m):
                sc = _dot(pair_keys(p, i), qt)
                s_scr[p % 2, i * KEY_CHUNK:(i + 1) * KEY_CHUNK, :] = sc
                mi = jnp.max(sc, axis=0, keepdims=True)
                return mi if m is None else jnp.maximum(m, mi)

            def finish(p, o_t, gate_tile):
                if p < GQA_KV_HEADS:
                    o = jnp.concatenate(
                        [o_t[p * HEAD_DIM:(p + 1) * HEAD_DIM, i * ROW_CHUNK:(i + 1) * ROW_CHUNK]
                         for i in range(group)], axis=0).T
                    col = M_GQA + p * LANES
                else:
                    hd = p - GQA_KV_HEADS
                    o = _rms((o_t[:, 0:ROW_CHUNK] - lam * o_t[:, ROW_CHUNK:]).T, sub_gain) * (1.0 - lam_i)
                    col = M_DIFF + hd * LANES
                mix_scr[r, col:col + LANES] = (o * gate_tile).astype(BF16)

            qt_next = pair_qt(c, 0)
            m_next = None
            for i in range(n_kc):
                m_next = scores(0, qt_next, i, m_next)
            def gate_tile_pair(j):
                c0 = C_GG if j == 0 else C_DG + (j - 1) * 2 * LANES
                return _silu(_dot(hc, win(c0, 2 * LANES)))

            gates_early = n_chunks > 1
            gates = [gate_tile_pair(j) for j in range(n_pairs // 2)] if gates_early else None
            yield
            for p in range(n_pairs):
                m = m_next
                m_next = None
                if p + 1 < n_pairs:
                    qt_next = pair_qt(c, p + 1)
                l = None
                for i in range(n_kc):
                    if p + 1 < n_pairs:
                        m_next = scores(p + 1, qt_next, i, m_next)
                    rk = slice(i * KEY_CHUNK, (i + 1) * KEY_CHUNK)
                    e = jnp.exp2(s_scr[p % 2, rk, :] - m)
                    li = jnp.sum(e, axis=0, keepdims=True)
                    l = li if l is None else l + li
                    e_scr[p % 2, rk, :] = e.astype(BF16)
                    yield
                if next_q and p % 2 == 0 and p // 2 < n_q_pieces:
                    new_q_piece(c + 1, p // 2)
                if p % 2 == 0:
                    gate_pair = gates[p // 2] if gates_early else gate_tile_pair(p // 2)
                finish(p, _dot(pair_values_t(p), e_scr[p % 2]) / l,
                       gate_pair[:, (p % 2) * LANES:(p % 2 + 1) * LANES])
                yield

        if n_chunks == 1:
            yield from attn_steps(0, next_q=False)
        else:
            for c in range(n_chunks):
                side = lru_piece(*lru_pieces[c]) if c < len(lru_pieces) else iter(())
                for n, _ in enumerate(attn_steps(c, next_q=c + 1 < n_chunks)):
                    if n % LRU_SPREAD == 0:
                        next(side, None)
                    yield
                for _ in side:
                    yield
            for extra in lru_pieces[n_chunks:]:
                yield from lru_piece(*extra)

        for c in range(n_chunks):
            r = rows_of(c)
            halves = []
            for j in range(2):
                half_n = wout_ref.shape[-1] // 2
                halves.append(_dot(mix_scr[r, :], wout_ref[0, :, j * half_n:(j + 1) * half_n]))
                yield
            out = jnp.concatenate(halves, axis=1)
            y_ref[s, r, :] = y_ref[s, r, :] + _rms(out, post_gain)
            yield

    def phase_done(gen):
        return next(gen, _PHASE_END) is _PHASE_END

    gens = [sequence_steps(s) for s in range(n_seq)]
    while not phase_done(gens[0]):
        pass
    for s in range(n_seq):
        cur_done = False
        nxt_done = s + 1 >= n_seq
        while not (cur_done and nxt_done):
            if not cur_done:
                cur_done = phase_done(gens[s])
            if not nxt_done:
                nxt_done = phase_done(gens[s + 1])


def _rope_tables(seq_len):
    quarter = HEAD_DIM // 4
    inv = np.power(np.float32(ROPE_BASE), -np.arange(quarter, dtype=np.float32) / quarter)
    t = np.arange(seq_len)
    row = (t // GRID_W).astype(np.float32)[:, None] * inv[None]
    col = (t % GRID_W).astype(np.float32)[:, None] * inv[None]
    cos = np.concatenate([np.cos(row), np.cos(row), np.cos(col), np.cos(col)], axis=-1)
    sin = np.concatenate([-np.sin(row), np.sin(row), -np.sin(col), np.sin(col)], axis=-1)
    reps = LANES // HEAD_DIM
    return (np.tile(cos, (1, reps)).astype(np.float32), np.tile(sin, (1, reps)).astype(np.float32))


def _block_diag(w):
    nb, bw, _ = w.shape
    eye = jnp.eye(nb, dtype=w.dtype)
    return (eye[:, None, :, None] * w[:, :, None, :]).reshape(nb * bw, nb * bw)


def _layer_call(x, mod, weights, extras, *, is_ctx, n_seq, layer0, n_layers):
    batch, seq_len, d_model = x.shape
    depth = weights[2].shape[0]
    past_len = 0 if is_ctx else extras[0].shape[3]
    tk = past_len + seq_len
    n_chunks = seq_len // ROW_CHUNK
    grid = (batch // n_seq, n_layers)
    single = dict(pipeline_mode=pl.Buffered(1))
    w_mode = single if n_layers == 1 else {}

    def per_layer(a):
        nd = a.ndim
        if a.dtype != BF16:
            return pl.BlockSpec(a.shape, lambda b, l: (0,) * nd, **single)
        return pl.BlockSpec((1,) + a.shape[1:], lambda b, l: (l + layer0,) + (0,) * (nd - 1), **w_mode)

    y_spec = pl.BlockSpec((n_seq, seq_len, d_model), lambda b, l: (b, 0, 0))
    x_spec = y_spec
    mod_spec = pl.BlockSpec((1,) + mod.shape[1:], lambda b, l: (l + layer0, 0, 0))
    in_specs = [x_spec, mod_spec] + [per_layer(w) for w in weights]
    args = [x, mod] + list(weights)
    if not is_ctx:
        for a in extras[:5]:
            in_specs.append(
                pl.BlockSpec((1, 1) + a.shape[2:], lambda b, l: (b, l + layer0, 0, 0)))
        for a in extras[5:]:
            in_specs.append(pl.BlockSpec(a.shape, lambda b, l, nd=a.ndim: (0,) * nd, **single))
        args += list(extras)

    out_shape = [jax.ShapeDtypeStruct(x.shape, F32)]
    out_specs = [y_spec]
    if is_ctx:
        for rows_, cols_ in ((GQA_KV_WIDTH, seq_len), (GQA_KV_WIDTH, seq_len), (DIFF_WIDTH, seq_len),
                             (seq_len * DIFF_HEADS, DIFF_WIDTH // DIFF_HEADS)):
            out_shape.append(jax.ShapeDtypeStruct((batch, depth, rows_, cols_), F32))
            out_specs.append(pl.BlockSpec((n_seq, 1, rows_, cols_), lambda b, l: (b, l, 0, 0)))
        out_shape.append(jax.ShapeDtypeStruct((batch, depth, 2, LRU_WIDTH), F32))
        out_specs.append(pl.BlockSpec((n_seq, 1, 2, LRU_WIDTH), lambda b, l: (b, l, 0, 0)))

    scratch = [pltpu.VMEM((n_seq,) + shape, dtype) for shape, dtype in (
        ((seq_len, d_model), BF16),
        ((seq_len, D_MIX), BF16),
        ((seq_len, 2 * LRU_WIDTH), F32),
        ((n_chunks, GQA_WIDTH, ROW_CHUNK), BF16),
        ((tk, GQA_KV_WIDTH), BF16),
        ((GQA_KV_WIDTH, tk), BF16),
        ((n_chunks, DIFF_WIDTH, ROW_CHUNK), BF16),
        ((tk, DIFF_WIDTH), BF16),
        ((DIFF_WIDTH, tk), BF16),
        ((2, tk, 2 * ROW_CHUNK), F32),
        ((2, tk, 2 * ROW_CHUNK), BF16),
    )]
    kern = functools.partial(_layer_kernel, is_ctx=is_ctx, n_seq=n_seq, seq_len=seq_len,
                             past_len=past_len, depth=depth, layer0=layer0)
    return pl.pallas_call(
        kern,
        grid=grid,
        in_specs=in_specs,
        out_specs=out_specs,
        out_shape=out_shape,
        scratch_shapes=scratch,
        compiler_params=pltpu.CompilerParams(
            dimension_semantics=("arbitrary", "arbitrary"),
            vmem_limit_bytes=VMEM_LIMIT),
        name="ctx_pass" if is_ctx else f"denoise_layer{layer0}",
    )(*args)


def kernel(x_prompt, x_sample, cache_gqa_k, cache_gqa_v, cache_diff_k, cache_diff_v, state_lru, c, c_ctx, w_mod, b_mod, g_pre, g_post, w_in, w_out, lru_conv_w, lru_conv_b, lru_wa, lru_ba, lru_wx, lru_bx, lru_lambda, gqa_gq, gqa_gk, diff_lam, diff_gsub):
    depth, d_model, _ = w_in.shape
    dec_batch = x_sample.shape[0]

    cond = jnp.concatenate(
        [c_ctx[None, :], c, jnp.zeros((MOD_ROWS - 1 - dec_batch, d_model), F32)], axis=0)
    n_mod = w_mod.shape[-1]
    mod_tile = n_mod // 3
    mod = pl.pallas_call(
        _mod_kernel,
        grid=(depth, n_mod // mod_tile),
        in_specs=[pl.BlockSpec((MOD_ROWS, d_model), lambda l, j: (0, 0)),
                  pl.BlockSpec((1, d_model, mod_tile), lambda l, j: (l, 0, j)),
                  pl.BlockSpec((1, 1, mod_tile), lambda l, j: (l, 0, j))],
        out_specs=pl.BlockSpec((1, MOD_ROWS, mod_tile), lambda l, j: (l, 0, j)),
        out_shape=jax.ShapeDtypeStruct((depth, MOD_ROWS, n_mod), F32),
        name="adaln_mod",
    )(cond, w_mod, b_mod[:, None, :])

    w_in_p = w_in.astype(BF16)
    w_out_p = w_out.astype(BF16)
    bd = jax.vmap(jax.vmap(_block_diag))
    wa_d, wx_d = bd(lru_wa), bd(lru_wx)
    n_ct = LRU_WIDTH // LANES
    wg = jnp.stack([
        jnp.concatenate([m[:, d, ct * LANES:(ct + 1) * LANES, ct * LANES:(ct + 1) * LANES]
                         for d in range(2) for m in (wa_d, wx_d)], axis=-1)
        for ct in range(n_ct)], axis=1).astype(BF16)
    bg = jnp.stack([
        jnp.concatenate([m[:, d, ct * LANES:(ct + 1) * LANES]
                         for d in range(2) for m in (lru_ba, lru_bx)], axis=-1)
        for ct in range(n_ct)], axis=1)
    reps = LANES // HEAD_DIM
    weights = [g_pre, g_post, w_in_p, w_out_p, lru_conv_w, lru_conv_b, wg, bg, lru_lambda,
               jnp.broadcast_to(gqa_gq[:, :, None], (depth, HEAD_DIM, LANES)),
               jnp.tile(gqa_gk, (1, reps)), diff_lam, diff_gsub]

    y_prompt, gk, gv, dk, dv, st = _layer_call(x_prompt, mod, weights, None, is_ctx=True, n_seq=2,
                                               layer0=0, n_layers=depth)
    b, t = x_prompt.shape[:2]
    new_gqa_k = gk.reshape(b, depth, GQA_KV_HEADS, HEAD_DIM, t).transpose(0, 1, 4, 2, 3)
    new_gqa_v = gv.reshape(b, depth, GQA_KV_HEADS, HEAD_DIM, t).transpose(0, 1, 4, 2, 3)
    new_diff_k = dk.reshape(b, depth, DIFF_HEADS, 2, HEAD_DIM, t).transpose(0, 1, 5, 2, 3, 4)
    new_diff_v = dv.reshape(b, depth, t, DIFF_HEADS, 2 * HEAD_DIM)

    db, _, past = cache_gqa_k.shape[:3]
    cos, sin = _rope_tables(x_sample.shape[1])

    def feature_major_chunks(tab):
        return tab.T.reshape(LANES, -1, ROW_CHUNK).transpose(1, 0, 2)

    extras = [cache_gqa_k.transpose(0, 1, 3, 4, 2).reshape(db, depth, GQA_KV_WIDTH, past),
              cache_gqa_v.transpose(0, 1, 3, 4, 2).reshape(db, depth, GQA_KV_WIDTH, past),
              cache_diff_k.transpose(0, 1, 3, 4, 5, 2).reshape(db, depth, DIFF_WIDTH, past),
              cache_diff_v.reshape(db, depth, past * DIFF_HEADS, 2 * HEAD_DIM),
              state_lru, cos, sin, feature_major_chunks(cos), feature_major_chunks(sin)]
    y_sample = x_sample
    for lyr in range(depth):
        (y_sample,) = _layer_call(y_sample, mod, weights, extras, is_ctx=False, n_seq=1,
                                  layer0=lyr, n_layers=1)

    return (y_prompt, y_sample, new_gqa_k, new_gqa_v, new_diff_k, new_diff_v, st)
```

```python
import functools
import math

import jax
import jax.numpy as jnp
import numpy as np
from jax import lax
from jax.experimental import pallas as pl
from jax.experimental.pallas import tpu as pltpu

F32 = jnp.float32
BF16 = jnp.bfloat16

GRID_W = 64
HEAD_DIM = 64
EPS = 1e-6
ROPE_BASE = 10000.0
LRU_WIDTH = 256
LRU_BLOCKS = 4
LRU_C = 8.0
CONV_W = 4
CONV_LEFT = 2
GQA_Q_HEADS = 4
GQA_KV_HEADS = 2
GQA_WIDTH = GQA_Q_HEADS * HEAD_DIM
GQA_KV_WIDTH = GQA_KV_HEADS * HEAD_DIM
DIFF_HEADS = 4
DIFF_WIDTH = DIFF_HEADS * 2 * HEAD_DIM
LOG2E = math.log2(math.e)
QK_SCALE = HEAD_DIM ** -0.5 * LOG2E

C_LRU_X = 0
C_LRU_G = C_LRU_X + LRU_WIDTH
C_GQ = C_LRU_G + LRU_WIDTH
C_GK = C_GQ + GQA_WIDTH
C_GV = C_GK + GQA_KV_WIDTH
C_GG = C_GV + GQA_KV_WIDTH
C_DQ = C_GG + GQA_WIDTH
C_DK = C_DQ + DIFF_WIDTH
C_DV = C_DK + DIFF_WIDTH
C_DG = C_DV + DIFF_WIDTH
N_IN = C_DG + DIFF_WIDTH
M_LRU = 0
M_GQA = LRU_WIDTH
M_DIFF = LRU_WIDTH + GQA_WIDTH
D_MIX = LRU_WIDTH + GQA_WIDTH + DIFF_WIDTH

LANES = 128
SUBLANES = 8
ROW_CHUNK = 256
KEY_CHUNK = 256
MOD_ROWS = 16
VMEM_LIMIT = 58 * 1024 * 1024


def _lambda_init(layer):
    return 0.8 - 0.6 * math.exp(-0.3 * layer)


def _dot(a, b):
    return jnp.dot(a, b, preferred_element_type=F32)


def _silu(x):
    return x * jax.nn.sigmoid(x)


def _rms(x, g):
    ms = jnp.mean(x * x, axis=-1, keepdims=True)
    return x * lax.rsqrt(ms + EPS) * g


def _rms_heads_t(t, gain_cols):
    gain = jnp.concatenate([gain_cols] * (t.shape[1] // LANES), axis=1)
    heads = []
    for hh in range(t.shape[0] // HEAD_DIM):
        th = t[hh * HEAD_DIM:(hh + 1) * HEAD_DIM]
        ms = jnp.sum(th * th, axis=0, keepdims=True) * (1.0 / HEAD_DIM)
        heads.append(th * lax.rsqrt(ms + EPS) * gain)
    return jnp.concatenate(heads, axis=0)


def _rms_heads(x, g2):
    lo = lax.broadcasted_iota(jnp.int32, x.shape, 1) < HEAD_DIM
    t = x * x
    s_lo = jnp.sum(jnp.where(lo, t, 0.0), axis=-1, keepdims=True)
    s_hi = jnp.sum(jnp.where(lo, 0.0, t), axis=-1, keepdims=True)
    ms = jnp.where(lo, s_lo, s_hi) * (1.0 / HEAD_DIM)
    return x * lax.rsqrt(ms + EPS) * g2


def _rope(x, cos, sin_signed):
    first = (lax.broadcasted_iota(jnp.int32, x.shape, 1) % 32) < 16
    partner = jnp.where(first, pltpu.roll(x, LANES - 16, 1), pltpu.roll(x, 16, 1))
    return x * cos + partner * sin_signed


def _shift_rows(x, k, n_rows):
    nb = n_rows // SUBLANES
    x3 = x.reshape(nb, SUBLANES, x.shape[-1])
    sub = lax.broadcasted_iota(jnp.int32, x3.shape, 1)
    rolled = pltpu.roll(x3, k % SUBLANES, 1)
    edge = jnp.zeros_like(rolled[:1])
    if k > 0:
        y3 = jnp.where(sub >= k, rolled, jnp.concatenate([edge, rolled[:-1]], axis=0))
    else:
        y3 = jnp.where(sub < SUBLANES + k, rolled, jnp.concatenate([rolled[1:], edge], axis=0))
    return y3.reshape(n_rows, x.shape[-1])


def _linear_scan(a, x, n_rows, reverse, h0=None):
    nb = n_rows // SUBLANES
    lanes = a.shape[-1]
    a3 = a.reshape(nb, SUBLANES, lanes)
    x3 = x.reshape(nb, SUBLANES, lanes)
    sub = lax.broadcasted_iota(jnp.int32, (nb, SUBLANES, lanes), 1)
    k = 1
    while k < SUBLANES:
        keep = (sub < SUBLANES - k) if reverse else (sub >= k)
        shift = SUBLANES - k if reverse else k
        x3 = a3 * jnp.where(keep, pltpu.roll(x3, shift, 1), 0.0) + x3
        a3 = a3 * jnp.where(keep, pltpu.roll(a3, shift, 1), 1.0)
        k *= 2
    blocks = [None] * nb
    carry = h0
    for j in (range(nb - 1, -1, -1) if reverse else range(nb)):
        xb = x3[j]
        if carry is not None:
            xb = xb + a3[j] * carry
        blocks[j] = xb
        carry = xb[0:1] if reverse else xb[SUBLANES - 1:SUBLANES]
    return jnp.concatenate(blocks, axis=0)


_PHASE_END = "phase-end"


def _mod_kernel(cond_ref, w_ref, b_ref, o_ref):
    o_ref[0] = _dot(_silu(cond_ref[...]).astype(BF16), w_ref[0].astype(BF16)) + b_ref[0]


def _layer_kernel(*refs, is_ctx, n_seq, seq_len, past_len, depth, layer0):
    T = seq_len
    Tk = past_len + T
    n_chunks = T // ROW_CHUNK
    it = iter(refs)
    x_ref, mod_ref, gpre_ref, gpost_ref, win_ref, wout_ref = (next(it) for _ in range(6))
    convw_ref, convb_ref, wg_ref, bg_ref, lam_ref = (next(it) for _ in range(5))
    gqc_ref, gk_ref, dlam_ref, gsub_ref = (next(it) for _ in range(4))
    if not is_ctx:
        (ck_ref, cv_ref, cdk_ref, cdv_ref, st_ref, cos_ref, sin_ref, cost_ref, sint_ref) = (
            next(it) for _ in range(9))
    y_ref = next(it)
    if is_ctx:
        ogk_ref, ogv_ref, odk_ref, odv_ref, ost_ref = (next(it) for _ in range(5))
    scratch_refs = [next(it) for _ in range(11)]

    @pl.when(pl.program_id(1) == 0)
    def _():
        y_ref[...] = x_ref[...]

    layer = pl.program_id(1) + layer0

    lam_i = jnp.float32(_lambda_init(depth - 1))
    for lyr in range(depth - 2, -1, -1):
        lam_i = jnp.where(layer == lyr, jnp.float32(_lambda_init(lyr)), lam_i)

    def layer_row(ref):
        return ref[pl.ds(layer, 1), :]

    conv_w, conv_b = convw_ref[layer], layer_row(convb_ref)
    gate_bias, lam_all = bg_ref[layer], lam_ref[layer]
    q_gain_cols, k_gain, sub_gain = gqc_ref[layer], layer_row(gk_ref), layer_row(gsub_ref)
    dl = dlam_ref[layer]

    d_model = gpre_ref.shape[-1]
    mod_row = mod_ref[0, 0:1, :] if is_ctx else mod_ref[0, pl.ds(pl.program_id(0) + 1, 1), :]
    shift = mod_row[:, 0:d_model]
    pre_gain = layer_row(gpre_ref) * (1.0 + mod_row[:, d_model:2 * d_model])
    post_gain = layer_row(gpost_ref) * mod_row[:, 2 * d_model:3 * d_model]
    lam = (jnp.exp(jnp.sum(dl[0:1] * dl[1:2], axis=-1, keepdims=True))
           - jnp.exp(jnp.sum(dl[2:3] * dl[3:4], axis=-1, keepdims=True)) + lam_i)

    def win(c0, n):
        return win_ref[0, :, c0:c0 + n]

    def rows_of(c):
        return pl.ds(c * ROW_CHUNK, ROW_CHUNK)

    def sequence_steps(s):
        (h_scr, mix_scr, zx_scr, qt_scr, k_scr, vt_scr, dqt_scr, dk_scr, dvt_scr, s_scr, e_scr) = (
            ref.at[s] for ref in scratch_refs)

        def prenorm_and_lru_in(c):
            r = rows_of(c)
            hn = _rms(y_ref[s, r, :], pre_gain) + shift
            hc = hn.astype(BF16)
            h_scr[r, :] = hc
            zxg = _dot(hc, win(C_LRU_X, 2 * LRU_WIDTH))
            zx_scr[r, 0:LRU_WIDTH] = zxg[:, 0:LRU_WIDTH]
            zx_scr[r, LRU_WIDTH:] = _silu(zxg[:, LRU_WIDTH:])

        lru_state = {}

        def lru_piece(ct, d):
            cl = slice(ct * LANES, (ct + 1) * LANES)
            reverse = d == 1
            if d == 0:
                zx = zx_scr[:, cl]
                u = conv_b[:, cl] + zx * conv_w[CONV_LEFT:CONV_LEFT + 1, cl]
                for j in range(CONV_W):
                    if j != CONV_LEFT:
                        u = u + _shift_rows(zx, CONV_LEFT - j, T) * conv_w[j:j + 1, cl]
                lru_state[ct] = (u, u.astype(BF16), None)
                yield
            u, u_bf, y_prev = lru_state[ct]
            gates = _dot(u_bf, wg_ref[0, ct, :, d * 2 * LANES:(d + 1) * 2 * LANES])
            gates = gates + gate_bias[ct:ct + 1, d * 2 * LANES:(d + 1) * 2 * LANES]
            r_g = jax.nn.sigmoid(gates[:, 0:LANES])
            i_g = jax.nn.sigmoid(gates[:, LANES:])
            yield
            lam_d = lam_all[d:d + 1, cl]
            sp = jnp.maximum(-lam_d, 0.0) + jnp.log1p(jnp.exp(-jnp.abs(lam_d)))
            a = jnp.exp2(r_g * (sp * (-LRU_C * LOG2E)))
            one_m_a2 = 1.0 - a * a
            root = jnp.where(one_m_a2 > 0.0, one_m_a2 * lax.rsqrt(one_m_a2), 0.0)
            inp = root * (i_g * u)
            yield
            y_d = _linear_scan(a, inp, T, reverse, None if is_ctx else st_ref[0, 0, d:d + 1, cl])
            yield
            if is_ctx:
                last_row = 0 if reverse else T - 1
                ost_ref[s, 0, d:d + 1, cl] = y_d[last_row:last_row + 1, :]
            if d == 0:
                lru_state[ct] = (u, u_bf, y_d)
            else:
                lru_gate = zx_scr[:, LRU_WIDTH + ct * LANES:LRU_WIDTH + (ct + 1) * LANES]
                mix_scr[:, M_LRU + ct * LANES:M_LRU + (ct + 1) * LANES] = (
                    (y_prev + y_d) * lru_gate).astype(BF16)
            yield

        def cached_kv(c):
            r = rows_of(c)
            k_scr[r, :] = ck_ref[0, 0, :, r].T.astype(BF16)
            vt_scr[:, r] = cv_ref[0, 0, :, r].astype(BF16)
            for hd in range(DIFF_HEADS):
                cl = slice(hd * LANES, (hd + 1) * LANES)
                dk_scr[r, cl] = cdk_ref[0, 0, cl, r].T.astype(BF16)
                rows_h = pl.ds(c * ROW_CHUNK * DIFF_HEADS + hd, ROW_CHUNK, stride=DIFF_HEADS)
                dvt_scr[cl, r] = cdv_ref[0, 0, rows_h, :].T.astype(BF16)

        def new_kv(c):
            r = rows_of(c)
            rk = pl.ds(past_len + c * ROW_CHUNK, ROW_CHUNK)
            hc = h_scr[r, :]
            zkv = _dot(hc, win(C_GK, 2 * GQA_KV_WIDTH))
            k = _rms_heads(zkv[:, 0:GQA_KV_WIDTH], k_gain)
            v_t = zkv[:, GQA_KV_WIDTH:].T
            if is_ctx:
                ogk_ref[s, 0, :, r] = k.T
                ogv_ref[s, 0, :, r] = v_t
            else:
                k = _rope(k, cos_ref[r, :], sin_ref[r, :])
            k_scr[rk, :] = k.astype(BF16)
            vt_scr[:, rk] = v_t.astype(BF16)
            yield
            zdk = _dot(hc, win(C_DK, DIFF_WIDTH))
            for hd in range(DIFF_HEADS):
                cl = slice(hd * LANES, (hd + 1) * LANES)
                dk = zdk[:, cl]
                if is_ctx:
                    odk_ref[s, 0, cl, r] = dk.T
                else:
                    dk = _rope(dk, cos_ref[r, :], sin_ref[r, :])
                dk_scr[rk, cl] = dk.astype(BF16)
            yield
            zdv = _dot(hc, win(C_DV, DIFF_WIDTH))
            for hd in range(DIFF_HEADS):
                cl = slice(hd * LANES, (hd + 1) * LANES)
                if is_ctx:
                    rows_h = pl.ds(c * ROW_CHUNK * DIFF_HEADS + hd, ROW_CHUNK, stride=DIFF_HEADS)
                    odv_ref[s, 0, rows_h, :] = zdv[:, cl]
                dvt_scr[cl, rk] = zdv[:, cl].T.astype(BF16)
            yield

        def new_q_piece(cq, g):
            start = cq * ROW_CHUNK
            rq = pl.ds(start if isinstance(cq, int) else pl.multiple_of(start, ROW_CHUNK), ROW_CHUNK)
            c0 = C_GQ if g == 0 else C_DQ + (g - 1) * 2 * LANES
            z = _dot(h_scr[rq, :], win(c0, 2 * LANES))
            for j in range(2):
                t = z[:, j * LANES:(j + 1) * LANES].T
                if g == 0:
                    t = _rms_heads_t(t, q_gain_cols)
                if not is_ctx:
                    half = HEAD_DIM // 4
                    blocks = [t[i * half:(i + 1) * half] for i in range(LANES // half)]
                    partner = jnp.concatenate(
                        [blocks[i + 1 - 2 * (i % 2)] for i in range(len(blocks))], axis=0)
                    t = t * cost_ref[cq] + partner * sint_ref[cq]
                t = (t * QK_SCALE).astype(BF16)
                if g == 0:
                    qt_scr[cq, j * LANES:(j + 1) * LANES, :] = t
                else:
                    hd = (g - 1) * 2 + j
                    dqt_scr[cq, hd * LANES:(hd + 1) * LANES, :] = t

        n_q_pieces = 1 + DIFF_HEADS // 2

        for c in range(n_chunks):
            prenorm_and_lru_in(c)
            yield
        lru_pieces = [(ct, d) for ct in range(LRU_WIDTH // LANES) for d in range(2)]
        n_cached = past_len // ROW_CHUNK
        for i in range(max(len(lru_pieces), n_chunks, n_cached)):
            if i < n_chunks:
                yield from new_kv(i)
            if i < len(lru_pieces):
                yield from lru_piece(*lru_pieces[i])
            if i < n_cached:
                cached_kv(i)
                yield
        for g in range(n_q_pieces):
            new_q_piece(0, g)
            yield
        yield _PHASE_END

        zero_half = jnp.zeros((HEAD_DIM, ROW_CHUNK), BF16)

        group = GQA_Q_HEADS // GQA_KV_HEADS
        n_pairs = GQA_KV_HEADS + DIFF_HEADS
        n_kc = Tk // KEY_CHUNK

        def pair_qt(cq, p):
            if p < GQA_KV_HEADS:
                heads = [qt_scr[cq, (p * group + i) * HEAD_DIM:(p * group + i + 1) * HEAD_DIM, :]
                         for i in range(group)]
                cols = [jnp.concatenate([q, zero_half] if p == 0 else [zero_half, q], axis=0)
                        for q in heads]
            else:
                hd = p - GQA_KV_HEADS
                q1 = dqt_scr[cq, hd * LANES:hd * LANES + HEAD_DIM, :]
                q2 = dqt_scr[cq, hd * LANES + HEAD_DIM:(hd + 1) * LANES, :]
                cols = [jnp.concatenate([q1, zero_half], axis=0),
                        jnp.concatenate([zero_half, q2], axis=0)]
            return jnp.concatenate(cols, axis=1)

        def attn_steps(c, next_q):
            start = c * ROW_CHUNK
            r = pl.ds(start if isinstance(c, int) else pl.multiple_of(start, ROW_CHUNK), ROW_CHUNK)
            hc = h_scr[r, :]

            def pair_keys(p, i):
                rk = slice(i * KEY_CHUNK, (i + 1) * KEY_CHUNK)
                if p < GQA_KV_HEADS:
                    return k_scr[rk, :]
                hd = p - GQA_KV_HEADS
                return dk_scr[rk, hd * LANES:(hd + 1) * LANES]

            def pair_values_t(p):
                if p < GQA_KV_HEADS:
                    return vt_scr[...]
                hd = p - GQA_KV_HEADS
                return dvt_scr[hd * LANES:(hd + 1) * LANES, :]

            def scores(p, qt, i, m):
                sc = _dot(pair_keys(p, i), qt)
                s_scr[p % 2, i * KEY_CHUNK:(i + 1) * KEY_CHUNK, :] = sc
                mi = jnp.max(sc, axis=0, keepdims=True)
                return mi if m is None else jnp.maximum(m, mi)

            def finish(p, o_t, gate_tile):
                if p < GQA_KV_HEADS:
                    o = jnp.concatenate(
                        [o_t[p * HEAD_DIM:(p + 1) * HEAD_DIM, i * ROW_CHUNK:(i + 1) * ROW_CHUNK]
                         for i in range(group)], axis=0).T
                    col = M_GQA + p * LANES
                else:
                    hd = p - GQA_KV_HEADS
                    o = _rms((o_t[:, 0:ROW_CHUNK] - lam * o_t[:, ROW_CHUNK:]).T, sub_gain) * (1.0 - lam_i)
                    col = M_DIFF + hd * LANES
                mix_scr[r, col:col + LANES] = (o * gate_tile).astype(BF16)

            qt_next = pair_qt(c, 0)
            m_next = None
            for i in range(n_kc):
                m_next = scores(0, qt_next, i, m_next)
            def gate_tile_pair(j):
                c0 = C_GG if j == 0 else C_DG + (j - 1) * 2 * LANES
                return _silu(_dot(hc, win(c0, 2 * LANES)))

            gates_early = n_chunks > 1
            gates = [gate_tile_pair(j) for j in range(n_pairs // 2)] if gates_early else None
            yield
            for p in range(n_pairs):
                m = m_next
                m_next = None
                if p + 1 < n_pairs:
                    qt_next = pair_qt(c, p + 1)
                l = None
                for i in range(n_kc):
                    if p + 1 < n_pairs:
                        m_next = scores(p + 1, qt_next, i, m_next)
                    rk = slice(i * KEY_CHUNK, (i + 1) * KEY_CHUNK)
                    e = jnp.exp2(s_scr[p % 2, rk, :] - m)
                    li = jnp.sum(e, axis=0, keepdims=True)
                    l = li if l is None else l + li
                    e_scr[p % 2, rk, :] = e.astype(BF16)
                    yield
                if next_q and p % 2 == 0 and p // 2 < n_q_pieces:
                    new_q_piece(jnp.where(c + 1 == n_chunks, 0, c + 1), p // 2)
                if p % 2 == 0:
                    gate_pair = gates[p // 2] if gates_early else gate_tile_pair(p // 2)
                finish(p, _dot(pair_values_t(p), e_scr[p % 2]) / l,
                       gate_pair[:, (p % 2) * LANES:(p % 2 + 1) * LANES])
                yield

        if n_chunks == 1:
            yield from attn_steps(0, next_q=False)
        else:
            def attn_chunk(c, carry):
                for _ in attn_steps(c, next_q=True):
                    pass
                return carry
            lax.fori_loop(0, n_chunks, attn_chunk, 0)
            yield

        for c in range(n_chunks):
            r = rows_of(c)
            halves = []
            for j in range(2):
                half_n = wout_ref.shape[-1] // 2
                halves.append(_dot(mix_scr[r, :], wout_ref[0, :, j * half_n:(j + 1) * half_n]))
                yield
            out = jnp.concatenate(halves, axis=1)
            y_ref[s, r, :] = y_ref[s, r, :] + _rms(out, post_gain)
            yield

    def phase_done(gen):
        return next(gen, _PHASE_END) is _PHASE_END

    gens = [sequence_steps(s) for s in range(n_seq)]
    while not phase_done(gens[0]):
        pass
    for s in range(n_seq):
        cur_done = False
        nxt_done = s + 1 >= n_seq
        while not (cur_done and nxt_done):
            if not cur_done:
                cur_done = phase_done(gens[s])
            if not nxt_done:
                nxt_done = phase_done(gens[s + 1])


def _rope_tables(seq_len):
    quarter = HEAD_DIM // 4
    inv = np.power(np.float32(ROPE_BASE), -np.arange(quarter, dtype=np.float32) / quarter)
    t = np.arange(seq_len)
    row = (t // GRID_W).astype(np.float32)[:, None] * inv[None]
    col = (t % GRID_W).astype(np.float32)[:, None] * inv[None]
    cos = np.concatenate([np.cos(row), np.cos(row), np.cos(col), np.cos(col)], axis=-1)
    sin = np.concatenate([-np.sin(row), np.sin(row), -np.sin(col), np.sin(col)], axis=-1)
    reps = LANES // HEAD_DIM
    return (np.tile(cos, (1, reps)).astype(np.float32), np.tile(sin, (1, reps)).astype(np.float32))


def _block_diag(w):
    nb, bw, _ = w.shape
    eye = jnp.eye(nb, dtype=w.dtype)
    return (eye[:, None, :, None] * w[:, :, None, :]).reshape(nb * bw, nb * bw)


def _layer_call(x, mod, weights, extras, *, is_ctx, n_seq, layer0, n_layers):
    batch, seq_len, d_model = x.shape
    depth = weights[2].shape[0]
    past_len = 0 if is_ctx else extras[0].shape[3]
    tk = past_len + seq_len
    n_chunks = seq_len // ROW_CHUNK
    grid = (batch // n_seq, n_layers)
    single = dict(pipeline_mode=pl.Buffered(1))
    w_mode = single if n_layers == 1 else {}

    def per_layer(a):
        nd = a.ndim
        if a.dtype != BF16:
            return pl.BlockSpec(a.shape, lambda b, l: (0,) * nd, **single)
        return pl.BlockSpec((1,) + a.shape[1:], lambda b, l: (l + layer0,) + (0,) * (nd - 1), **w_mode)

    y_spec = pl.BlockSpec((n_seq, seq_len, d_model), lambda b, l: (b, 0, 0))
    x_spec = y_spec
    mod_spec = pl.BlockSpec((1,) + mod.shape[1:], lambda b, l: (l + layer0, 0, 0))
    in_specs = [x_spec, mod_spec] + [per_layer(w) for w in weights]
    args = [x, mod] + list(weights)
    if not is_ctx:
        for a in extras[:5]:
            in_specs.append(
                pl.BlockSpec((1, 1) + a.shape[2:], lambda b, l: (b, l + layer0, 0, 0)))
        for a in extras[5:]:
            in_specs.append(pl.BlockSpec(a.shape, lambda b, l, nd=a.ndim: (0,) * nd, **single))
        args += list(extras)

    out_shape = [jax.ShapeDtypeStruct(x.shape, F32)]
    out_specs = [y_spec]
    if is_ctx:
        for rows_, cols_ in ((GQA_KV_WIDTH, seq_len), (GQA_KV_WIDTH, seq_len), (DIFF_WIDTH, seq_len),
                             (seq_len * DIFF_HEADS, DIFF_WIDTH // DIFF_HEADS)):
            out_shape.append(jax.ShapeDtypeStruct((batch, depth, rows_, cols_), F32))
            out_specs.append(pl.BlockSpec((n_seq, 1, rows_, cols_), lambda b, l: (b, l, 0, 0)))
        out_shape.append(jax.ShapeDtypeStruct((batch, depth, 2, LRU_WIDTH), F32))
        out_specs.append(pl.BlockSpec((n_seq, 1, 2, LRU_WIDTH), lambda b, l: (b, l, 0, 0)))

    scratch = [pltpu.VMEM((n_seq,) + shape, dtype) for shape, dtype in (
        ((seq_len, d_model), BF16),
        ((seq_len, D_MIX), BF16),
        ((seq_len, 2 * LRU_WIDTH), F32),
        ((n_chunks, GQA_WIDTH, ROW_CHUNK), BF16),
        ((tk, GQA_KV_WIDTH), BF16),
        ((GQA_KV_WIDTH, tk), BF16),
        ((n_chunks, DIFF_WIDTH, ROW_CHUNK), BF16),
        ((tk, DIFF_WIDTH), BF16),
        ((DIFF_WIDTH, tk), BF16),
        ((2, tk, 2 * ROW_CHUNK), F32),
        ((2, tk, 2 * ROW_CHUNK), BF16),
    )]
    kern = functools.partial(_layer_kernel, is_ctx=is_ctx, n_seq=n_seq, seq_len=seq_len,
                             past_len=past_len, depth=depth, layer0=layer0)
    return pl.pallas_call(
        kern,
        grid=grid,
        in_specs=in_specs,
        out_specs=out_specs,
        out_shape=out_shape,
        scratch_shapes=scratch,
        compiler_params=pltpu.CompilerParams(
            dimension_semantics=("parallel", "arbitrary"),
            vmem_limit_bytes=VMEM_LIMIT),
        name="ctx_pass" if is_ctx else f"denoise_layer{layer0}",
    )(*args)


def kernel(x_prompt, x_sample, cache_gqa_k, cache_gqa_v, cache_diff_k, cache_diff_v, state_lru, c, c_ctx, w_mod, b_mod, g_pre, g_post, w_in, w_out, lru_conv_w, lru_conv_b, lru_wa, lru_ba, lru_wx, lru_bx, lru_lambda, gqa_gq, gqa_gk, diff_lam, diff_gsub):
    depth, d_model, _ = w_in.shape
    dec_batch = x_sample.shape[0]

    cond = jnp.concatenate(
        [c_ctx[None, :], c, jnp.zeros((MOD_ROWS - 1 - dec_batch, d_model), F32)], axis=0)
    n_mod = w_mod.shape[-1]
    mod_tile = n_mod // 3
    mod = pl.pallas_call(
        _mod_kernel,
        grid=(depth, n_mod // mod_tile),
        in_specs=[pl.BlockSpec((MOD_ROWS, d_model), lambda l, j: (0, 0)),
                  pl.BlockSpec((1, d_model, mod_tile), lambda l, j: (l, 0, j)),
                  pl.BlockSpec((1, 1, mod_tile), lambda l, j: (l, 0, j))],
        out_specs=pl.BlockSpec((1, MOD_ROWS, mod_tile), lambda l, j: (l, 0, j)),
        out_shape=jax.ShapeDtypeStruct((depth, MOD_ROWS, n_mod), F32),
        name="adaln_mod",
    )(cond, w_mod, b_mod[:, None, :])

    w_in_p = w_in.astype(BF16)
    w_out_p = w_out.astype(BF16)
    bd = jax.vmap(jax.vmap(_block_diag))
    wa_d, wx_d = bd(lru_wa), bd(lru_wx)
    n_ct = LRU_WIDTH // LANES
    wg = jnp.stack([
        jnp.concatenate([m[:, d, ct * LANES:(ct + 1) * LANES, ct * LANES:(ct + 1) * LANES]
                         for d in range(2) for m in (wa_d, wx_d)], axis=-1)
        for ct in range(n_ct)], axis=1).astype(BF16)
    bg = jnp.stack([
        jnp.concatenate([m[:, d, ct * LANES:(ct + 1) * LANES]
                         for d in range(2) for m in (lru_ba, lru_bx)], axis=-1)
        for ct in range(n_ct)], axis=1)
    reps = LANES // HEAD_DIM
    weights = [g_pre, g_post, w_in_p, w_out_p, lru_conv_w, lru_conv_b, wg, bg, lru_lambda,
               jnp.broadcast_to(gqa_gq[:, :, None], (depth, HEAD_DIM, LANES)),
               jnp.tile(gqa_gk, (1, reps)), diff_lam, diff_gsub]

    y_prompt, gk, gv, dk, dv, st = _layer_call(x_prompt, mod, weights, None, is_ctx=True, n_seq=2,
                                               layer0=0, n_layers=depth)
    b, t = x_prompt.shape[:2]
    new_gqa_k = gk.reshape(b, depth, GQA_KV_HEADS, HEAD_DIM, t).transpose(0, 1, 4, 2, 3)
    new_gqa_v = gv.reshape(b, depth, GQA_KV_HEADS, HEAD_DIM, t).transpose(0, 1, 4, 2, 3)
    new_diff_k = dk.reshape(b, depth, DIFF_HEADS, 2, HEAD_DIM, t).transpose(0, 1, 5, 2, 3, 4)
    new_diff_v = dv.reshape(b, depth, t, DIFF_HEADS, 2 * HEAD_DIM)

    db, _, past = cache_gqa_k.shape[:3]
    cos, sin = _rope_tables(x_sample.shape[1])

    def feature_major_chunks(tab):
        return tab.T.reshape(LANES, -1, ROW_CHUNK).transpose(1, 0, 2)

    extras = [cache_gqa_k.transpose(0, 1, 3, 4, 2).reshape(db, depth, GQA_KV_WIDTH, past),
              cache_gqa_v.transpose(0, 1, 3, 4, 2).reshape(db, depth, GQA_KV_WIDTH, past),
              cache_diff_k.transpose(0, 1, 3, 4, 5, 2).reshape(db, depth, DIFF_WIDTH, past),
              cache_diff_v.reshape(db, depth, past * DIFF_HEADS, 2 * HEAD_DIM),
              state_lru, cos, sin, feature_major_chunks(cos), feature_major_chunks(sin)]
    y_sample = x_sample
    for lyr in range(depth):
        (y_sample,) = _layer_call(y_sample, mod, weights, extras, is_ctx=False, n_seq=1,
                                  layer0=lyr, n_layers=1)

    return (y_prompt, y_sample, new_gqa_k, new_gqa_v, new_diff_k, new_diff_v, st)
```
